```python
import jax
import jax.numpy as jnp
from jax import lax
import numpy as np

D_MODEL = 1024
BATCH = 8
SEQ = 2048
DEPTH = 4

N_MIXERS = 2
N_RWKV = (DEPTH + N_MIXERS - 1) // N_MIXERS
N_MLA = DEPTH // N_MIXERS
NORM_EPS = 1e-6

RWKV_HEAD = 64
RWKV_HEADS = D_MODEL // RWKV_HEAD
DECAY_LORA = 64
ICLR_LORA = 64
VRES_LORA = 32
GATE_LORA = 160
GN_EPS = 64e-5

MLA_HEADS = 16
QK_NOPE = 64
QK_ROPE = 32
V_HEAD = 64
Q_LORA = 384
KV_LORA = 256
ROPE_THETA = 10000.0
Q_BLOCK = 128

N_EXPERTS = 16
EXPERT_FF = 1024
EC_CAPACITY = 2

kernel_name = 'hybrid_rwkv7_mla_ecmoe_encoder'


def rms_norm(x, g, eps=NORM_EPS):
    xf = x.astype(jnp.float32)
    y = xf * lax.rsqrt(jnp.mean(xf * xf, -1, keepdims=True) + eps)
    return (y * g.astype(jnp.float32)).astype(x.dtype)


def centred_shift(x):
    zero = jnp.zeros_like(x[:, :1])
    prev = jnp.concatenate([zero, x[:, :-1]], axis=1)
    nxt = jnp.concatenate([x[:, 1:], zero], axis=1)
    return 0.5 * (prev + nxt)


def wkv7_scan(r, w, k, v, a, b, reverse):
    bsz, _, nh, hd = r.shape

    def step(state, inp):
        r_t, w_t, k_t, v_t, a_t, b_t = inp
        sa = jnp.einsum('bhij,bhj->bhi', state, a_t)
        state = (state * w_t[:, :, None, :] + sa[..., None] * b_t[:, :, None, :]
                 + v_t[..., None] * k_t[:, :, None, :])
        y_t = jnp.einsum('bhij,bhj->bhi', state, r_t)
        return state, y_t

    xs = tuple(jnp.moveaxis(t, 1, 0) for t in (r, w, k, v, a, b))
    s0 = jnp.zeros((bsz, nh, hd, hd), jnp.float32)
    _, ys = lax.scan(step, s0, xs, reverse=reverse)
    return jnp.moveaxis(ys, 0, 1)


def rwkv7_time_mix(h, v_first, mu, w_r, w_k, w_v, w_o, w0, w1, w2, a0, a1, a2,
                   g1, g2, k_k, k_a, r_k, gn_g, gn_b, vres):
    b, s, d = h.shape
    heads = lambda t: t.reshape(b, s, RWKV_HEADS, RWKV_HEAD)
    xx = centred_shift(h) - h
    xr, xw, xk, xv, xa, xg = [h + xx * mu[i] for i in range(6)]
    r = xr @ w_r
    k = xk @ w_k
    v = xv @ w_v
    if vres is None:
        v_first = v
    else:
        v0, v1, v2 = vres
        v = v + (v_first - v) * jax.nn.sigmoid(v0 + (xv @ v1) @ v2)
    g = jax.nn.sigmoid(xg @ g1) @ g2
    kk = heads((k * k_k).astype(jnp.float32))
    kk = kk / jnp.maximum(jnp.sqrt(jnp.sum(kk * kk, -1, keepdims=True)), 1e-12)
    r_h = heads(r).astype(jnp.float32)
    v_h = heads(v).astype(jnp.float32)
    r_k32 = r_k.astype(jnp.float32)
    y = jnp.zeros_like(r_h)
    bonus = jnp.zeros_like(r_h)
    for direction in range(2):
        w_log = -jax.nn.softplus(-(w0[direction] + jnp.tanh(xw @ w1[direction]) @ w2[direction])) - 0.5
        decay = jnp.exp(-jnp.exp(heads(w_log).astype(jnp.float32)))
        a = jax.nn.sigmoid(a0[direction] + (xa @ a1[direction]) @ a2[direction])
        k_d = heads(k * (1.0 + (a - 1.0) * k_a)).astype(jnp.float32)
        a_h = heads(a).astype(jnp.float32)
        y = y + wkv7_scan(r_h, decay, k_d, v_h, -kk, kk * a_h, reverse=(direction == 1))
        bonus = bonus + jnp.sum(r_h * k_d * r_k32, -1, keepdims=True) * v_h
    mean = jnp.mean(y, -1, keepdims=True)
    var = jnp.mean(jnp.square(y - mean), -1, keepdims=True)
    y = ((y - mean) * lax.rsqrt(var + GN_EPS)).reshape(b, s, d)
    y = y * gn_g.astype(jnp.float32) + gn_b.astype(jnp.float32) + bonus.reshape(b, s, d)
    out = (y.astype(h.dtype) * g) @ w_o
    return out, v_first


def rope_tables(positions):
    inv_freq = ROPE_THETA ** (-jnp.arange(0, QK_ROPE, 2, dtype=jnp.float32) / QK_ROPE)
    ang = positions.astype(jnp.float32)[..., None] * inv_freq
    return jnp.cos(ang)[:, :, None, :], jnp.sin(ang)[:, :, None, :]


def rope_tail(x, cos, sin):
    x_nope, x_rope = x[..., :QK_NOPE], x[..., QK_NOPE:]
    x1, x2 = jnp.split(x_rope.astype(jnp.float32), 2, axis=-1)
    rot = jnp.concatenate([x1 * cos - x2 * sin, x2 * cos + x1 * sin], -1).astype(x.dtype)
    return jnp.concatenate([x_nope, rot], -1)


def block_attention(q, k, v):
    b, s, nh, dq = q.shape
    scale = dq ** -0.5
    n_blk = s // Q_BLOCK
    qb = jnp.moveaxis(q.reshape(b, n_blk, Q_BLOCK, nh, dq), 1, 0)

    def one_block(q_blk):
        sc = jnp.einsum('bqhd,bkhd->bhqk', q_blk, k).astype(jnp.float32) * scale
        p = jax.nn.softmax(sc, axis=-1).astype(v.dtype)
        return jnp.einsum('bhqk,bkhd->bqhd', p, v)

    ob = lax.map(one_block, qb)
    return jnp.moveaxis(ob, 0, 1).reshape(b, s, nh, v.shape[-1])


def mla_mix(h, cos, sin, w_in, q_norm_g, kv_norm_g, w_uq, w_ukv, q_head_g, k_head_g, w_o):
    b, s, _ = h.shape
    c = h @ w_in
    c_q, c_kv, k_rope = jnp.split(c, [Q_LORA, Q_LORA + KV_LORA], axis=-1)
    q = (rms_norm(c_q, q_norm_g) @ w_uq).reshape(b, s, MLA_HEADS, QK_NOPE + QK_ROPE)
    kv = (rms_norm(c_kv, kv_norm_g) @ w_ukv).reshape(b, s, MLA_HEADS, QK_NOPE + V_HEAD)
    k_nope, v = kv[..., :QK_NOPE], kv[..., QK_NOPE:]
    k = jnp.concatenate(
        [k_nope, jnp.broadcast_to(k_rope[:, :, None, :], (b, s, MLA_HEADS, QK_ROPE))], -1)
    q = rope_tail(rms_norm(q, q_head_g), cos, sin)
    k = rope_tail(rms_norm(k, k_head_g), cos, sin)
    o = block_attention(q, k, v)
    return o.reshape(b, s, MLA_HEADS * V_HEAD) @ w_o


def expert_choice_ffn(h, w_router, w_gate, w_up, w_down):
    b, s, _ = h.shape
    cap = EC_CAPACITY * s // N_EXPERTS
    aff = jax.nn.softmax((h @ w_router).astype(jnp.float32), axis=-1)
    gates, idx = lax.top_k(jnp.swapaxes(aff, 1, 2), cap)
    b_idx = jnp.arange(b)[:, None, None]
    xe = h[b_idx, idx]
    hid = (jax.nn.silu(jnp.einsum('becd,edf->becf', xe, w_gate))
           * jnp.einsum('becd,edf->becf', xe, w_up))
    ye = jnp.einsum('becf,efd->becd', hid, w_down) * gates[..., None].astype(h.dtype)
    return jnp.zeros_like(h).at[b_idx, idx].add(ye)


def setup_inputs(seed: int = 0) -> dict:
    key = jax.random.key(seed)
    ks = iter(jax.random.split(key, 40))

    def nrm(shape, scale):
        return jax.random.normal(next(ks), shape, jnp.float32) * scale

    def gain(shape):
        return 1.0 + nrm(shape, 0.02)

    D, L, R, M = D_MODEL, DEPTH, N_RWKV, N_MLA
    return {
        'x': nrm((BATCH, SEQ, D), 1.0),
        'positions': (jnp.arange(SEQ, dtype=jnp.int32)[None, :]
                      + jax.random.randint(next(ks), (BATCH, 1), 0, 4096, dtype=jnp.int32)),
        'norm_mix_g': gain((L, D)),
        'norm_ffn_g': gain((L, D)),
        'rw_mu': jax.random.uniform(next(ks), (R, 6, D), jnp.float32),
        'rw_wr': nrm((R, D, D), D ** -0.5),
        'rw_wk': nrm((R, D, D), D ** -0.5),
        'rw_wv': nrm((R, D, D), D ** -0.5),
        'rw_wo': nrm((R, D, D), D ** -0.5),
        'rw_w0': -0.5 + nrm((R, 2, D), 0.5),
        'rw_w1': nrm((R, 2, D, DECAY_LORA), D ** -0.5),
        'rw_w2': nrm((R, 2, DECAY_LORA, D), 0.5 * DECAY_LORA ** -0.5),
        'rw_a0': nrm((R, 2, D), 0.1),
        'rw_a1': nrm((R, 2, D, ICLR_LORA), D ** -0.5),
        'rw_a2': nrm((R, 2, ICLR_LORA, D), 0.5 * ICLR_LORA ** -0.5),
        'rw_g1': nrm((R, D, GATE_LORA), D ** -0.5),
        'rw_g2': nrm((R, GATE_LORA, D), GATE_LORA ** -0.5),
        'rw_kk': 0.85 + nrm((R, D), 0.05),
        'rw_ka': 1.0 + nrm((R, D), 0.05),
        'rw_rk': nrm((R, RWKV_HEADS, RWKV_HEAD), 0.1),
        'rw_gn_g': gain((R, D)),
        'rw_gn_b': nrm((R, D), 0.02),
        'rw_v0': nrm((R - 1, D), 0.1),
        'rw_v1': nrm((R - 1, D, VRES_LORA), D ** -0.5),
        'rw_v2': nrm((R - 1, VRES_LORA, D), 0.5 * VRES_LORA ** -0.5),
        'mla_w_in': nrm((M, D, Q_LORA + KV_LORA + QK_ROPE), D ** -0.5),
        'mla_q_norm_g': gain((M, Q_LORA)),
        'mla_kv_norm_g': gain((M, KV_LORA)),
        'mla_w_uq': nrm((M, Q_LORA, MLA_HEADS * (QK_NOPE + QK_ROPE)), Q_LORA ** -0.5),
        'mla_w_ukv': nrm((M, KV_LORA, MLA_HEADS * (QK_NOPE + V_HEAD)), KV_LORA ** -0.5),
        'mla_q_head_g': gain((M, QK_NOPE + QK_ROPE)),
        'mla_k_head_g': gain((M, QK_NOPE + QK_ROPE)),
        'mla_w_o': nrm((M, MLA_HEADS * V_HEAD, D), (MLA_HEADS * V_HEAD) ** -0.5),
        'moe_router': nrm((L, D, N_EXPERTS), D ** -0.5),
        'moe_w_gate': nrm((L, N_EXPERTS, D, EXPERT_FF), D ** -0.5),
        'moe_w_up': nrm((L, N_EXPERTS, D, EXPERT_FF), D ** -0.5),
        'moe_w_down': nrm((L, N_EXPERTS, EXPERT_FF, D), EXPERT_FF ** -0.5),
    }


def reference(x, positions, norm_mix_g, norm_ffn_g,
              rw_mu, rw_wr, rw_wk, rw_wv, rw_wo, rw_w0, rw_w1, rw_w2,
              rw_a0, rw_a1, rw_a2, rw_g1, rw_g2, rw_kk, rw_ka, rw_rk,
              rw_gn_g, rw_gn_b, rw_v0, rw_v1, rw_v2,
              mla_w_in, mla_q_norm_g, mla_kv_norm_g, mla_w_uq, mla_w_ukv,
              mla_q_head_g, mla_k_head_g, mla_w_o,
              moe_router, moe_w_gate, moe_w_up, moe_w_down):
    cos, sin = rope_tables(positions)
    v_first = None
    for i in range(DEPTH):
        h = rms_norm(x, norm_mix_g[i])
        j = i // N_MIXERS
        if i % N_MIXERS == 0:
            vres = None if j == 0 else (rw_v0[j - 1], rw_v1[j - 1], rw_v2[j - 1])
            mix, v_first = rwkv7_time_mix(
                h, v_first, rw_mu[j], rw_wr[j], rw_wk[j], rw_wv[j], rw_wo[j],
                rw_w0[j], rw_w1[j], rw_w2[j], rw_a0[j], rw_a1[j], rw_a2[j],
                rw_g1[j], rw_g2[j], rw_kk[j], rw_ka[j], rw_rk[j],
                rw_gn_g[j], rw_gn_b[j], vres)
        else:
            mix = mla_mix(h, cos, sin, mla_w_in[j], mla_q_norm_g[j], mla_kv_norm_g[j],
                          mla_w_uq[j], mla_w_ukv[j], mla_q_head_g[j], mla_k_head_g[j],
                          mla_w_o[j])
        x = x + mix
        x = x + expert_choice_ffn(rms_norm(x, norm_ffn_g[i]), moe_router[i],
                                  moe_w_gate[i], moe_w_up[i], moe_w_down[i])
    return x
```

```python
import functools

import jax
import jax.numpy as jnp
from jax import lax
from jax.experimental import pallas as pl
from jax.experimental.pallas import tpu as pltpu

F32 = jnp.float32
BF16 = jnp.bfloat16

NORM_EPS = 1e-6
GN_EPS = 64e-5
RWKV_HEAD = 64
QK_NOPE = 64
QK_ROPE = 32
V_HEAD = 64
Q_LORA = 384
KV_LORA = 256
ROPE_THETA = 10000.0
EC_CAPACITY = 2
HEAD_PAD = 128
WKV_CHUNK = 64
VMEM_LIMIT = 56 * 1024 * 1024

NT_DIMS = (((1,), (1,)), ((), ()))
TN_DIMS = (((0,), (0,)), ((), ()))


def _dot(a, b):
    return jnp.dot(a.astype(BF16), b.astype(BF16), preferred_element_type=F32)


def _dot_nt(a, b):
    return lax.dot_general(a.astype(BF16), b.astype(BF16), NT_DIMS, preferred_element_type=F32)


def _dot_tn(a, b):
    return lax.dot_general(a.astype(BF16), b.astype(BF16), TN_DIMS, preferred_element_type=F32)


def _split_bf16(x):
    hi = x.astype(BF16)
    lo = (x - hi.astype(F32)).astype(BF16)
    return hi, lo


def _dot3(a, b):
    a_hi, a_lo = _split_bf16(a)
    b_hi, b_lo = _split_bf16(b)
    f = lambda x, y: jnp.dot(x, y, preferred_element_type=F32)
    return f(a_hi, b_hi) + f(a_hi, b_lo) + f(a_lo, b_hi)


def _mm_kernel(x_ref, w_ref, o_ref):
    o_ref[...] = _dot(x_ref[...], w_ref[...]).astype(o_ref.dtype)


def matmul(x, w, out_dtype=F32, bm=512):
    m, k = x.shape
    n = w.shape[1]
    bm = min(bm, m)
    bn = n if n <= 1024 else 1024
    assert m % bm == 0 and n % bn == 0
    return pl.pallas_call(
        _mm_kernel,
        out_shape=jax.ShapeDtypeStruct((m, n), out_dtype),
        grid=(n // bn, m // bm),
        in_specs=[pl.BlockSpec((bm, k), lambda j, i: (i, 0)),
                  pl.BlockSpec((k, bn), lambda j, i: (0, j))],
        out_specs=pl.BlockSpec((bm, bn), lambda j, i: (i, j)),
        compiler_params=pltpu.CompilerParams(
            dimension_semantics=("parallel", "parallel"), vmem_limit_bytes=VMEM_LIMIT),
        name="matmul",
    )(x, w)


def _wkv_kernel(r_ref, lw_ref, k_ref, v_ref, a_ref, b_ref, y_ref, h_scr, *, reverse, nh, hd):
    c = pl.program_id(1)

    @pl.when(c == 0)
    def _():
        h_scr[...] = jnp.zeros_like(h_scr)

    lw = lw_ref[0]
    L = lw.shape[0]
    row = lax.broadcasted_iota(jnp.int32, (L, L), 0)
    col = lax.broadcasted_iota(jnp.int32, (L, L), 1)
    if reverse:
        incl, strict = col >= row, col > row
    else:
        incl, strict = col <= row, col < row
    tri = jnp.where(incl, 1.0, 0.0).astype(BF16)
    eye = jnp.where(row == col, 1.0, 0.0).astype(F32)

    lw_hi, lw_lo = _split_bf16(lw)
    cs = (jnp.dot(tri, lw_hi, preferred_element_type=F32)
          + jnp.dot(tri, lw_lo, preferred_element_type=F32))
    last = 0 if reverse else L - 1
    ctot = cs[last:last + 1, :]

    r, k, v, a, b = r_ref[0], k_ref[0], v_ref[0], a_ref[0], b_ref[0]
    rt = r * jnp.exp(cs)
    at = a * jnp.exp(cs - lw)
    einv = jnp.exp(-cs)
    bt = b * einv
    kt = k * einv
    etot = jnp.exp(ctot - cs)
    bb = b * etot
    kb = k * etot
    wtot = jnp.exp(ctot)

    n_double = max(L.bit_length() - 2, 0)
    for h in range(nh):
        sl = slice(h * hd, (h + 1) * hd)
        at_h, rt_h, bt_h, kt_h, v_h = at[:, sl], rt[:, sl], bt[:, sl], kt[:, sl], v[:, sl]
        a_ab = jnp.where(strict, _dot_nt(at_h, bt_h), 0.0)
        a_ak = jnp.where(strict, _dot_nt(at_h, kt_h), 0.0)
        a_rb = jnp.where(incl, _dot_nt(rt_h, bt_h), 0.0)
        a_rk = jnp.where(incl, _dot_nt(rt_h, kt_h), 0.0)
        t = eye + a_ab
        p = a_ab
        for _ in range(n_double):
            p = _dot3(p, p)
            t = t + _dot3(p, t)
        ah = _dot(t, at_h)
        vh = _dot(t, _dot(a_ak, v_h))
        rh = rt_h + _dot(a_rb, ah)
        yh = _dot(a_rb, vh) + _dot(a_rk, v_h)
        bb_h, kb_h = bb[:, sl], kb[:, sl]
        m = _dot_tn(bb_h, ah) + eye[:hd, :hd] * wtot[:, sl]
        g = _dot_tn(bb_h, vh) + _dot_tn(kb_h, v_h)
        hs = h_scr[h]
        y_ref[0, :, sl] = _dot(rh, hs) + yh
        h_scr[h] = _dot(m, hs) + g


def wkv7(r, lw, k, v, a, b, reverse):
    bsz, s, d = r.shape
    hd = RWKV_HEAD
    nh = d // hd
    L = min(WKV_CHUNK, s)
    assert L >= hd and s % L == 0
    nc = s // L
    if reverse:
        idx = lambda bi, ci: (bi, nc - 1 - ci, 0)
    else:
        idx = lambda bi, ci: (bi, ci, 0)
    spec = pl.BlockSpec((1, L, d), idx)
    return pl.pallas_call(
        functools.partial(_wkv_kernel, reverse=reverse, nh=nh, hd=hd),
        out_shape=jax.ShapeDtypeStruct((bsz, s, d), F32),
        grid=(bsz, nc),
        in_specs=[spec] * 6,
        out_specs=spec,
        scratch_shapes=[pltpu.VMEM((nh, hd, hd), F32)],
        compiler_params=pltpu.CompilerParams(
            dimension_semantics=("parallel", "arbitrary"), vmem_limit_bytes=VMEM_LIMIT),
        name="wkv7_rev" if reverse else "wkv7_fwd",
    )(r, lw, k, v, a, b)


def _headproj_kernel(x_ref, w_ref, g_ref, c_ref, sa_ref, sb_ref, o_ref, *, nh, real_dim):
    y = _dot(x_ref[...], w_ref[...])
    cf, sa, sb = c_ref[...], sa_ref[...], sb_ref[...]
    g = g_ref[...]
    half = QK_ROPE // 2
    for h in range(nh):
        sl = slice(h * HEAD_PAD, (h + 1) * HEAD_PAD)
        yh = y[:, sl]
        ms = jnp.sum(yh * yh, axis=-1, keepdims=True) * (1.0 / real_dim)
        yn = yh * lax.rsqrt(ms + NORM_EPS) * g
        rot = (yn * cf + pltpu.roll(yn, HEAD_PAD - half, 1) * sa + pltpu.roll(yn, half, 1) * sb)
        o_ref[:, sl] = rot.astype(o_ref.dtype)


def head_proj(x, w_pad, gain_pad, cf, sa, sb, nh, bm=256):
    m, k = x.shape
    n = nh * HEAD_PAD
    bm = min(bm, m)
    tab = pl.BlockSpec((bm, HEAD_PAD), lambda i: (i, 0))
    return pl.pallas_call(
        functools.partial(_headproj_kernel, nh=nh, real_dim=QK_NOPE + QK_ROPE),
        out_shape=jax.ShapeDtypeStruct((m, n), BF16),
        grid=(m // bm,),
        in_specs=[pl.BlockSpec((bm, k), lambda i: (i, 0)),
                  pl.BlockSpec((k, n), lambda i: (0, 0)),
                  pl.BlockSpec((1, HEAD_PAD), lambda i: (0, 0)),
                  tab, tab, tab],
        out_specs=pl.BlockSpec((bm, n), lambda i: (i, 0)),
        compiler_params=pltpu.CompilerParams(
            dimension_semantics=("parallel",), vmem_limit_bytes=VMEM_LIMIT),
        name="head_proj",
    )(x, w_pad, gain_pad, cf, sa, sb)


def _attn_kernel(q_ref, k_ref, v_ref, o_ref):
    q, k, v = q_ref[0], k_ref[0], v_ref[0]
    lane = lax.broadcasted_iota(jnp.int32, (1, 2 * V_HEAD), 1)
    acc = jnp.zeros((q.shape[0], 2 * V_HEAD), F32)
    for j in range(2):
        sl = slice(j * HEAD_PAD, (j + 1) * HEAD_PAD)
        s = _dot_nt(q[:, sl], k[:, sl])
        p = jnp.exp(s - jnp.max(s, axis=-1, keepdims=True))
        den = jnp.sum(p, axis=-1, keepdims=True)
        vj = jnp.where((lane >= j * V_HEAD) & (lane < (j + 1) * V_HEAD), v, jnp.zeros_like(v))
        acc = acc + _dot(p, vj) / den
    o_ref[0] = acc.astype(o_ref.dtype)


def attention(q, k, v, tq=256):
    bsz, s, _ = q.shape
    nh = v.shape[-1] // V_HEAD
    tq = min(tq, s)
    return pl.pallas_call(
        _attn_kernel,
        out_shape=jax.ShapeDtypeStruct((bsz, s, nh * V_HEAD), BF16),
        grid=(bsz, nh // 2, s // tq),
        in_specs=[pl.BlockSpec((1, tq, 2 * HEAD_PAD), lambda b, h, i: (b, i, h)),
                  pl.BlockSpec((1, s, 2 * HEAD_PAD), lambda b, h, i: (b, 0, h)),
                  pl.BlockSpec((1, s, 2 * V_HEAD), lambda b, h, i: (b, 0, h))],
        out_specs=pl.BlockSpec((1, tq, 2 * V_HEAD), lambda b, h, i: (b, i, h)),
        compiler_params=pltpu.CompilerParams(
            dimension_semantics=("parallel", "parallel", "parallel"), vmem_limit_bytes=VMEM_LIMIT),
        name="mla_attention",
    )(q, k, v)


def _route_kernel(h_ref, wr_ref, pos_ref, gate_ref, *, cap):
    h = h_ref[0]
    wr = wr_ref[...]
    s = h.shape[0]
    ne = wr.shape[0]
    h_hi, h_lo = _split_bf16(h)
    w_hi, w_lo = _split_bf16(wr)
    logits = (lax.dot_general(w_hi, h_hi, NT_DIMS, preferred_element_type=F32)
              + lax.dot_general(w_hi, h_lo, NT_DIMS, preferred_element_type=F32)
              + lax.dot_general(w_lo, h_hi, NT_DIMS, preferred_element_type=F32))
    ex = jnp.exp(logits - jnp.max(logits, axis=0, keepdims=True))
    aff = ex / jnp.sum(ex, axis=0, keepdims=True)
    gate_ref[0] = aff

    bits = pltpu.bitcast(aff, jnp.int32)

    def count(mask):
        return jnp.sum(jnp.where(mask, 1.0, 0.0), axis=1, keepdims=True)

    def thr_step(i, prefix):
        cand = prefix | lax.shift_left(jnp.int32(1), 30 - i)
        return jnp.where(count(bits >= cand) >= cap, cand, prefix)

    thr = lax.fori_loop(0, 31, thr_step, jnp.zeros((ne, 1), jnp.int32))
    gt = bits > thr
    eq = bits == thr
    need = cap - count(gt)
    idx = lax.broadcasted_iota(jnp.int32, (ne, s), 1)

    def cut_step(i, p):
        cand = p + lax.shift_left(jnp.int32(1), (s.bit_length() - 1) - i)
        return jnp.where(count(eq & (idx < cand)) <= need, cand, p)

    cut = lax.fori_loop(0, s.bit_length(), cut_step, jnp.zeros((ne, 1), jnp.int32))
    sel = gt | (eq & (idx < cut))

    blk = 128 if s % 128 == 0 else s
    ri = lax.broadcasted_iota(jnp.int32, (blk, blk), 0)
    ci = lax.broadcasted_iota(jnp.int32, (blk, blk), 1)
    upper = jnp.where(ri <= ci, 1.0, 0.0).astype(BF16)
    off = jnp.zeros((ne, 1), F32)
    for j in range(s // blk):
        sl = slice(j * blk, (j + 1) * blk)
        sel_j = sel[:, sl]
        inc = jnp.dot(jnp.where(sel_j, 1.0, 0.0).astype(BF16), upper, preferred_element_type=F32)
        pos_ref[0, :, sl] = jnp.where(sel_j, (off + inc - 1.0).astype(jnp.int32), -1)
        off = off + inc[:, blk - 1:blk]


def route(hn, w_router, cap):
    bsz, s, d = hn.shape
    ne = w_router.shape[1]
    out = pl.BlockSpec((1, ne, s), lambda b: (b, 0, 0))
    return pl.pallas_call(
        functools.partial(_route_kernel, cap=cap),
        out_shape=(jax.ShapeDtypeStruct((bsz, ne, s), jnp.int32),
                   jax.ShapeDtypeStruct((bsz, ne, s), F32)),
        grid=(bsz,),
        in_specs=[pl.BlockSpec((1, s, d), lambda b: (b, 0, 0)),
                  pl.BlockSpec((ne, d), lambda b: (0, 0))],
        out_specs=(out, out),
        compiler_params=pltpu.CompilerParams(
            dimension_semantics=("parallel",), vmem_limit_bytes=VMEM_LIMIT),
        name="ec_route",
    )(hn, w_router.T)


def _expert_kernel(h_ref, pos_ref, gate_ref, wg_ref, wu_ref, wd_ref, o_ref, *, cap):
    e = pl.program_id(1)

    @pl.when(e == 0)
    def _():
        o_ref[...] = jnp.zeros_like(o_ref)

    pos = pos_ref[0, 0]
    gate = gate_ref[0, 0]
    s = pos.shape[1]
    onehot = pos == lax.broadcasted_iota(jnp.int32, (cap, s), 0)
    sel = jnp.where(onehot, 1.0, 0.0).astype(BF16)
    gcol = jnp.sum(jnp.where(onehot, gate, 0.0), axis=1, keepdims=True)
    xe = jnp.dot(sel, h_ref[0], preferred_element_type=F32).astype(BF16)
    hg = jnp.dot(xe, wg_ref[0], preferred_element_type=F32)
    hu = jnp.dot(xe, wu_ref[0], preferred_element_type=F32)
    hid = (hg * jax.nn.sigmoid(hg) * hu).astype(BF16)
    ye = jnp.dot(hid, wd_ref[0], preferred_element_type=F32) * gcol
    o_ref[0] += lax.dot_general(sel, ye.astype(BF16), TN_DIMS, preferred_element_type=F32)


def experts(hn_bf16, pos, gate, wg, wu, wd, cap):
    bsz, s, d = hn_bf16.shape
    ne, _, ff = wg.shape
    pos4 = pos.reshape(bsz, ne, 1, s)
    gate4 = gate.reshape(bsz, ne, 1, s)
    row = pl.BlockSpec((1, 1, 1, s), lambda b, e: (b, e, 0, 0))
    tok = pl.BlockSpec((1, s, d), lambda b, e: (b, 0, 0))
    return pl.pallas_call(
        functools.partial(_expert_kernel, cap=cap),
        out_shape=jax.ShapeDtypeStruct((bsz, s, d), F32),
        grid=(bsz, ne),
        in_specs=[tok, row, row,
                  pl.BlockSpec((1, d, ff), lambda b, e: (e, 0, 0)),
                  pl.BlockSpec((1, d, ff), lambda b, e: (e, 0, 0)),
                  pl.BlockSpec((1, ff, d), lambda b, e: (e, 0, 0))],
        out_specs=tok,
        compiler_params=pltpu.CompilerParams(
            dimension_semantics=("parallel", "arbitrary"), vmem_limit_bytes=VMEM_LIMIT),
        name="ec_experts",
    )(hn_bf16, pos4, gate4, wg, wu, wd)


def rms_norm(x, g, eps=NORM_EPS):
    y = x * lax.rsqrt(jnp.mean(x * x, -1, keepdims=True) + eps)
    return y * g


def centred_shift(x):
    zero = jnp.zeros_like(x[:, :1])
    prev = jnp.concatenate([zero, x[:, :-1]], axis=1)
    nxt = jnp.concatenate([x[:, 1:], zero], axis=1)
    return 0.5 * (prev + nxt)


def rwkv7_time_mix(h, v_first, mu, w_r, w_k, w_v, w_o, w0, w1, w2, a0, a1, a2,
                   g1, g2, k_k, k_a, r_k, gn_g, gn_b, vres):
    bsz, s, d = h.shape
    m = bsz * s
    nh = d // RWKV_HEAD
    heads = lambda t: t.reshape(bsz, s, nh, RWKV_HEAD)
    xx = centred_shift(h) - h
    xr, xw, xk, xv, xa, xg = [(h + xx * mu[i]).astype(BF16).reshape(m, d) for i in range(6)]
    r = matmul(xr, w_r)
    k = matmul(xk, w_k)
    v = matmul(xv, w_v)
    if vres is None:
        v_first = v
    else:
        v0, v1, v2 = vres
        v = v + (v_first - v) * jax.nn.sigmoid(v0 + matmul(matmul(xv, v1), v2))
    g = matmul(jax.nn.sigmoid(matmul(xg, g1)), g2)
    kk = heads(k * k_k)
    kk = kk / jnp.maximum(jnp.sqrt(jnp.sum(kk * kk, -1, keepdims=True)), 1e-12)
    kk = kk.reshape(m, d)
    to3 = lambda t: t.reshape(bsz, s, d)
    r_k_flat = r_k.reshape(1, d)
    y = jnp.zeros((bsz, s, d), F32)
    bonus = jnp.zeros((bsz, s, nh, RWKV_HEAD), F32)
    for direction in range(2):
        z = w0[direction] + matmul(jnp.tanh(matmul(xw, w1[direction])), w2[direction])
        w_log = -jax.nn.softplus(-z) - 0.5
        lw = -jnp.exp(w_log)
        a = jax.nn.sigmoid(a0[direction] + matmul(matmul(xa, a1[direction]), a2[direction]))
        k_d = k * (1.0 + (a - 1.0) * k_a)
        y = y + wkv7(to3(r), to3(lw), to3(k_d), to3(v), to3(-kk), to3(kk * a),
                     reverse=(direction == 1))
        bonus = bonus + jnp.sum(heads(r * k_d * r_k_flat), -1, keepdims=True) * heads(v)
    yh = heads(y)
    mean = jnp.mean(yh, -1, keepdims=True)
    var = jnp.mean(jnp.square(yh - mean), -1, keepdims=True)
    yn = ((yh - mean) * lax.rsqrt(var + GN_EPS)).reshape(m, d)
    yn = yn * gn_g + gn_b + bonus.reshape(m, d)
    out = matmul(yn * g, w_o)
    return out.reshape(bsz, s, d), v_first


def rope_tables(positions):
    inv_freq = ROPE_THETA ** (-jnp.arange(0, QK_ROPE, 2, dtype=F32) / QK_ROPE)
    ang = positions.astype(F32)[..., None] * inv_freq
    cos, sin = jnp.cos(ang), jnp.sin(ang)
    half = QK_ROPE // 2
    shape = cos.shape[:-1]
    ones = jnp.ones(shape + (QK_NOPE,), F32)
    z = lambda n: jnp.zeros(shape + (n,), F32)
    tail = HEAD_PAD - QK_NOPE - QK_ROPE
    cf = jnp.concatenate([ones, cos, cos, z(tail)], -1)
    sa = jnp.concatenate([z(QK_NOPE), -sin, z(half + tail)], -1)
    sb = jnp.concatenate([z(QK_NOPE + half), sin, z(tail)], -1)
    flat = lambda t: t.reshape(-1, HEAD_PAD)
    return flat(cf), flat(sa), flat(sb)


def _pad_heads(w, nh, width, take):
    k = w.shape[0]
    wh = w.reshape(k, nh, width)[:, :, take]
    wh = jnp.pad(wh, ((0, 0), (0, 0), (0, HEAD_PAD - wh.shape[-1])))
    return wh.reshape(k, nh * HEAD_PAD)


def mla_mix(h, tables, w_in, q_norm_g, kv_norm_g, w_uq, w_ukv, q_head_g, k_head_g, w_o):
    bsz, s, d = h.shape
    m = bsz * s
    qk = QK_NOPE + QK_ROPE
    nh = w_uq.shape[1] // qk
    cf, sa, sb = tables
    n_in = w_in.shape[1]
    n_pad = -n_in % 128
    c = matmul(h.astype(BF16).reshape(m, d), jnp.pad(w_in, ((0, 0), (0, n_pad))))
    c_q = rms_norm(c[:, :Q_LORA], q_norm_g)
    c_kv = rms_norm(c[:, Q_LORA:Q_LORA + KV_LORA], kv_norm_g)
    k_rope = c[:, Q_LORA + KV_LORA:Q_LORA + KV_LORA + QK_ROPE]

    scale = qk ** -0.5
    pad_g = lambda g_: jnp.pad(g_, (0, HEAD_PAD - qk)).reshape(1, HEAD_PAD)
    wq = _pad_heads(w_uq, nh, qk, slice(0, qk))
    q = head_proj(c_q.astype(BF16), wq.astype(BF16), pad_g(q_head_g * scale), cf, sa, sb, nh)

    wk = _pad_heads(w_ukv, nh, QK_NOPE + V_HEAD, slice(0, QK_NOPE))
    place = jnp.zeros((QK_ROPE, nh, HEAD_PAD), F32)
    place = place.at[jnp.arange(QK_ROPE), :, QK_NOPE + jnp.arange(QK_ROPE)].set(1.0)
    wk = jnp.concatenate([wk, place.reshape(QK_ROPE, nh * HEAD_PAD)], 0)
    k_in = jnp.concatenate([c_kv, k_rope], -1)
    kx = head_proj(k_in.astype(BF16), wk.astype(BF16), pad_g(k_head_g), cf, sa, sb, nh)

    wv = w_ukv.reshape(KV_LORA, nh, QK_NOPE + V_HEAD)[:, :, QK_NOPE:].reshape(KV_LORA, nh * V_HEAD)
    v = matmul(c_kv.astype(BF16), wv.astype(BF16), out_dtype=BF16)

    to3 = lambda t: t.reshape(bsz, s, -1)
    o = attention(to3(q), to3(kx), to3(v))
    return matmul(o.reshape(m, nh * V_HEAD), w_o.astype(BF16)).reshape(bsz, s, d)


def expert_choice_ffn(hn, w_router, w_gate, w_up, w_down):
    bsz, s, d = hn.shape
    ne = w_router.shape[1]
    cap = EC_CAPACITY * s // ne
    pos, gate = route(hn, w_router, cap)
    return experts(hn.astype(BF16), pos, gate, w_gate.astype(BF16), w_up.astype(BF16),
                   w_down.astype(BF16), cap)


def kernel(x, positions, norm_mix_g, norm_ffn_g,
           rw_mu, rw_wr, rw_wk, rw_wv, rw_wo, rw_w0, rw_w1, rw_w2,
           rw_a0, rw_a1, rw_a2, rw_g1, rw_g2, rw_kk, rw_ka, rw_rk,
           rw_gn_g, rw_gn_b, rw_v0, rw_v1, rw_v2,
           mla_w_in, mla_q_norm_g, mla_kv_norm_g, mla_w_uq, mla_w_ukv,
           mla_q_head_g, mla_k_head_g, mla_w_o,
           moe_router, moe_w_gate, moe_w_up, moe_w_down):
    depth = norm_mix_g.shape[0]
    n_mixers = 2
    tables = rope_tables(positions)
    v_first = None
    for i in range(depth):
        h = rms_norm(x, norm_mix_g[i])
        j = i // n_mixers
        if i % n_mixers == 0:
            vres = None if j == 0 else (rw_v0[j - 1], rw_v1[j - 1], rw_v2[j - 1])
            mix, v_first = rwkv7_time_mix(
                h, v_first, rw_mu[j], rw_wr[j], rw_wk[j], rw_wv[j], rw_wo[j],
                rw_w0[j], rw_w1[j], rw_w2[j], rw_a0[j], rw_a1[j], rw_a2[j],
                rw_g1[j], rw_g2[j], rw_kk[j], rw_ka[j], rw_rk[j],
                rw_gn_g[j], rw_gn_b[j], vres)
        else:
            mix = mla_mix(h, tables, mla_w_in[j], mla_q_norm_g[j], mla_kv_norm_g[j],
                          mla_w_uq[j], mla_w_ukv[j], mla_q_head_g[j], mla_k_head_g[j],
                          mla_w_o[j])
        x = x + mix
        x = x + expert_choice_ffn(rms_norm(x, norm_ffn_g[i]), moe_router[i],
                                  moe_w_gate[i], moe_w_up[i], moe_w_down[i])
    return x
```

```python
import functools

import jax
import jax.numpy as jnp
from jax import lax
from jax.experimental import pallas as pl
from jax.experimental.pallas import tpu as pltpu

F32 = jnp.float32
BF16 = jnp.bfloat16

NORM_EPS = 1e-6
GN_EPS = 64e-5
RWKV_HEAD = 64
QK_NOPE = 64
QK_ROPE = 32
V_HEAD = 64
Q_LORA = 384
KV_LORA = 256
ROPE_THETA = 10000.0
EC_CAPACITY = 2
HEAD_PAD = 128
WKV_CHUNK = 64
VMEM_LIMIT = 56 * 1024 * 1024

NT_DIMS = (((1,), (1,)), ((), ()))
TN_DIMS = (((0,), (0,)), ((), ()))


def _dot(a, b):
    return jnp.dot(a.astype(BF16), b.astype(BF16), preferred_element_type=F32)


def _dot_nt(a, b):
    return lax.dot_general(a.astype(BF16), b.astype(BF16), NT_DIMS, preferred_element_type=F32)


def _dot_tn(a, b):
    return lax.dot_general(a.astype(BF16), b.astype(BF16), TN_DIMS, preferred_element_type=F32)


def _split_bf16(x):
    hi = x.astype(BF16)
    lo = (x - hi.astype(F32)).astype(BF16)
    return hi, lo


def _mm_kernel(x_ref, w_ref, o_ref):
    o_ref[...] = _dot(x_ref[...], w_ref[...]).astype(o_ref.dtype)


def matmul(x, w, out_dtype=F32, bm=512):
    m, k = x.shape
    n = w.shape[1]
    bm = min(bm, m)
    bn = n if n <= 1024 else 1024
    assert m % bm == 0 and n % bn == 0
    return pl.pallas_call(
        _mm_kernel,
        out_shape=jax.ShapeDtypeStruct((m, n), out_dtype),
        grid=(n // bn, m // bm),
        in_specs=[pl.BlockSpec((bm, k), lambda j, i: (i, 0)),
                  pl.BlockSpec((k, bn), lambda j, i: (0, j))],
        out_specs=pl.BlockSpec((bm, bn), lambda j, i: (i, j)),
        compiler_params=pltpu.CompilerParams(
            dimension_semantics=("parallel", "parallel"), vmem_limit_bytes=VMEM_LIMIT),
        name="matmul",
    )(x, w)


def _wkv_kernel(r_ref, lw_ref, k_ref, v_ref, a_ref, b_ref, y_ref, h_scr, *, reverse, nh, hd):
    c = pl.program_id(1)

    @pl.when(c == 0)
    def _():
        h_scr[...] = jnp.zeros_like(h_scr)

    lw = lw_ref[0]
    L = lw.shape[0]
    row = lax.broadcasted_iota(jnp.int32, (L, L), 0)
    col = lax.broadcasted_iota(jnp.int32, (L, L), 1)
    incl = (col >= row) if reverse else (col <= row)
    tri = jnp.where(incl, 1.0, 0.0).astype(BF16)
    eye = jnp.where(row == col, 1.0, 0.0).astype(F32)

    lw_hi, lw_lo = _split_bf16(lw)
    cs = (jnp.dot(tri, lw_hi, preferred_element_type=F32)
          + jnp.dot(tri, lw_lo, preferred_element_type=F32))
    last = 0 if reverse else L - 1
    ctot = cs[last:last + 1, :]

    r, k, v, a, b = r_ref[0], k_ref[0], v_ref[0], a_ref[0], b_ref[0]
    rt = r * jnp.exp(cs)
    at = a * jnp.exp(cs - lw)
    einv = jnp.exp(-cs)
    bt = b * einv
    kt = k * einv
    etot = jnp.exp(ctot - cs)
    bb = b * etot
    kb = k * etot
    wtot = jnp.exp(ctot)

    row2 = lax.broadcasted_iota(jnp.int32, (2 * L, 2 * L), 0)
    col2 = lax.broadcasted_iota(jnp.int32, (2 * L, 2 * L), 1)
    rt2, cs2 = row2 & (L - 1), col2 & (L - 1)
    before = (cs2 > rt2) if reverse else (cs2 < rt2)
    score_mask = before | ((row2 >= L) & (cs2 == rt2))
    right_half = lax.broadcasted_iota(jnp.int32, (L, 2 * L), 1) >= L
    diag_blk = (row ^ col) < 2
    level_masks = []
    size = 2
    while size < L:
        x = row ^ col
        level_masks.append((x >= size) & (x < 2 * size))
        size *= 2

    heads = range(nh)
    sls = [slice(h * hd, (h + 1) * hd) for h in heads]
    v_hs = [v[:, sl] for sl in sls]
    at_hs = [at[:, sl] for sl in sls]
    rt_hs = [rt[:, sl] for sl in sls]
    sc = [jnp.where(score_mask,
                    _dot_nt(jnp.concatenate([at_hs[h], rt_hs[h]], 0),
                            jnp.concatenate([bt[:, sls[h]], kt[:, sls[h]]], 0)), 0.0)
          for h in heads]
    top = [s_[:L] for s_ in sc]
    bot = [s_[L:] for s_ in sc]
    n_ab = [t_[:, :L] for t_ in top]
    t = [eye + jnp.where(diag_blk, n_, 0.0) for n_ in n_ab]
    for lm in level_masks:
        u = [_dot(jnp.where(lm, n_ab[h], 0.0), t[h]) for h in heads]
        t = [t[h] + _dot(t[h], u[h]) for h in heads]
    x = [_dot(jnp.where(right_half, top[h], 0.0), jnp.concatenate([v_hs[h], v_hs[h]], 0)) for h in heads]
    ah = [_dot(t[h], at_hs[h]) for h in heads]
    vh = [_dot(t[h], x[h]) for h in heads]
    rh = [rt_hs[h] + _dot(jnp.where(right_half, 0.0, bot[h]), jnp.concatenate([ah[h], ah[h]], 0))
          for h in heads]
    vv = [jnp.concatenate([vh[h], v_hs[h]], 0) for h in heads]
    yh = [_dot(bot[h], vv[h]) for h in heads]
    m = [_dot_tn(bb[:, sls[h]], ah[h]) + eye[:hd, :hd] * wtot[:, sls[h]] for h in heads]
    g = [_dot_tn(jnp.concatenate([bb[:, sls[h]], kb[:, sls[h]]], 0), vv[h]) for h in heads]
    hs = [h_scr[h] for h in heads]
    fin = [_dot(jnp.concatenate([rh[h], m[h]], 0), hs[h]) for h in heads]
    for h in heads:
        y_ref[0, :, sls[h]] = fin[h][:L] + yh[h]
        h_scr[h] = fin[h][L:] + g[h]


def wkv7(r, lw, k, v, a, b, reverse):
    bsz, s, d = r.shape
    hd = RWKV_HEAD
    nh = d // hd
    L = min(WKV_CHUNK, s)
    assert L >= hd and s % L == 0
    nc = s // L
    if reverse:
        idx = lambda bi, ci: (bi, nc - 1 - ci, 0)
    else:
        idx = lambda bi, ci: (bi, ci, 0)
    spec = pl.BlockSpec((1, L, d), idx)
    return pl.pallas_call(
        functools.partial(_wkv_kernel, reverse=reverse, nh=nh, hd=hd),
        out_shape=jax.ShapeDtypeStruct((bsz, s, d), F32),
        grid=(bsz, nc),
        in_specs=[spec] * 6,
        out_specs=spec,
        scratch_shapes=[pltpu.VMEM((nh, hd, hd), F32)],
        compiler_params=pltpu.CompilerParams(
            dimension_semantics=("parallel", "arbitrary"), vmem_limit_bytes=VMEM_LIMIT),
        name="wkv7_rev" if reverse else "wkv7_fwd",
    )(r, lw, k, v, a, b)


def _headproj_kernel(x_ref, w_ref, g_ref, c_ref, sa_ref, sb_ref, o_ref, *, nh, real_dim):
    y = _dot(x_ref[...], w_ref[...])
    cf, sa, sb = c_ref[...], sa_ref[...], sb_ref[...]
    g = g_ref[...]
    half = QK_ROPE // 2
    for h in range(nh):
        sl = slice(h * HEAD_PAD, (h + 1) * HEAD_PAD)
        yh = y[:, sl]
        ms = jnp.sum(yh * yh, axis=-1, keepdims=True) * (1.0 / real_dim)
        yn = yh * lax.rsqrt(ms + NORM_EPS) * g
        rot = (yn * cf + pltpu.roll(yn, HEAD_PAD - half, 1) * sa + pltpu.roll(yn, half, 1) * sb)
        o_ref[:, sl] = rot.astype(o_ref.dtype)


def head_proj(x, w_pad, gain_pad, cf, sa, sb, nh, bm=256):
    m, k = x.shape
    n = nh * HEAD_PAD
    bm = min(bm, m)
    tab = pl.BlockSpec((bm, HEAD_PAD), lambda i: (i, 0))
    return pl.pallas_call(
        functools.partial(_headproj_kernel, nh=nh, real_dim=QK_NOPE + QK_ROPE),
        out_shape=jax.ShapeDtypeStruct((m, n), BF16),
        grid=(m // bm,),
        in_specs=[pl.BlockSpec((bm, k), lambda i: (i, 0)),
                  pl.BlockSpec((k, n), lambda i: (0, 0)),
                  pl.BlockSpec((1, HEAD_PAD), lambda i: (0, 0)),
                  tab, tab, tab],
        out_specs=pl.BlockSpec((bm, n), lambda i: (i, 0)),
        compiler_params=pltpu.CompilerParams(
            dimension_semantics=("parallel",), vmem_limit_bytes=VMEM_LIMIT),
        name="head_proj",
    )(x, w_pad, gain_pad, cf, sa, sb)


def _attn_kernel(q_ref, k_ref, v_ref, o_ref):
    q, k, v = q_ref[0], k_ref[0], v_ref[0]
    lane = lax.broadcasted_iota(jnp.int32, (1, 2 * V_HEAD), 1)
    acc = jnp.zeros((q.shape[0], 2 * V_HEAD), F32)
    for j in range(2):
        sl = slice(j * HEAD_PAD, (j + 1) * HEAD_PAD)
        s = _dot_nt(q[:, sl], k[:, sl])
        p = jnp.exp(s - jnp.max(s, axis=-1, keepdims=True))
        den = jnp.sum(p, axis=-1, keepdims=True)
        vj = jnp.where((lane >= j * V_HEAD) & (lane < (j + 1) * V_HEAD), v, jnp.zeros_like(v))
        acc = acc + _dot(p, vj) / den
    o_ref[0] = acc.astype(o_ref.dtype)


def attention(q, k, v, tq=256):
    bsz, s, _ = q.shape
    nh = v.shape[-1] // V_HEAD
    tq = min(tq, s)
    return pl.pallas_call(
        _attn_kernel,
        out_shape=jax.ShapeDtypeStruct((bsz, s, nh * V_HEAD), BF16),
        grid=(bsz, nh // 2, s // tq),
        in_specs=[pl.BlockSpec((1, tq, 2 * HEAD_PAD), lambda b, h, i: (b, i, h)),
                  pl.BlockSpec((1, s, 2 * HEAD_PAD), lambda b, h, i: (b, 0, h)),
                  pl.BlockSpec((1, s, 2 * V_HEAD), lambda b, h, i: (b, 0, h))],
        out_specs=pl.BlockSpec((1, tq, 2 * V_HEAD), lambda b, h, i: (b, i, h)),
        compiler_params=pltpu.CompilerParams(
            dimension_semantics=("parallel", "parallel", "parallel"), vmem_limit_bytes=VMEM_LIMIT),
        name="mla_attention",
    )(q, k, v)


def _route_kernel(h_ref, wr_ref, pos_ref, gate_ref, *, cap):
    h = h_ref[0]
    wr = wr_ref[...]
    s = h.shape[0]
    ne = wr.shape[0]
    h_hi, h_lo = _split_bf16(h)
    w_hi, w_lo = _split_bf16(wr)
    logits = (lax.dot_general(w_hi, h_hi, NT_DIMS, preferred_element_type=F32)
              + lax.dot_general(w_hi, h_lo, NT_DIMS, preferred_element_type=F32)
              + lax.dot_general(w_lo, h_hi, NT_DIMS, preferred_element_type=F32))
    ex = jnp.exp(logits - jnp.max(logits, axis=0, keepdims=True))
    aff = ex / jnp.sum(ex, axis=0, keepdims=True)
    gate_ref[0] = aff

    bits = pltpu.bitcast(aff, jnp.int32)

    def count(mask):
        return jnp.sum(jnp.where(mask, 1.0, 0.0), axis=1, keepdims=True)

    def thr_step(i, prefix):
        cand = prefix | lax.shift_left(jnp.int32(1), 30 - i)
        return jnp.where(count(bits >= cand) >= cap, cand, prefix)

    thr = lax.fori_loop(0, 31, thr_step, jnp.zeros((ne, 1), jnp.int32))
    gt = bits > thr
    eq = bits == thr
    need = cap - count(gt)
    idx = lax.broadcasted_iota(jnp.int32, (ne, s), 1)

    def cut_step(i, p):
        cand = p + lax.shift_left(jnp.int32(1), (s.bit_length() - 1) - i)
        return jnp.where(count(eq & (idx < cand)) <= need, cand, p)

    cut = lax.fori_loop(0, s.bit_length(), cut_step, jnp.zeros((ne, 1), jnp.int32))
    sel = gt | (eq & (idx < cut))

    blk = 128 if s % 128 == 0 else s
    ri = lax.broadcasted_iota(jnp.int32, (blk, blk), 0)
    ci = lax.broadcasted_iota(jnp.int32, (blk, blk), 1)
    upper = jnp.where(ri <= ci, 1.0, 0.0).astype(BF16)
    off = jnp.zeros((ne, 1), F32)
    for j in range(s // blk):
        sl = slice(j * blk, (j + 1) * blk)
        sel_j = sel[:, sl]
        inc = jnp.dot(jnp.where(sel_j, 1.0, 0.0).astype(BF16), upper, preferred_element_type=F32)
        pos_ref[0, :, sl] = jnp.where(sel_j, (off + inc - 1.0).astype(jnp.int32), -1)
        off = off + inc[:, blk - 1:blk]


def route(hn, w_router, cap):
    bsz, s, d = hn.shape
    ne = w_router.shape[1]
    out = pl.BlockSpec((1, ne, s), lambda b: (b, 0, 0))
    return pl.pallas_call(
        functools.partial(_route_kernel, cap=cap),
        out_shape=(jax.ShapeDtypeStruct((bsz, ne, s), jnp.int32),
                   jax.ShapeDtypeStruct((bsz, ne, s), F32)),
        grid=(bsz,),
        in_specs=[pl.BlockSpec((1, s, d), lambda b: (b, 0, 0)),
                  pl.BlockSpec((ne, d), lambda b: (0, 0))],
        out_specs=(out, out),
        compiler_params=pltpu.CompilerParams(
            dimension_semantics=("parallel",), vmem_limit_bytes=VMEM_LIMIT),
        name="ec_route",
    )(hn, w_router.T)


def _expert_kernel(h_ref, pos_ref, gate_ref, wg_ref, wu_ref, wd_ref, o_ref, *, cap):
    e = pl.program_id(1)

    @pl.when(e == 0)
    def _():
        o_ref[...] = jnp.zeros_like(o_ref)

    pos = pos_ref[0, 0]
    gate = gate_ref[0, 0]
    s = pos.shape[1]
    onehot = pos == lax.broadcasted_iota(jnp.int32, (cap, s), 0)
    sel = jnp.where(onehot, 1.0, 0.0).astype(BF16)
    gcol = jnp.sum(jnp.where(onehot, gate, 0.0), axis=1, keepdims=True)
    xe = jnp.dot(sel, h_ref[0], preferred_element_type=F32).astype(BF16)
    hg = jnp.dot(xe, wg_ref[0], preferred_element_type=F32)
    hu = jnp.dot(xe, wu_ref[0], preferred_element_type=F32)
    hid = (hg * jax.nn.sigmoid(hg) * hu).astype(BF16)
    ye = jnp.dot(hid, wd_ref[0], preferred_element_type=F32) * gcol
    o_ref[0] += lax.dot_general(sel, ye.astype(BF16), TN_DIMS, preferred_element_type=F32)


def experts(hn_bf16, pos, gate, wg, wu, wd, cap):
    bsz, s, d = hn_bf16.shape
    ne, _, ff = wg.shape
    pos4 = pos.reshape(bsz, ne, 1, s)
    gate4 = gate.reshape(bsz, ne, 1, s)
    row = pl.BlockSpec((1, 1, 1, s), lambda b, e: (b, e, 0, 0))
    tok = pl.BlockSpec((1, s, d), lambda b, e: (b, 0, 0))
    return pl.pallas_call(
        functools.partial(_expert_kernel, cap=cap),
        out_shape=jax.ShapeDtypeStruct((bsz, s, d), F32),
        grid=(bsz, ne),
        in_specs=[tok, row, row,
                  pl.BlockSpec((1, d, ff), lambda b, e: (e, 0, 0)),
                  pl.BlockSpec((1, d, ff), lambda b, e: (e, 0, 0)),
                  pl.BlockSpec((1, ff, d), lambda b, e: (e, 0, 0))],
        out_specs=tok,
        compiler_params=pltpu.CompilerParams(
            dimension_semantics=("parallel", "arbitrary"), vmem_limit_bytes=VMEM_LIMIT),
        name="ec_experts",
    )(hn_bf16, pos4, gate4, wg, wu, wd)


def rms_norm(x, g, eps=NORM_EPS):
    y = x * lax.rsqrt(jnp.mean(x * x, -1, keepdims=True) + eps)
    return y * g


def centred_shift(x):
    zero = jnp.zeros_like(x[:, :1])
    prev = jnp.concatenate([zero, x[:, :-1]], axis=1)
    nxt = jnp.concatenate([x[:, 1:], zero], axis=1)
    return 0.5 * (prev + nxt)


def rwkv7_time_mix(h, v_first, mu, w_r, w_k, w_v, w_o, w0, w1, w2, a0, a1, a2,
                   g1, g2, k_k, k_a, r_k, gn_g, gn_b, vres):
    bsz, s, d = h.shape
    m = bsz * s
    nh = d // RWKV_HEAD
    heads = lambda t: t.reshape(bsz, s, nh, RWKV_HEAD)
    xx = centred_shift(h) - h
    xr, xw, xk, xv, xa, xg = [(h + xx * mu[i]).astype(BF16).reshape(m, d) for i in range(6)]
    r = matmul(xr, w_r)
    k = matmul(xk, w_k)
    v = matmul(xv, w_v)
    if vres is None:
        v_first = v
    else:
        v0, v1, v2 = vres
        v = v + (v_first - v) * jax.nn.sigmoid(v0 + matmul(matmul(xv, v1), v2))
    g = matmul(jax.nn.sigmoid(matmul(xg, g1)), g2)
    kk = heads(k * k_k)
    kk = kk / jnp.maximum(jnp.sqrt(jnp.sum(kk * kk, -1, keepdims=True)), 1e-12)
    kk = kk.reshape(m, d)
    to3 = lambda t: t.reshape(bsz, s, d)
    r_k_flat = r_k.reshape(1, d)
    y = jnp.zeros((bsz, s, d), F32)
    bonus = jnp.zeros((bsz, s, nh, RWKV_HEAD), F32)
    for direction in range(2):
        z = w0[direction] + matmul(jnp.tanh(matmul(xw, w1[direction])), w2[direction])
        w_log = -jax.nn.softplus(-z) - 0.5
        lw = -jnp.exp(w_log)
        a = jax.nn.sigmoid(a0[direction] + matmul(matmul(xa, a1[direction]), a2[direction]))
        k_d = k * (1.0 + (a - 1.0) * k_a)
        y = y + wkv7(to3(r), to3(lw), to3(k_d), to3(v), to3(-kk), to3(kk * a),
                     reverse=(direction == 1))
        bonus = bonus + jnp.sum(heads(r * k_d * r_k_flat), -1, keepdims=True) * heads(v)
    yh = heads(y)
    mean = jnp.mean(yh, -1, keepdims=True)
    var = jnp.mean(jnp.square(yh - mean), -1, keepdims=True)
    yn = ((yh - mean) * lax.rsqrt(var + GN_EPS)).reshape(m, d)
    yn = yn * gn_g + gn_b + bonus.reshape(m, d)
    out = matmul(yn * g, w_o)
    return out.reshape(bsz, s, d), v_first


def rope_tables(positions):
    inv_freq = ROPE_THETA ** (-jnp.arange(0, QK_ROPE, 2, dtype=F32) / QK_ROPE)
    ang = positions.astype(F32)[..., None] * inv_freq
    cos, sin = jnp.cos(ang), jnp.sin(ang)
    half = QK_ROPE // 2
    shape = cos.shape[:-1]
    ones = jnp.ones(shape + (QK_NOPE,), F32)
    z = lambda n: jnp.zeros(shape + (n,), F32)
    tail = HEAD_PAD - QK_NOPE - QK_ROPE
    cf = jnp.concatenate([ones, cos, cos, z(tail)], -1)
    sa = jnp.concatenate([z(QK_NOPE), -sin, z(half + tail)], -1)
    sb = jnp.concatenate([z(QK_NOPE + half), sin, z(tail)], -1)
    flat = lambda t: t.reshape(-1, HEAD_PAD)
    return flat(cf), flat(sa), flat(sb)


def _pad_heads(w, nh, width, take):
    k = w.shape[0]
    wh = w.reshape(k, nh, width)[:, :, take]
    wh = jnp.pad(wh, ((0, 0), (0, 0), (0, HEAD_PAD - wh.shape[-1])))
    return wh.reshape(k, nh * HEAD_PAD)


def mla_mix(h, tables, w_in, q_norm_g, kv_norm_g, w_uq, w_ukv, q_head_g, k_head_g, w_o):
    bsz, s, d = h.shape
    m = bsz * s
    qk = QK_NOPE + QK_ROPE
    nh = w_uq.shape[1] // qk
    cf, sa, sb = tables
    n_in = w_in.shape[1]
    n_pad = -n_in % 128
    c = matmul(h.astype(BF16).reshape(m, d), jnp.pad(w_in, ((0, 0), (0, n_pad))))
    c_q = rms_norm(c[:, :Q_LORA], q_norm_g)
    c_kv = rms_norm(c[:, Q_LORA:Q_LORA + KV_LORA], kv_norm_g)
    k_rope = c[:, Q_LORA + KV_LORA:Q_LORA + KV_LORA + QK_ROPE]

    scale = qk ** -0.5
    pad_g = lambda g_: jnp.pad(g_, (0, HEAD_PAD - qk)).reshape(1, HEAD_PAD)
    wq = _pad_heads(w_uq, nh, qk, slice(0, qk))
    q = head_proj(c_q.astype(BF16), wq.astype(BF16), pad_g(q_head_g * scale), cf, sa, sb, nh)

    wk = _pad_heads(w_ukv, nh, QK_NOPE + V_HEAD, slice(0, QK_NOPE))
    place = jnp.zeros((QK_ROPE, nh, HEAD_PAD), F32)
    place = place.at[jnp.arange(QK_ROPE), :, QK_NOPE + jnp.arange(QK_ROPE)].set(1.0)
    wk = jnp.concatenate([wk, place.reshape(QK_ROPE, nh * HEAD_PAD)], 0)
    k_in = jnp.concatenate([c_kv, k_rope], -1)
    kx = head_proj(k_in.astype(BF16), wk.astype(BF16), pad_g(k_head_g), cf, sa, sb, nh)

    wv = w_ukv.reshape(KV_LORA, nh, QK_NOPE + V_HEAD)[:, :, QK_NOPE:].reshape(KV_LORA, nh * V_HEAD)
    v = matmul(c_kv.astype(BF16), wv.astype(BF16), out_dtype=BF16)

    to3 = lambda t: t.reshape(bsz, s, -1)
    o = attention(to3(q), to3(kx), to3(v))
    return matmul(o.reshape(m, nh * V_HEAD), w_o.astype(BF16)).reshape(bsz, s, d)


def expert_choice_ffn(hn, w_router, w_gate, w_up, w_down):
    bsz, s, d = hn.shape
    ne = w_router.shape[1]
    cap = EC_CAPACITY * s // ne
    pos, gate = route(hn, w_router, cap)
    return experts(hn.astype(BF16), pos, gate, w_gate.astype(BF16), w_up.astype(BF16),
                   w_down.astype(BF16), cap)


def kernel(x, positions, norm_mix_g, norm_ffn_g,
           rw_mu, rw_wr, rw_wk, rw_wv, rw_wo, rw_w0, rw_w1, rw_w2,
           rw_a0, rw_a1, rw_a2, rw_g1, rw_g2, rw_kk, rw_ka, rw_rk,
           rw_gn_g, rw_gn_b, rw_v0, rw_v1, rw_v2,
           mla_w_in, mla_q_norm_g, mla_kv_norm_g, mla_w_uq, mla_w_ukv,
           mla_q_head_g, mla_k_head_g, mla_w_o,
           moe_router, moe_w_gate, moe_w_up, moe_w_down):
    depth = norm_mix_g.shape[0]
    n_mixers = 2
    tables = rope_tables(positions)
    v_first = None
    for i in range(depth):
        h = rms_norm(x, norm_mix_g[i])
        j = i // n_mixers
        if i % n_mixers == 0:
            vres = None if j == 0 else (rw_v0[j - 1], rw_v1[j - 1], rw_v2[j - 1])
            mix, v_first = rwkv7_time_mix(
                h, v_first, rw_mu[j], rw_wr[j], rw_wk[j], rw_wv[j], rw_wo[j],
                rw_w0[j], rw_w1[j], rw_w2[j], rw_a0[j], rw_a1[j], rw_a2[j],
                rw_g1[j], rw_g2[j], rw_kk[j], rw_ka[j], rw_rk[j],
                rw_gn_g[j], rw_gn_b[j], vres)
        else:
            mix = mla_mix(h, tables, mla_w_in[j], mla_q_norm_g[j], mla_kv_norm_g[j],
                          mla_w_uq[j], mla_w_ukv[j], mla_q_head_g[j], mla_k_head_g[j],
                          mla_w_o[j])
        x = x + mix
        x = x + expert_choice_ffn(rms_norm(x, norm_ffn_g[i]), moe_router[i],
                                  moe_w_gate[i], moe_w_up[i], moe_w_down[i])
    return x
```

```python
import functools

import jax
import jax.numpy as jnp
from jax import lax
from jax.experimental import pallas as pl
from jax.experimental.pallas import tpu as pltpu

F32 = jnp.float32
BF16 = jnp.bfloat16

NORM_EPS = 1e-6
GN_EPS = 64e-5
RWKV_HEAD = 64
QK_NOPE = 64
QK_ROPE = 32
V_HEAD = 64
Q_LORA = 384
KV_LORA = 256
ROPE_THETA = 10000.0
EC_CAPACITY = 2
LANES = 128
SUBLANES = 8
HEAD_PAD = LANES
WKV_CHUNK = 64
VMEM_LIMIT = 56 * 1024 * 1024

NT_DIMS = (((1,), (1,)), ((), ()))
TN_DIMS = (((0,), (0,)), ((), ()))


def _dot(a, b):
    return jnp.dot(a.astype(BF16), b.astype(BF16), preferred_element_type=F32)


def _dot_nt(a, b):
    return lax.dot_general(a.astype(BF16), b.astype(BF16), NT_DIMS, preferred_element_type=F32)


def _dot_tn(a, b):
    return lax.dot_general(a.astype(BF16), b.astype(BF16), TN_DIMS, preferred_element_type=F32)


def _split_bf16(x):
    hi = x.astype(BF16)
    lo = (x - hi.astype(F32)).astype(BF16)
    return hi, lo


def _rms(x, gain):
    return x * lax.rsqrt(jnp.mean(x * x, axis=-1, keepdims=True) + NORM_EPS) * gain


def _seg_sum(t, seg):
    width = min(LANES, t.shape[1])
    ri = lax.broadcasted_iota(jnp.int32, (width, width), 0)
    ci = lax.broadcasted_iota(jnp.int32, (width, width), 1)
    ones = jnp.where((ri // seg) == (ci // seg), 1.0, 0.0).astype(BF16)
    parts = [_dot(t[:, j:j + width], ones) for j in range(0, t.shape[1], width)]
    return parts[0] if len(parts) == 1 else jnp.concatenate(parts, axis=1)


def _softplus(u):
    return jnp.maximum(u, 0.0) + jnp.log(1.0 + jnp.exp(-jnp.abs(u)))


def _pad_to(w, axis, mult):
    pad = -w.shape[axis] % mult
    if pad == 0:
        return w
    widths = [(0, 0)] * w.ndim
    widths[axis] = (0, pad)
    return jnp.pad(w, widths)


def _const_spec(shape):
    return pl.BlockSpec(shape, lambda *_: (0,) * len(shape))


def _params(*sem):
    return pltpu.CompilerParams(dimension_semantics=sem, vmem_limit_bytes=VMEM_LIMIT)


def _rwkv_pre_kernel(*refs, has_vres, seg):
    it = iter(refs)
    (x_ref, xp_ref, xn_ref, vec_ref, wr_ref, wk_ref, wv_ref,
     w1_ref, a1_ref, g1_ref, w2_ref, a2_ref, g2_ref) = [next(it) for _ in range(13)]
    if has_vres:
        v1_ref, v2_ref, vf_ref = next(it), next(it), next(it)
    r_o, na_o, v_o, g_o, bon_o = [next(it) for _ in range(5)]
    lw_o = [next(it), next(it)]
    k_o = [next(it), next(it)]
    b_o = [next(it), next(it)]
    if not has_vres:
        vf_o = next(it)

    i = pl.program_id(1)
    vec = vec_ref[...]
    row_of = lambda j: vec[j:j + 1]
    gain = row_of(0)
    x = x_ref[0]
    bm = x.shape[0]
    h = _rms(x, gain)
    hp = jnp.where(i == 0, 0.0, _rms(xp_ref[0][SUBLANES - 1:SUBLANES], gain))
    hn = jnp.where(i == pl.num_programs(1) - 1, 0.0, _rms(xn_ref[0][0:1], gain))
    row = lax.broadcasted_iota(jnp.int32, (bm, 1), 0)
    h_prev = jnp.where(row == 0, hp, pltpu.roll(h, 1, 0))
    h_next = jnp.where(row == bm - 1, hn, pltpu.roll(h, bm - 1, 0))
    xx = 0.5 * (h_prev + h_next) - h
    xr, xw, xk, xv, xa, xg = [(h + xx * row_of(1 + j)).astype(BF16) for j in range(6)]

    r = _dot(xr, wr_ref[...])
    k = _dot(xk, wk_ref[...])
    v = _dot(xv, wv_ref[...])
    if has_vres:
        vz = row_of(14) + _dot(_dot(xv, v1_ref[...]), v2_ref[...])
        v = v + (vf_ref[0] - v) * jax.nn.sigmoid(vz)
    else:
        vf_o[0] = v
    g = _dot(jax.nn.sigmoid(_dot(xg, g1_ref[...])), g2_ref[...])

    kkr = k * row_of(11)
    kk = kkr / jnp.maximum(jnp.sqrt(_seg_sum(kkr * kkr, seg)), 1e-12)
    tw = jnp.tanh(_dot(xw, w1_ref[...])).astype(BF16)
    al = _dot(xa, a1_ref[...]).astype(BF16)
    k_sum = jnp.zeros_like(k)
    for d in range(2):
        z = row_of(7 + d) + _dot(tw, w2_ref[d])
        lw_o[d][0] = -jnp.exp(-_softplus(-z) - 0.5)
        a = jax.nn.sigmoid(row_of(9 + d) + _dot(al, a2_ref[d]))
        k_d = k * (1.0 + (a - 1.0) * row_of(12))
        k_o[d][0] = k_d.astype(BF16)
        b_o[d][0] = (kk * a).astype(BF16)
        k_sum = k_sum + k_d
    bon_o[0] = (_seg_sum(r * k_sum * row_of(13), seg) * v).astype(BF16)
    r_o[0] = r.astype(BF16)
    na_o[0] = (-kk).astype(BF16)
    v_o[0] = v.astype(BF16)
    g_o[0] = g.astype(BF16)


def rwkv_pre(x, v_first, norm_g, mu, w_r, w_k, w_v, w0, w1, w2, a0, a1, a2, g1, g2, k_k, k_a, r_k,
             vres, bm=256):
    bsz, s, d = x.shape
    bm = min(bm, s)
    nt = s // bm
    has_vres = vres is not None
    zero = jnp.zeros((d,), F32)
    v0 = vres[0] if has_vres else zero
    vec = jnp.stack([norm_g, *mu, w0[0], w0[1], a0[0], a0[1], k_k, k_a, r_k.reshape(d), v0, zero])
    bf = lambda t: t.astype(BF16)
    cat2 = lambda t: jnp.concatenate([t[0], t[1]], axis=1)
    lo = w1.shape[2]
    second = lambda t: jnp.stack([jnp.pad(t[0], ((0, lo), (0, 0))), jnp.pad(t[1], ((lo, 0), (0, 0)))])
    ins = [x, x, x, vec, bf(w_r), bf(w_k), bf(w_v), bf(cat2(w1)), bf(cat2(a1)), bf(_pad_to(g1, 1, LANES)),
           bf(second(w2)), bf(second(a2)), bf(_pad_to(g2, 0, LANES))]
    tile = pl.BlockSpec((1, bm, d), lambda b, i: (b, i, 0))
    hb = bm // SUBLANES
    in_specs = [tile,
                pl.BlockSpec((1, SUBLANES, d), lambda b, i: (b, jnp.maximum(i * hb - 1, 0), 0)),
                pl.BlockSpec((1, SUBLANES, d), lambda b, i: (b, jnp.minimum((i + 1) * hb, s // SUBLANES - 1), 0))]
    in_specs += [_const_spec(t.shape) for t in ins[3:]]
    if has_vres:
        extra = [bf(_pad_to(vres[1], 1, LANES)), bf(_pad_to(vres[2], 0, LANES))]
        ins += extra + [v_first]
        in_specs += [_const_spec(t.shape) for t in extra] + [tile]
    n_bf, n_f32 = 5, 2
    out_shape = ([jax.ShapeDtypeStruct((bsz, s, d), BF16)] * n_bf + [jax.ShapeDtypeStruct((bsz, s, d), F32)] * n_f32
                 + [jax.ShapeDtypeStruct((bsz, s, d), BF16)] * 4)
    if not has_vres:
        out_shape.append(jax.ShapeDtypeStruct((bsz, s, d), F32))
    outs = pl.pallas_call(
        functools.partial(_rwkv_pre_kernel, has_vres=has_vres, seg=RWKV_HEAD),
        out_shape=tuple(out_shape),
        grid=(bsz, nt),
        in_specs=in_specs,
        out_specs=tuple([tile] * len(out_shape)),
        compiler_params=_params("parallel", "parallel"),
        name="rwkv_pre",
    )(*ins)
    r, na, v, g, bonus, lw0, lw1, k0, k1, b0, b1 = outs[:11]
    v_first = v_first if has_vres else outs[11]
    return r, na, v, g, bonus, (lw0, lw1), (k0, k1), (b0, b1), v_first


def _wkv_kernel(r_ref, lw_ref, k_ref, v_ref, a_ref, b_ref, y_ref, h_scr, *, reverse, nh, hd):
    c = pl.program_id(1)

    @pl.when(c == 0)
    def _():
        h_scr[...] = jnp.zeros_like(h_scr)

    lw = lw_ref[0]
    L = lw.shape[0]
    row = lax.broadcasted_iota(jnp.int32, (L, L), 0)
    col = lax.broadcasted_iota(jnp.int32, (L, L), 1)
    incl = (col >= row) if reverse else (col <= row)
    tri = jnp.where(incl, 1.0, 0.0).astype(BF16)
    eye = jnp.where(row == col, 1.0, 0.0).astype(F32)

    lw_hi, lw_lo = _split_bf16(lw)
    cs = (jnp.dot(tri, lw_hi, preferred_element_type=F32)
          + jnp.dot(tri, lw_lo, preferred_element_type=F32))
    last = 0 if reverse else L - 1
    ctot = cs[last:last + 1, :]

    r, k, v, a, b = [t[0].astype(F32) for t in (r_ref, k_ref, v_ref, a_ref, b_ref)]
    rt = r * jnp.exp(cs)
    at = a * jnp.exp(cs - lw)
    einv = jnp.exp(-cs)
    bt = b * einv
    kt = k * einv
    etot = jnp.exp(ctot - cs)
    bb = b * etot
    kb = k * etot
    wtot = jnp.exp(ctot)

    row2 = lax.broadcasted_iota(jnp.int32, (2 * L, 2 * L), 0)
    col2 = lax.broadcasted_iota(jnp.int32, (2 * L, 2 * L), 1)
    rt2, cs2 = row2 & (L - 1), col2 & (L - 1)
    before = (cs2 > rt2) if reverse else (cs2 < rt2)
    score_mask = before | ((row2 >= L) & (cs2 == rt2))
    right_half = lax.broadcasted_iota(jnp.int32, (L, 2 * L), 1) >= L
    diag_blk = (row ^ col) < 2
    level_masks = []
    size = 2
    while size < L:
        x = row ^ col
        level_masks.append((x >= size) & (x < 2 * size))
        size *= 2

    heads = range(nh)
    sls = [slice(h * hd, (h + 1) * hd) for h in heads]
    v_hs = [v[:, sl] for sl in sls]
    at_hs = [at[:, sl] for sl in sls]
    rt_hs = [rt[:, sl] for sl in sls]
    sc = [jnp.where(score_mask,
                    _dot_nt(jnp.concatenate([at_hs[h], rt_hs[h]], 0),
                            jnp.concatenate([bt[:, sls[h]], kt[:, sls[h]]], 0)), 0.0)
          for h in heads]
    top = [s_[:L] for s_ in sc]
    bot = [s_[L:] for s_ in sc]
    n_ab = [t_[:, :L] for t_ in top]
    t = [eye + jnp.where(diag_blk, n_, 0.0) for n_ in n_ab]
    for lm in level_masks:
        u = [_dot(jnp.where(lm, n_ab[h], 0.0), t[h]) for h in heads]
        t = [t[h] + _dot(t[h], u[h]) for h in heads]
    x = [_dot(jnp.where(right_half, top[h], 0.0), jnp.concatenate([v_hs[h], v_hs[h]], 0)) for h in heads]
    ah = [_dot(t[h], at_hs[h]) for h in heads]
    vh = [_dot(t[h], x[h]) for h in heads]
    rh = [rt_hs[h] + _dot(jnp.where(right_half, 0.0, bot[h]), jnp.concatenate([ah[h], ah[h]], 0))
          for h in heads]
    vv = [jnp.concatenate([vh[h], v_hs[h]], 0) for h in heads]
    yh = [_dot(bot[h], vv[h]) for h in heads]
    m = [_dot_tn(bb[:, sls[h]], ah[h]) + eye[:hd, :hd] * wtot[:, sls[h]] for h in heads]
    g = [_dot_tn(jnp.concatenate([bb[:, sls[h]], kb[:, sls[h]]], 0), vv[h]) for h in heads]
    hs = [h_scr[h] for h in heads]
    fin = [_dot(jnp.concatenate([rh[h], m[h]], 0), hs[h]) for h in heads]
    for h in heads:
        y_ref[0, :, sls[h]] = fin[h][:L] + yh[h]
        h_scr[h] = fin[h][L:] + g[h]


def wkv7(r, lw, k, v, a, b, reverse):
    bsz, s, d = r.shape
    hd = RWKV_HEAD
    nh = d // hd
    L = min(WKV_CHUNK, s)
    assert L == hd and s % L == 0
    nc = s // L
    if reverse:
        idx = lambda bi, ci: (bi, nc - 1 - ci, 0)
    else:
        idx = lambda bi, ci: (bi, ci, 0)
    spec = pl.BlockSpec((1, L, d), idx)
    return pl.pallas_call(
        functools.partial(_wkv_kernel, reverse=reverse, nh=nh, hd=hd),
        out_shape=jax.ShapeDtypeStruct((bsz, s, d), F32),
        grid=(bsz, nc),
        in_specs=[spec] * 6,
        out_specs=spec,
        scratch_shapes=[pltpu.VMEM((nh, hd, hd), F32)],
        compiler_params=_params("parallel", "arbitrary"),
        name="wkv7_rev" if reverse else "wkv7_fwd",
    )(r, lw, k, v, a, b)


def _rwkv_post_kernel(yf_ref, yb_ref, bon_ref, g_ref, x_ref, vec_ref, wo_ref, o_ref, *, seg):
    y = yf_ref[...] + yb_ref[...]
    yc = y - _seg_sum(y, seg) * (1.0 / seg)
    var = _seg_sum(yc * yc, seg) * (1.0 / seg)
    vec = vec_ref[...]
    yn = yc * lax.rsqrt(var + GN_EPS) * vec[0:1] + vec[1:2] + bon_ref[...].astype(F32)
    o_ref[...] = x_ref[...] + _dot(yn * g_ref[...].astype(F32), wo_ref[...])


def rwkv_post(yf, yb, bonus, g, x, gn_g, gn_b, w_o, bm=512):
    m, d = x.shape
    bm = min(bm, m)
    vec = _pad_to(jnp.stack([gn_g, gn_b]), 0, SUBLANES)
    tile = pl.BlockSpec((bm, d), lambda i: (i, 0))
    return pl.pallas_call(
        functools.partial(_rwkv_post_kernel, seg=RWKV_HEAD),
        out_shape=jax.ShapeDtypeStruct((m, d), F32),
        grid=(m // bm,),
        in_specs=[tile] * 5 + [_const_spec(vec.shape), _const_spec(w_o.shape)],
        out_specs=tile,
        compiler_params=_params("parallel"),
        name="rwkv_post",
    )(yf, yb, bonus, g, x, vec, w_o.astype(BF16))


def _head_norm_rope(y, gain, cf, sa, sb, o_ref, nh):
    half = QK_ROPE // 2
    inv_dim = 1.0 / (QK_NOPE + QK_ROPE)
    for h in range(nh):
        sl = slice(h * HEAD_PAD, (h + 1) * HEAD_PAD)
        yh = y[:, sl]
        ms = jnp.sum(yh * yh, axis=-1, keepdims=True) * inv_dim
        yn = yh * lax.rsqrt(ms + NORM_EPS) * gain
        rot = yn * cf + pltpu.roll(yn, HEAD_PAD - half, 1) * sa + pltpu.roll(yn, half, 1) * sb
        o_ref[:, sl] = rot.astype(o_ref.dtype)


def _mla_pre_kernel(x_ref, gx_ref, gc_ref, hg_ref, cf_ref, sa_ref, sb_ref, win_ref, wq_ref, wk_ref, wv_ref,
                    q_ref, k_ref, v_ref, *, nh):
    gc = gc_ref[...]
    h = _rms(x_ref[...], gx_ref[...])
    c = _dot(h, win_ref[...])
    c_q = _rms(c[:, :Q_LORA], gc[0:1])
    c_kv = _rms(c[:, Q_LORA:Q_LORA + KV_LORA], gc[1:2, :KV_LORA]).astype(BF16)
    cf, sa, sb = cf_ref[...], sa_ref[...], sb_ref[...]
    hg = hg_ref[...]
    _head_norm_rope(_dot(c_q, wq_ref[...]), hg[0:1], cf, sa, sb, q_ref, nh)
    k_in = jnp.concatenate([c_kv, c[:, Q_LORA + KV_LORA:].astype(BF16)], axis=1)
    _head_norm_rope(_dot(k_in, wk_ref[...]), hg[1:2], cf, sa, sb, k_ref, nh)
    v_ref[...] = _dot(c_kv, wv_ref[...]).astype(v_ref.dtype)


def _pad_heads(w, nh, width, take):
    k = w.shape[0]
    wh = w.reshape(k, nh, width)[:, :, take]
    wh = jnp.pad(wh, ((0, 0), (0, 0), (0, HEAD_PAD - wh.shape[-1])))
    return wh.reshape(k, nh * HEAD_PAD)


def mla_pre(x, tables, norm_g, w_in, q_norm_g, kv_norm_g, w_uq, w_ukv, q_head_g, k_head_g, bm=256):
    m, d = x.shape
    bm = min(bm, m)
    qk = QK_NOPE + QK_ROPE
    nh = w_uq.shape[1] // qk
    assert Q_LORA % LANES == 0 and KV_LORA % LANES == 0 and KV_LORA <= Q_LORA
    gx = norm_g.reshape(1, d)
    gc = _pad_to(jnp.stack([q_norm_g, jnp.pad(kv_norm_g, (0, Q_LORA - KV_LORA))]), 0, SUBLANES)
    padh = lambda t: jnp.pad(t, (0, HEAD_PAD - qk))
    hg = _pad_to(jnp.stack([padh(q_head_g * qk ** -0.5), padh(k_head_g)]), 0, SUBLANES)
    win = _pad_to(w_in, 1, LANES)
    rope_w = win.shape[1] - Q_LORA - KV_LORA
    wq = _pad_heads(w_uq, nh, qk, slice(0, qk))
    wk = _pad_heads(w_ukv, nh, QK_NOPE + V_HEAD, slice(0, QK_NOPE))
    place = jnp.zeros((rope_w, nh, HEAD_PAD), F32)
    place = place.at[jnp.arange(QK_ROPE), :, QK_NOPE + jnp.arange(QK_ROPE)].set(1.0)
    wk = jnp.concatenate([wk, place.reshape(rope_w, nh * HEAD_PAD)], 0)
    wv = w_ukv.reshape(KV_LORA, nh, QK_NOPE + V_HEAD)[:, :, QK_NOPE:].reshape(KV_LORA, nh * V_HEAD)
    weights = [t.astype(BF16) for t in (win, wq, wk, wv)]
    tab = pl.BlockSpec((bm, HEAD_PAD), lambda i: (i, 0))
    row = lambda n: pl.BlockSpec((bm, n), lambda i: (i, 0))
    return pl.pallas_call(
        functools.partial(_mla_pre_kernel, nh=nh),
        out_shape=(jax.ShapeDtypeStruct((m, nh * HEAD_PAD), BF16),
                   jax.ShapeDtypeStruct((m, nh * HEAD_PAD), BF16),
                   jax.ShapeDtypeStruct((m, nh * V_HEAD), BF16)),
        grid=(m // bm,),
        in_specs=[row(d), _const_spec(gx.shape), _const_spec(gc.shape), _const_spec(hg.shape), tab, tab, tab]
                 + [_const_spec(t.shape) for t in weights],
        out_specs=(row(nh * HEAD_PAD), row(nh * HEAD_PAD), row(nh * V_HEAD)),
        compiler_params=_params("parallel"),
        name="mla_pre",
    )(x, gx, gc, hg, *tables, *weights)


def _attn_kernel(q_ref, k_ref, v_ref, o_ref):
    q, k, v = q_ref[0], k_ref[0], v_ref[0]
    lane = lax.broadcasted_iota(jnp.int32, (1, 2 * V_HEAD), 1)
    acc = jnp.zeros((q.shape[0], 2 * V_HEAD), F32)
    for j in range(2):
        sl = slice(j * HEAD_PAD, (j + 1) * HEAD_PAD)
        s = _dot_nt(q[:, sl], k[:, sl])
        p = jnp.exp(s - jnp.max(s, axis=-1, keepdims=True))
        den = jnp.sum(p, axis=-1, keepdims=True)
        vj = jnp.where((lane >= j * V_HEAD) & (lane < (j + 1) * V_HEAD), v, jnp.zeros_like(v))
        acc = acc + _dot(p, vj) / den
    o_ref[0] = acc.astype(o_ref.dtype)


def attention(q, k, v, tq=256):
    bsz, s, _ = q.shape
    nh = v.shape[-1] // V_HEAD
    tq = min(tq, s)
    return pl.pallas_call(
        _attn_kernel,
        out_shape=jax.ShapeDtypeStruct((bsz, s, nh * V_HEAD), BF16),
        grid=(bsz, nh // 2, s // tq),
        in_specs=[pl.BlockSpec((1, tq, 2 * HEAD_PAD), lambda b, h, i: (b, i, h)),
                  pl.BlockSpec((1, s, 2 * HEAD_PAD), lambda b, h, i: (b, 0, h)),
                  pl.BlockSpec((1, s, 2 * V_HEAD), lambda b, h, i: (b, 0, h))],
        out_specs=pl.BlockSpec((1, tq, 2 * V_HEAD), lambda b, h, i: (b, i, h)),
        compiler_params=_params("parallel", "parallel", "parallel"),
        name="mla_attention",
    )(q, k, v)


def _proj_res_kernel(a_ref, w_ref, x_ref, o_ref):
    o_ref[...] = x_ref[...] + _dot(a_ref[...], w_ref[...])


def proj_residual(a, w, x, bm=512):
    m, k = a.shape
    n = w.shape[1]
    bm = min(bm, m)
    return pl.pallas_call(
        _proj_res_kernel,
        out_shape=jax.ShapeDtypeStruct((m, n), F32),
        grid=(m // bm,),
        in_specs=[pl.BlockSpec((bm, k), lambda i: (i, 0)), _const_spec(w.shape),
                  pl.BlockSpec((bm, n), lambda i: (i, 0))],
        out_specs=pl.BlockSpec((bm, n), lambda i: (i, 0)),
        compiler_params=_params("parallel"),
        name="proj_residual",
    )(a, w.astype(BF16), x)


def _route_kernel(x_ref, g_ref, wr_ref, pos_ref, gate_ref, hn_ref, *, cap):
    h = _rms(x_ref[0], g_ref[...])
    hn_ref[0] = h.astype(hn_ref.dtype)
    wr = wr_ref[...]
    s = h.shape[0]
    ne = wr.shape[0]
    h_hi, h_lo = _split_bf16(h)
    w_hi, w_lo = _split_bf16(wr)
    logits = (lax.dot_general(w_hi, h_hi, NT_DIMS, preferred_element_type=F32)
              + lax.dot_general(w_hi, h_lo, NT_DIMS, preferred_element_type=F32)
              + lax.dot_general(w_lo, h_hi, NT_DIMS, preferred_element_type=F32))
    ex = jnp.exp(logits - jnp.max(logits, axis=0, keepdims=True))
    aff = ex / jnp.sum(ex, axis=0, keepdims=True)
    gate_ref[0] = aff

    bits = pltpu.bitcast(aff, jnp.int32)

    def count(mask):
        return jnp.sum(jnp.where(mask, 1.0, 0.0), axis=1, keepdims=True)

    def thr_step(i, prefix):
        cand = prefix | lax.shift_left(jnp.int32(1), 30 - i)
        return jnp.where(count(bits >= cand) >= cap, cand, prefix)

    thr = lax.fori_loop(0, 31, thr_step, jnp.zeros((ne, 1), jnp.int32))
    gt = bits > thr
    eq = bits == thr
    need = cap - count(gt)
    idx = lax.broadcasted_iota(jnp.int32, (ne, s), 1)

    def cut_step(i, p):
        cand = p + lax.shift_left(jnp.int32(1), (s.bit_length() - 1) - i)
        return jnp.where(count(eq & (idx < cand)) <= need, cand, p)

    cut = lax.fori_loop(0, s.bit_length(), cut_step, jnp.zeros((ne, 1), jnp.int32))
    sel = gt | (eq & (idx < cut))

    blk = LANES if s % LANES == 0 else s
    ri = lax.broadcasted_iota(jnp.int32, (blk, blk), 0)
    ci = lax.broadcasted_iota(jnp.int32, (blk, blk), 1)
    upper = jnp.where(ri <= ci, 1.0, 0.0).astype(BF16)
    off = jnp.zeros((ne, 1), F32)
    for j in range(s // blk):
        sl = slice(j * blk, (j + 1) * blk)
        sel_j = sel[:, sl]
        inc = jnp.dot(jnp.where(sel_j, 1.0, 0.0).astype(BF16), upper, preferred_element_type=F32)
        pos_ref[0, :, sl] = jnp.where(sel_j, (off + inc - 1.0).astype(jnp.int32), -1)
        off = off + inc[:, blk - 1:blk]


def route(x, norm_g, w_router, cap):
    bsz, s, d = x.shape
    ne = w_router.shape[1]
    out = pl.BlockSpec((1, ne, s), lambda b: (b, 0, 0))
    tok = pl.BlockSpec((1, s, d), lambda b: (b, 0, 0))
    return pl.pallas_call(
        functools.partial(_route_kernel, cap=cap),
        out_shape=(jax.ShapeDtypeStruct((bsz, ne, s), jnp.int32),
                   jax.ShapeDtypeStruct((bsz, ne, s), F32),
                   jax.ShapeDtypeStruct((bsz, s, d), BF16)),
        grid=(bsz,),
        in_specs=[tok, _const_spec((1, d)), _const_spec((ne, d))],
        out_specs=(out, out, tok),
        compiler_params=_params("parallel"),
        name="ec_route",
    )(x, norm_g.reshape(1, d), w_router.T)


def _selection(pos_row, cap):
    return pos_row == lax.broadcasted_iota(jnp.int32, (cap, pos_row.shape[1]), 0)


def _expert_ffn_kernel(h_ref, pos_ref, gate_ref, wg_ref, wu_ref, wd_ref, ye_ref, wg_s, wu_s, wd_s, *, cap):
    @pl.when(pl.program_id(1) == 0)
    def _():
        wg_s[...] = wg_ref[0].astype(BF16)
        wu_s[...] = wu_ref[0].astype(BF16)
        wd_s[...] = wd_ref[0].astype(BF16)

    onehot = _selection(pos_ref[0, 0], cap)
    sel = jnp.where(onehot, 1.0, 0.0).astype(BF16)
    gcol = jnp.sum(jnp.where(onehot, gate_ref[0, 0], 0.0), axis=1, keepdims=True)
    xe = jnp.dot(sel, h_ref[0], preferred_element_type=F32).astype(BF16)
    hg = jnp.dot(xe, wg_s[...], preferred_element_type=F32)
    hu = jnp.dot(xe, wu_s[...], preferred_element_type=F32)
    hid = (hg * jax.nn.sigmoid(hg) * hu).astype(BF16)
    ye_ref[0, 0] = (jnp.dot(hid, wd_s[...], preferred_element_type=F32) * gcol).astype(ye_ref.dtype)


def _expert_scatter_kernel(x_ref, pos_ref, ye_ref, o_ref, *, cap, eg):
    @pl.when(pl.program_id(1) == 0)
    def _():
        o_ref[...] = x_ref[...]

    sel = jnp.concatenate(
        [jnp.where(_selection(pos_ref[0, g], cap), 1.0, 0.0).astype(BF16) for g in range(eg)], axis=0)
    ye = ye_ref[0].reshape(eg * cap, ye_ref.shape[-1])
    o_ref[0] += lax.dot_general(sel, ye, TN_DIMS, preferred_element_type=F32)


def expert_choice_ffn(x, norm_g, w_router, w_gate, w_up, w_down, eg=4):
    bsz, s, d = x.shape
    ne, _, ff = w_gate.shape
    cap = EC_CAPACITY * s // ne
    eg = min(eg, ne)
    pos, gate, hn = route(x, norm_g, w_router, cap)
    pos4 = pos.reshape(bsz, ne, 1, s)
    gate4 = gate.reshape(bsz, ne, 1, s)
    row = pl.BlockSpec((1, 1, 1, s), lambda e, b: (b, e, 0, 0))
    wspec = lambda shape: pl.BlockSpec((1,) + shape, lambda e, b: (e, 0, 0))
    ye = pl.pallas_call(
        functools.partial(_expert_ffn_kernel, cap=cap),
        out_shape=jax.ShapeDtypeStruct((bsz, ne, cap, d), BF16),
        grid=(ne, bsz),
        in_specs=[pl.BlockSpec((1, s, d), lambda e, b: (b, 0, 0)), row, row,
                  wspec((d, ff)), wspec((d, ff)), wspec((ff, d))],
        out_specs=pl.BlockSpec((1, 1, cap, d), lambda e, b: (b, e, 0, 0)),
        scratch_shapes=[pltpu.VMEM((d, ff), BF16), pltpu.VMEM((d, ff), BF16), pltpu.VMEM((ff, d), BF16)],
        compiler_params=_params("arbitrary", "arbitrary"),
        name="ec_expert_ffn",
    )(hn, pos4, gate4, w_gate, w_up, w_down)
    tok = pl.BlockSpec((1, s, d), lambda b, j: (b, 0, 0))
    return pl.pallas_call(
        functools.partial(_expert_scatter_kernel, cap=cap, eg=eg),
        out_shape=jax.ShapeDtypeStruct((bsz, s, d), F32),
        grid=(bsz, ne // eg),
        in_specs=[tok, pl.BlockSpec((1, eg, 1, s), lambda b, j: (b, j, 0, 0)),
                  pl.BlockSpec((1, eg, cap, d), lambda b, j: (b, j, 0, 0))],
        out_specs=tok,
        compiler_params=_params("parallel", "arbitrary"),
        name="ec_expert_scatter",
    )(x, pos4, ye)


def rwkv7_layer(x, v_first, norm_g, mu, w_r, w_k, w_v, w_o, w0, w1, w2, a0, a1, a2,
                g1, g2, k_k, k_a, r_k, gn_g, gn_b, vres):
    bsz, s, d = x.shape
    r, na, v, g, bonus, lw, k_d, b_d, v_first = rwkv_pre(
        x, v_first, norm_g, mu, w_r, w_k, w_v, w0, w1, w2, a0, a1, a2, g1, g2, k_k, k_a, r_k, vres)
    yf = wkv7(r, lw[0], k_d[0], v, na, b_d[0], reverse=False)
    yb = wkv7(r, lw[1], k_d[1], v, na, b_d[1], reverse=True)
    flat = lambda t: t.reshape(bsz * s, d)
    out = rwkv_post(flat(yf), flat(yb), flat(bonus), flat(g), flat(x), gn_g, gn_b, w_o)
    return out.reshape(bsz, s, d), v_first


def rope_tables(positions):
    inv_freq = ROPE_THETA ** (-jnp.arange(0, QK_ROPE, 2, dtype=F32) / QK_ROPE)
    ang = positions.astype(F32)[..., None] * inv_freq
    cos, sin = jnp.cos(ang), jnp.sin(ang)
    half = QK_ROPE // 2
    shape = cos.shape[:-1]
    ones = jnp.ones(shape + (QK_NOPE,), F32)
    z = lambda n: jnp.zeros(shape + (n,), F32)
    tail = HEAD_PAD - QK_NOPE - QK_ROPE
    cf = jnp.concatenate([ones, cos, cos, z(tail)], -1)
    sa = jnp.concatenate([z(QK_NOPE), -sin, z(half + tail)], -1)
    sb = jnp.concatenate([z(QK_NOPE + half), sin, z(tail)], -1)
    flat = lambda t: t.reshape(-1, HEAD_PAD)
    return flat(cf), flat(sa), flat(sb)


def mla_layer(x, tables, norm_g, w_in, q_norm_g, kv_norm_g, w_uq, w_ukv, q_head_g, k_head_g, w_o):
    bsz, s, d = x.shape
    xf = x.reshape(bsz * s, d)
    q, k, v = mla_pre(xf, tables, norm_g, w_in, q_norm_g, kv_norm_g, w_uq, w_ukv, q_head_g, k_head_g)
    to3 = lambda t: t.reshape(bsz, s, -1)
    o = attention(to3(q), to3(k), to3(v))
    return proj_residual(o.reshape(bsz * s, -1), w_o, xf).reshape(bsz, s, d)


def kernel(x, positions, norm_mix_g, norm_ffn_g,
           rw_mu, rw_wr, rw_wk, rw_wv, rw_wo, rw_w0, rw_w1, rw_w2,
           rw_a0, rw_a1, rw_a2, rw_g1, rw_g2, rw_kk, rw_ka, rw_rk,
           rw_gn_g, rw_gn_b, rw_v0, rw_v1, rw_v2,
           mla_w_in, mla_q_norm_g, mla_kv_norm_g, mla_w_uq, mla_w_ukv,
           mla_q_head_g, mla_k_head_g, mla_w_o,
           moe_router, moe_w_gate, moe_w_up, moe_w_down):
    depth = norm_mix_g.shape[0]
    n_mixers = 2
    tables = rope_tables(positions)
    v_first = None
    for i in range(depth):
        j = i // n_mixers
        if i % n_mixers == 0:
            vres = None if j == 0 else (rw_v0[j - 1], rw_v1[j - 1], rw_v2[j - 1])
            x, v_first = rwkv7_layer(
                x, v_first, norm_mix_g[i], rw_mu[j], rw_wr[j], rw_wk[j], rw_wv[j], rw_wo[j],
                rw_w0[j], rw_w1[j], rw_w2[j], rw_a0[j], rw_a1[j], rw_a2[j],
                rw_g1[j], rw_g2[j], rw_kk[j], rw_ka[j], rw_rk[j],
                rw_gn_g[j], rw_gn_b[j], vres)
        else:
            x = mla_layer(x, tables, norm_mix_g[i], mla_w_in[j], mla_q_norm_g[j], mla_kv_norm_g[j],
                          mla_w_uq[j], mla_w_ukv[j], mla_q_head_g[j], mla_k_head_g[j], mla_w_o[j])
        x = expert_choice_ffn(x, norm_ffn_g[i], moe_router[i], moe_w_gate[i], moe_w_up[i], moe_w_down[i])
    return x
```

```python
import functools

import jax
import jax.numpy as jnp
from jax import lax
from jax.experimental import pallas as pl
from jax.experimental.pallas import tpu as pltpu

F32 = jnp.float32
BF16 = jnp.bfloat16

NORM_EPS = 1e-6
GN_EPS = 64e-5
RWKV_HEAD = 64
QK_NOPE = 64
QK_ROPE = 32
V_HEAD = 64
Q_LORA = 384
KV_LORA = 256
ROPE_THETA = 10000.0
EC_CAPACITY = 2
LANES = 128
SUBLANES = 8
HEAD_PAD = LANES
ONES_ROWS = 16
LOG2_E = 1.4426950408889634
WKV_CHUNK = 64
VMEM_LIMIT = 56 * 1024 * 1024

NT_DIMS = (((1,), (1,)), ((), ()))
TN_DIMS = (((0,), (0,)), ((), ()))


def _dot(a, b):
    return jnp.dot(a.astype(BF16), b.astype(BF16), preferred_element_type=F32)


def _dot_nt(a, b):
    return lax.dot_general(a.astype(BF16), b.astype(BF16), NT_DIMS, preferred_element_type=F32)


def _dot_tn(a, b):
    return lax.dot_general(a.astype(BF16), b.astype(BF16), TN_DIMS, preferred_element_type=F32)


def _split_bf16(x):
    hi = x.astype(BF16)
    lo = (x - hi.astype(F32)).astype(BF16)
    return hi, lo


def _rms(x, gain):
    return x * lax.rsqrt(jnp.mean(x * x, axis=-1, keepdims=True) + NORM_EPS) * gain


def _seg_sum(t, seg):
    width = min(LANES, t.shape[1])
    ri = lax.broadcasted_iota(jnp.int32, (width, width), 0)
    ci = lax.broadcasted_iota(jnp.int32, (width, width), 1)
    ones = jnp.where((ri // seg) == (ci // seg), 1.0, 0.0).astype(BF16)
    parts = [_dot(t[:, j:j + width], ones) for j in range(0, t.shape[1], width)]
    return parts[0] if len(parts) == 1 else jnp.concatenate(parts, axis=1)


def _softplus(u):
    return jnp.maximum(u, 0.0) + jnp.log(1.0 + jnp.exp(-jnp.abs(u)))


def _pad_to(w, axis, mult):
    pad = -w.shape[axis] % mult
    if pad == 0:
        return w
    widths = [(0, 0)] * w.ndim
    widths[axis] = (0, pad)
    return jnp.pad(w, widths)


def _const_spec(shape):
    return pl.BlockSpec(shape, lambda *_: (0,) * len(shape))


def _params(*sem):
    return pltpu.CompilerParams(dimension_semantics=sem, vmem_limit_bytes=VMEM_LIMIT)


def _rwkv_pre_kernel(*refs, has_vres, seg):
    it = iter(refs)
    (x_ref, xp_ref, xn_ref, vec_ref, wr_ref, wk_ref, wv_ref,
     w1_ref, a1_ref, g1_ref, w2_ref, a2_ref, g2_ref) = [next(it) for _ in range(13)]
    if has_vres:
        v1_ref, v2_ref, vf_ref = next(it), next(it), next(it)
    r_o, na_o, v_o, g_o, bon_o = [next(it) for _ in range(5)]
    lw_o = [next(it), next(it)]
    k_o = [next(it), next(it)]
    b_o = [next(it), next(it)]
    if not has_vres:
        vf_o = next(it)

    i = pl.program_id(1)
    vec = vec_ref[...]
    row_of = lambda j: vec[j:j + 1]
    gain = row_of(0)
    x = x_ref[0]
    bm = x.shape[0]
    h = _rms(x, gain)
    hp = jnp.where(i == 0, 0.0, _rms(xp_ref[0][SUBLANES - 1:SUBLANES], gain))
    hn = jnp.where(i == pl.num_programs(1) - 1, 0.0, _rms(xn_ref[0][0:1], gain))
    row = lax.broadcasted_iota(jnp.int32, (bm, 1), 0)
    h_prev = jnp.where(row == 0, hp, pltpu.roll(h, 1, 0))
    h_next = jnp.where(row == bm - 1, hn, pltpu.roll(h, bm - 1, 0))
    xx = 0.5 * (h_prev + h_next) - h
    xr, xw, xk, xv, xa, xg = [(h + xx * row_of(1 + j)).astype(BF16) for j in range(6)]

    r = _dot(xr, wr_ref[...])
    k = _dot(xk, wk_ref[...])
    v = _dot(xv, wv_ref[...])
    if has_vres:
        vz = row_of(14) + _dot(_dot(xv, v1_ref[...]), v2_ref[...])
        v = v + (vf_ref[0] - v) * jax.nn.sigmoid(vz)
    else:
        vf_o[0] = v
    g = _dot(jax.nn.sigmoid(_dot(xg, g1_ref[...])), g2_ref[...])

    kkr = k * row_of(11)
    kk = kkr / jnp.maximum(jnp.sqrt(_seg_sum(kkr * kkr, seg)), 1e-12)
    tw = jnp.tanh(_dot(xw, w1_ref[...])).astype(BF16)
    al = _dot(xa, a1_ref[...]).astype(BF16)
    k_sum = jnp.zeros_like(k)
    for d in range(2):
        z = row_of(7 + d) + _dot(tw, w2_ref[d])
        lw_o[d][0] = -jnp.exp(-_softplus(-z) - 0.5)
        a = jax.nn.sigmoid(row_of(9 + d) + _dot(al, a2_ref[d]))
        k_d = k * (1.0 + (a - 1.0) * row_of(12))
        k_o[d][0] = k_d.astype(BF16)
        b_o[d][0] = (kk * a).astype(BF16)
        k_sum = k_sum + k_d
    bon_o[0] = (_seg_sum(r * k_sum * row_of(13), seg) * v).astype(BF16)
    r_o[0] = r.astype(BF16)
    na_o[0] = (-kk).astype(BF16)
    v_o[0] = v.astype(BF16)
    g_o[0] = g.astype(BF16)


def rwkv_pre(x, v_first, norm_g, mu, w_r, w_k, w_v, w0, w1, w2, a0, a1, a2, g1, g2, k_k, k_a, r_k,
             vres, bm=256):
    bsz, s, d = x.shape
    bm = min(bm, s)
    nt = s // bm
    has_vres = vres is not None
    zero = jnp.zeros((d,), F32)
    v0 = vres[0] if has_vres else zero
    vec = jnp.stack([norm_g, *mu, w0[0], w0[1], a0[0], a0[1], k_k, k_a, r_k.reshape(d), v0, zero])
    bf = lambda t: t.astype(BF16)
    cat2 = lambda t: jnp.concatenate([t[0], t[1]], axis=1)
    lo = w1.shape[2]
    second = lambda t: jnp.stack([jnp.pad(t[0], ((0, lo), (0, 0))), jnp.pad(t[1], ((lo, 0), (0, 0)))])
    ins = [x, x, x, vec, bf(w_r), bf(w_k), bf(w_v), bf(cat2(w1)), bf(cat2(a1)), bf(_pad_to(g1, 1, LANES)),
           bf(second(w2)), bf(second(a2)), bf(_pad_to(g2, 0, LANES))]
    tile = pl.BlockSpec((1, bm, d), lambda b, i: (b, i, 0))
    hb = bm // SUBLANES
    in_specs = [tile,
                pl.BlockSpec((1, SUBLANES, d), lambda b, i: (b, jnp.maximum(i * hb - 1, 0), 0)),
                pl.BlockSpec((1, SUBLANES, d), lambda b, i: (b, jnp.minimum((i + 1) * hb, s // SUBLANES - 1), 0))]
    in_specs += [_const_spec(t.shape) for t in ins[3:]]
    if has_vres:
        extra = [bf(_pad_to(vres[1], 1, LANES)), bf(_pad_to(vres[2], 0, LANES))]
        ins += extra + [v_first]
        in_specs += [_const_spec(t.shape) for t in extra] + [tile]
    n_bf, n_f32 = 5, 2
    out_shape = ([jax.ShapeDtypeStruct((bsz, s, d), BF16)] * n_bf + [jax.ShapeDtypeStruct((bsz, s, d), F32)] * n_f32
                 + [jax.ShapeDtypeStruct((bsz, s, d), BF16)] * 4)
    if not has_vres:
        out_shape.append(jax.ShapeDtypeStruct((bsz, s, d), F32))
    outs = pl.pallas_call(
        functools.partial(_rwkv_pre_kernel, has_vres=has_vres, seg=RWKV_HEAD),
        out_shape=tuple(out_shape),
        grid=(bsz, nt),
        in_specs=in_specs,
        out_specs=tuple([tile] * len(out_shape)),
        compiler_params=_params("parallel", "parallel"),
        name="rwkv_pre",
    )(*ins)
    r, na, v, g, bonus, lw0, lw1, k0, k1, b0, b1 = outs[:11]
    v_first = v_first if has_vres else outs[11]
    return r, na, v, g, bonus, (lw0, lw1), (k0, k1), (b0, b1), v_first


def _wkv_kernel(r_ref, lw_ref, k_ref, v_ref, a_ref, b_ref, y_ref, h_scr, *, reverse, nh, hd):
    c = pl.program_id(1)

    @pl.when(c == 0)
    def _():
        h_scr[...] = jnp.zeros_like(h_scr)

    lw = lw_ref[0]
    L = lw.shape[0]
    row = lax.broadcasted_iota(jnp.int32, (L, L), 0)
    col = lax.broadcasted_iota(jnp.int32, (L, L), 1)
    incl = (col >= row) if reverse else (col <= row)
    tri = jnp.where(incl, 1.0, 0.0).astype(BF16)
    eye = jnp.where(row == col, 1.0, 0.0).astype(F32)

    lw_hi, lw_lo = _split_bf16(lw)
    cs = (jnp.dot(tri, lw_hi, preferred_element_type=F32)
          + jnp.dot(tri, lw_lo, preferred_element_type=F32))
    last = 0 if reverse else L - 1
    ctot = cs[last:last + 1, :]

    r, k, v, a, b = [t[0].astype(F32) for t in (r_ref, k_ref, v_ref, a_ref, b_ref)]
    rt = r * jnp.exp(cs)
    at = a * jnp.exp(cs - lw)
    einv = jnp.exp(-cs)
    bt = b * einv
    kt = k * einv
    etot = jnp.exp(ctot - cs)
    bb = b * etot
    kb = k * etot
    wtot = jnp.exp(ctot)

    row2 = lax.broadcasted_iota(jnp.int32, (2 * L, 2 * L), 0)
    col2 = lax.broadcasted_iota(jnp.int32, (2 * L, 2 * L), 1)
    rt2, cs2 = row2 & (L - 1), col2 & (L - 1)
    before = (cs2 > rt2) if reverse else (cs2 < rt2)
    score_mask = before | ((row2 >= L) & (cs2 == rt2))
    right_half = lax.broadcasted_iota(jnp.int32, (L, 2 * L), 1) >= L
    diag_blk = (row ^ col) < 2
    level_masks = []
    size = 2
    while size < L:
        x = row ^ col
        level_masks.append((x >= size) & (x < 2 * size))
        size *= 2

    heads = range(nh)
    sls = [slice(h * hd, (h + 1) * hd) for h in heads]
    v_hs = [v[:, sl] for sl in sls]
    at_hs = [at[:, sl] for sl in sls]
    rt_hs = [rt[:, sl] for sl in sls]
    sc = [jnp.where(score_mask,
                    _dot_nt(jnp.concatenate([at_hs[h], rt_hs[h]], 0),
                            jnp.concatenate([bt[:, sls[h]], kt[:, sls[h]]], 0)), 0.0)
          for h in heads]
    top = [s_[:L] for s_ in sc]
    bot = [s_[L:] for s_ in sc]
    n_ab = [t_[:, :L] for t_ in top]
    t = [eye + jnp.where(diag_blk, n_, 0.0) for n_ in n_ab]
    for lm in level_masks:
        u = [_dot(jnp.where(lm, n_ab[h], 0.0), t[h]) for h in heads]
        t = [t[h] + _dot(t[h], u[h]) for h in heads]
    x = [_dot(jnp.where(right_half, top[h], 0.0), jnp.concatenate([v_hs[h], v_hs[h]], 0)) for h in heads]
    ah = [_dot(t[h], at_hs[h]) for h in heads]
    vh = [_dot(t[h], x[h]) for h in heads]
    rh = [rt_hs[h] + _dot(jnp.where(right_half, 0.0, bot[h]), jnp.concatenate([ah[h], ah[h]], 0))
          for h in heads]
    vv = [jnp.concatenate([vh[h], v_hs[h]], 0) for h in heads]
    yh = [_dot(bot[h], vv[h]) for h in heads]
    m = [_dot_tn(bb[:, sls[h]], ah[h]) + eye[:hd, :hd] * wtot[:, sls[h]] for h in heads]
    g = [_dot_tn(jnp.concatenate([bb[:, sls[h]], kb[:, sls[h]]], 0), vv[h]) for h in heads]
    hs = [h_scr[h] for h in heads]
    fin = [_dot(jnp.concatenate([rh[h], m[h]], 0), hs[h]) for h in heads]
    for h in heads:
        y_ref[0, :, sls[h]] = fin[h][:L] + yh[h]
        h_scr[h] = fin[h][L:] + g[h]


def wkv7(r, lw, k, v, a, b, reverse):
    bsz, s, d = r.shape
    hd = RWKV_HEAD
    nh = d // hd
    L = min(WKV_CHUNK, s)
    assert L == hd and s % L == 0
    nc = s // L
    if reverse:
        idx = lambda bi, ci: (bi, nc - 1 - ci, 0)
    else:
        idx = lambda bi, ci: (bi, ci, 0)
    spec = pl.BlockSpec((1, L, d), idx)
    return pl.pallas_call(
        functools.partial(_wkv_kernel, reverse=reverse, nh=nh, hd=hd),
        out_shape=jax.ShapeDtypeStruct((bsz, s, d), F32),
        grid=(bsz, nc),
        in_specs=[spec] * 6,
        out_specs=spec,
        scratch_shapes=[pltpu.VMEM((nh, hd, hd), F32)],
        compiler_params=_params("parallel", "arbitrary"),
        name="wkv7_rev" if reverse else "wkv7_fwd",
    )(r, lw, k, v, a, b)


def _rwkv_post_kernel(yf_ref, yb_ref, bon_ref, g_ref, x_ref, vec_ref, wo_ref, o_ref, *, seg):
    y = yf_ref[...] + yb_ref[...]
    yc = y - _seg_sum(y, seg) * (1.0 / seg)
    var = _seg_sum(yc * yc, seg) * (1.0 / seg)
    vec = vec_ref[...]
    yn = yc * lax.rsqrt(var + GN_EPS) * vec[0:1] + vec[1:2] + bon_ref[...].astype(F32)
    o_ref[...] = x_ref[...] + _dot(yn * g_ref[...].astype(F32), wo_ref[...])


def rwkv_post(yf, yb, bonus, g, x, gn_g, gn_b, w_o, bm=512):
    m, d = x.shape
    bm = min(bm, m)
    vec = _pad_to(jnp.stack([gn_g, gn_b]), 0, SUBLANES)
    tile = pl.BlockSpec((bm, d), lambda i: (i, 0))
    return pl.pallas_call(
        functools.partial(_rwkv_post_kernel, seg=RWKV_HEAD),
        out_shape=jax.ShapeDtypeStruct((m, d), F32),
        grid=(m // bm,),
        in_specs=[tile] * 5 + [_const_spec(vec.shape), _const_spec(w_o.shape)],
        out_specs=tile,
        compiler_params=_params("parallel"),
        name="rwkv_post",
    )(yf, yb, bonus, g, x, vec, w_o.astype(BF16))


def _head_norm_rope(y2, ctab, stab, o_ref, nh):
    inv_dim = 1.0 / (QK_NOPE + QK_ROPE)
    for h in range(nh):
        sl = slice(h * HEAD_PAD, (h + 1) * HEAD_PAD)
        rh = slice((nh + h) * HEAD_PAD, (nh + h + 1) * HEAD_PAD)
        yh = y2[:, sl]
        ms = jnp.sum(yh * yh, axis=-1, keepdims=True) * inv_dim
        o_ref[0, :, sl] = ((yh * ctab + y2[:, rh] * stab) * lax.rsqrt(ms + NORM_EPS)).astype(o_ref.dtype)


def _mla_pre_kernel(x_ref, gx_ref, gc_ref, hg_ref, cf_ref, sf_ref, win_ref, wq_ref, wk_ref, wvt_ref,
                    q_ref, k_ref, vt_ref, *, nh):
    gc = gc_ref[...]
    h = _rms(x_ref[0], gx_ref[...])
    c = _dot(h, win_ref[...])
    c_q = _rms(c[:, :Q_LORA], gc[0:1])
    c_kv = _rms(c[:, Q_LORA:Q_LORA + KV_LORA], gc[1:2, :KV_LORA]).astype(BF16)
    cf, sf = cf_ref[...], sf_ref[...]
    hg = hg_ref[...]
    _head_norm_rope(_dot(c_q, wq_ref[...]), cf * hg[0:1], sf * hg[1:2], q_ref, nh)
    k_in = jnp.concatenate([c_kv, c[:, Q_LORA + KV_LORA:].astype(BF16)], axis=1)
    _head_norm_rope(_dot(k_in, wk_ref[...]), cf * hg[2:3], sf * hg[3:4], k_ref, nh)
    vt_ref[0] = _dot_nt(wvt_ref[...], c_kv).astype(vt_ref.dtype)


def _pad_heads(w, nh, width, take):
    k = w.shape[0]
    wh = w.reshape(k, nh, width)[:, :, take]
    wh = jnp.pad(wh, ((0, 0), (0, 0), (0, HEAD_PAD - wh.shape[-1])))
    return wh.reshape(k, nh * HEAD_PAD)


def _with_rotate_half(w_pad, nh):
    k = w_pad.shape[0]
    half = QK_ROPE // 2
    w3 = w_pad.reshape(k, nh, HEAD_PAD)
    x1 = w3[:, :, QK_NOPE:QK_NOPE + half]
    x2 = w3[:, :, QK_NOPE + half:QK_NOPE + QK_ROPE]
    z = lambda n: jnp.zeros((k, nh, n), w_pad.dtype)
    rh = jnp.concatenate([z(QK_NOPE), -x2, x1, z(HEAD_PAD - QK_NOPE - QK_ROPE)], axis=-1)
    return jnp.concatenate([w_pad, rh.reshape(k, nh * HEAD_PAD)], axis=1)


def _gain_rows(g):
    half = QK_ROPE // 2
    tail = jnp.zeros((HEAD_PAD - QK_NOPE - QK_ROPE,), F32)
    straight = jnp.concatenate([g, tail])
    swapped = jnp.concatenate([jnp.zeros((QK_NOPE,), F32), g[QK_NOPE + half:], g[QK_NOPE:QK_NOPE + half], tail])
    return [straight, swapped]


def mla_pre(x, tables, norm_g, w_in, q_norm_g, kv_norm_g, w_uq, w_ukv, q_head_g, k_head_g, bm=256):
    bsz, s, d = x.shape
    bm = min(bm, s)
    qk = QK_NOPE + QK_ROPE
    nh = w_uq.shape[1] // qk
    assert Q_LORA % LANES == 0 and KV_LORA % LANES == 0 and KV_LORA <= Q_LORA
    gx = norm_g.reshape(1, d)
    gc = _pad_to(jnp.stack([q_norm_g, jnp.pad(kv_norm_g, (0, Q_LORA - KV_LORA))]), 0, SUBLANES)
    hg = _pad_to(jnp.stack(_gain_rows(q_head_g * (qk ** -0.5 * LOG2_E)) + _gain_rows(k_head_g)), 0, SUBLANES)
    win = _pad_to(w_in, 1, LANES)
    rope_w = win.shape[1] - Q_LORA - KV_LORA
    wq = _with_rotate_half(_pad_heads(w_uq, nh, qk, slice(0, qk)), nh)
    wk = _pad_heads(w_ukv, nh, QK_NOPE + V_HEAD, slice(0, QK_NOPE))
    place = jnp.zeros((rope_w, nh, HEAD_PAD), F32)
    place = place.at[jnp.arange(QK_ROPE), :, QK_NOPE + jnp.arange(QK_ROPE)].set(1.0)
    wk = _with_rotate_half(jnp.concatenate([wk, place.reshape(rope_w, nh * HEAD_PAD)], 0), nh)
    wvt = w_ukv.reshape(KV_LORA, nh, QK_NOPE + V_HEAD)[:, :, QK_NOPE:].reshape(KV_LORA, nh * V_HEAD).T
    weights = [t.astype(BF16) for t in (win, wq, wk, wvt)]
    nt = s // bm
    tab = pl.BlockSpec((bm, HEAD_PAD), lambda b, i: (b * nt + i, 0))
    row = lambda n: pl.BlockSpec((1, bm, n), lambda b, i: (b, i, 0))
    return pl.pallas_call(
        functools.partial(_mla_pre_kernel, nh=nh),
        out_shape=(jax.ShapeDtypeStruct((bsz, s, nh * HEAD_PAD), BF16),
                   jax.ShapeDtypeStruct((bsz, s, nh * HEAD_PAD), BF16),
                   jax.ShapeDtypeStruct((bsz, nh * V_HEAD, s), BF16)),
        grid=(bsz, nt),
        in_specs=[row(d), _const_spec(gx.shape), _const_spec(gc.shape), _const_spec(hg.shape), tab, tab]
                 + [_const_spec(t.shape) for t in weights],
        out_specs=(row(nh * HEAD_PAD), row(nh * HEAD_PAD),
                   pl.BlockSpec((1, nh * V_HEAD, bm), lambda b, i: (b, 0, i))),
        compiler_params=_params("parallel", "parallel"),
        name="mla_pre",
    )(x, gx, gc, hg, *tables, *weights)


def _attn_kernel(q_ref, k_ref, vt_ref, o_ref, *, kc):
    s = k_ref.shape[1]
    heads = range(2)
    qs = [q_ref[0, :, j * HEAD_PAD:(j + 1) * HEAD_PAD] for j in heads]

    def scores(c):
        return [_dot_nt(k_ref[0, c * kc:(c + 1) * kc, j * HEAD_PAD:(j + 1) * HEAD_PAD], qs[j]) for j in heads]

    ones = jnp.ones((ONES_ROWS, kc), BF16)
    nxt = scores(0)
    m, acc = [None] * 2, [None] * 2
    for c in range(s // kc):
        cur = nxt
        if (c + 1) * kc < s:
            nxt = scores(c + 1)
        for j in heads:
            vt = jnp.concatenate([vt_ref[0, j * V_HEAD:(j + 1) * V_HEAD, c * kc:(c + 1) * kc], ones], axis=0)
            mc = jnp.max(cur[j], axis=0, keepdims=True)
            if c == 0:
                m[j] = mc
                acc[j] = jnp.dot(vt, jnp.exp2((cur[j] - mc).astype(BF16)), preferred_element_type=F32)
            else:
                m_new = jnp.maximum(m[j], mc)
                pt = jnp.exp2((cur[j] - m_new).astype(BF16))
                acc[j] = jnp.exp2(m[j] - m_new) * acc[j] + jnp.dot(vt, pt, preferred_element_type=F32)
                m[j] = m_new
    out = jnp.concatenate([acc[j][:V_HEAD] / acc[j][V_HEAD:V_HEAD + 1] for j in heads], axis=0)
    o_ref[0] = out.T.astype(o_ref.dtype)


def attention(q, k, vt, tq=512, kc=256):
    bsz, s, _ = q.shape
    nh = vt.shape[1] // V_HEAD
    tq = min(tq, s)
    kc = min(kc, s)
    return pl.pallas_call(
        functools.partial(_attn_kernel, kc=kc),
        out_shape=jax.ShapeDtypeStruct((bsz, s, nh * V_HEAD), BF16),
        grid=(bsz, nh // 2, s // tq),
        in_specs=[pl.BlockSpec((1, tq, 2 * HEAD_PAD), lambda b, h, i: (b, i, h)),
                  pl.BlockSpec((1, s, 2 * HEAD_PAD), lambda b, h, i: (b, 0, h)),
                  pl.BlockSpec((1, 2 * V_HEAD, s), lambda b, h, i: (b, h, 0))],
        out_specs=pl.BlockSpec((1, tq, 2 * V_HEAD), lambda b, h, i: (b, i, h)),
        compiler_params=_params("parallel", "parallel", "parallel"),
        name="mla_attention",
    )(q, k, vt)


def _proj_res_kernel(a_ref, w_ref, x_ref, o_ref):
    o_ref[...] = x_ref[...] + _dot(a_ref[...], w_ref[...])


def proj_residual(a, w, x, bm=512):
    m, k = a.shape
    n = w.shape[1]
    bm = min(bm, m)
    return pl.pallas_call(
        _proj_res_kernel,
        out_shape=jax.ShapeDtypeStruct((m, n), F32),
        grid=(m // bm,),
        in_specs=[pl.BlockSpec((bm, k), lambda i: (i, 0)), _const_spec(w.shape),
                  pl.BlockSpec((bm, n), lambda i: (i, 0))],
        out_specs=pl.BlockSpec((bm, n), lambda i: (i, 0)),
        compiler_params=_params("parallel"),
        name="proj_residual",
    )(a, w.astype(BF16), x)


def _route_kernel(x_ref, g_ref, wr_ref, pos_ref, gate_ref, hn_ref, *, cap):
    h = _rms(x_ref[0], g_ref[...])
    hn_ref[0] = h.astype(hn_ref.dtype)
    wr = wr_ref[...]
    s = h.shape[0]
    ne = wr.shape[0]
    h_hi, h_lo = _split_bf16(h)
    w_hi, w_lo = _split_bf16(wr)
    logits = (lax.dot_general(w_hi, h_hi, NT_DIMS, preferred_element_type=F32)
              + lax.dot_general(w_hi, h_lo, NT_DIMS, preferred_element_type=F32)
              + lax.dot_general(w_lo, h_hi, NT_DIMS, preferred_element_type=F32))
    ex = jnp.exp(logits - jnp.max(logits, axis=0, keepdims=True))
    aff = ex / jnp.sum(ex, axis=0, keepdims=True)
    gate_ref[0] = aff

    bits = pltpu.bitcast(aff, jnp.int32)

    def count(mask):
        return jnp.sum(jnp.where(mask, 1.0, 0.0), axis=1, keepdims=True)

    def thr_step(i, prefix):
        cand = prefix | lax.shift_left(jnp.int32(1), 30 - i)
        return jnp.where(count(bits >= cand) >= cap, cand, prefix)

    thr = lax.fori_loop(0, 31, thr_step, jnp.zeros((ne, 1), jnp.int32))
    gt = bits > thr
    eq = bits == thr
    need = cap - count(gt)
    idx = lax.broadcasted_iota(jnp.int32, (ne, s), 1)

    def cut_step(i, p):
        cand = p + lax.shift_left(jnp.int32(1), (s.bit_length() - 1) - i)
        return jnp.where(count(eq & (idx < cand)) <= need, cand, p)

    cut = lax.fori_loop(0, s.bit_length(), cut_step, jnp.zeros((ne, 1), jnp.int32))
    sel = gt | (eq & (idx < cut))

    blk = LANES if s % LANES == 0 else s
    ri = lax.broadcasted_iota(jnp.int32, (blk, blk), 0)
    ci = lax.broadcasted_iota(jnp.int32, (blk, blk), 1)
    upper = jnp.where(ri <= ci, 1.0, 0.0).astype(BF16)
    off = jnp.zeros((ne, 1), F32)
    for j in range(s // blk):
        sl = slice(j * blk, (j + 1) * blk)
        sel_j = sel[:, sl]
        inc = jnp.dot(jnp.where(sel_j, 1.0, 0.0).astype(BF16), upper, preferred_element_type=F32)
        pos_ref[0, :, sl] = jnp.where(sel_j, (off + inc - 1.0).astype(jnp.int32), -1)
        off = off + inc[:, blk - 1:blk]


def route(x, norm_g, w_router, cap):
    bsz, s, d = x.shape
    ne = w_router.shape[1]
    out = pl.BlockSpec((1, ne, s), lambda b: (b, 0, 0))
    tok = pl.BlockSpec((1, s, d), lambda b: (b, 0, 0))
    return pl.pallas_call(
        functools.partial(_route_kernel, cap=cap),
        out_shape=(jax.ShapeDtypeStruct((bsz, ne, s), jnp.int32),
                   jax.ShapeDtypeStruct((bsz, ne, s), F32),
                   jax.ShapeDtypeStruct((bsz, s, d), BF16)),
        grid=(bsz,),
        in_specs=[tok, _const_spec((1, d)), _const_spec((ne, d))],
        out_specs=(out, out, tok),
        compiler_params=_params("parallel"),
        name="ec_route",
    )(x, norm_g.reshape(1, d), w_router.T)


def _selection(pos_row, cap):
    return pos_row == lax.broadcasted_iota(jnp.int32, (cap, pos_row.shape[1]), 0)


def _expert_ffn_kernel(h_ref, pos_ref, gate_ref, wg_ref, wu_ref, wd_ref, ye_ref, wg_s, wu_s, wd_s, *, cap):
    @pl.when(pl.program_id(1) == 0)
    def _():
        wg_s[...] = wg_ref[0, 0].astype(BF16)
        wu_s[...] = wu_ref[0, 0].astype(BF16)
        wd_s[...] = wd_ref[0, 0].astype(BF16)

    onehot = _selection(pos_ref[0, 0], cap)
    sel = jnp.where(onehot, 1.0, 0.0).astype(BF16)
    gcol = jnp.sum(jnp.where(onehot, gate_ref[0, 0], 0.0), axis=1, keepdims=True)
    xe = jnp.dot(sel, h_ref[0], preferred_element_type=F32).astype(BF16)
    hg = jnp.dot(xe, wg_s[...], preferred_element_type=F32)
    hu = jnp.dot(xe, wu_s[...], preferred_element_type=F32)
    hid = (hg * jax.nn.sigmoid(hg) * hu).astype(BF16)
    ye_ref[0, 0] = (jnp.dot(hid, wd_s[...], preferred_element_type=F32) * gcol).astype(ye_ref.dtype)


def _expert_scatter_kernel(x_ref, pos_ref, ye_ref, o_ref, *, cap, eg):
    @pl.when(pl.program_id(1) == 0)
    def _():
        o_ref[...] = x_ref[...]

    sel = jnp.concatenate(
        [jnp.where(_selection(pos_ref[0, g], cap), 1.0, 0.0).astype(BF16) for g in range(eg)], axis=0)
    ye = ye_ref[0].reshape(eg * cap, ye_ref.shape[-1])
    o_ref[0] += lax.dot_general(sel, ye, TN_DIMS, preferred_element_type=F32)


def expert_choice_ffn(x, norm_g, w_router, w_gate, w_up, w_down, layer, eg=4):
    bsz, s, d = x.shape
    _, ne, _, ff = w_gate.shape
    cap = EC_CAPACITY * s // ne
    eg = min(eg, ne)
    pos, gate, hn = route(x, norm_g, w_router, cap)
    pos4 = pos.reshape(bsz, ne, 1, s)
    gate4 = gate.reshape(bsz, ne, 1, s)
    row = pl.BlockSpec((1, 1, 1, s), lambda e, b: (b, e, 0, 0))
    wspec = lambda shape: pl.BlockSpec((1, 1) + shape, lambda e, b: (layer, e, 0, 0))
    ye = pl.pallas_call(
        functools.partial(_expert_ffn_kernel, cap=cap),
        out_shape=jax.ShapeDtypeStruct((bsz, ne, cap, d), BF16),
        grid=(ne, bsz),
        in_specs=[pl.BlockSpec((1, s, d), lambda e, b: (b, 0, 0)), row, row,
                  wspec((d, ff)), wspec((d, ff)), wspec((ff, d))],
        out_specs=pl.BlockSpec((1, 1, cap, d), lambda e, b: (b, e, 0, 0)),
        scratch_shapes=[pltpu.VMEM((d, ff), BF16), pltpu.VMEM((d, ff), BF16), pltpu.VMEM((ff, d), BF16)],
        compiler_params=_params("arbitrary", "arbitrary"),
        name="ec_expert_ffn",
    )(hn, pos4, gate4, w_gate, w_up, w_down)
    tok = pl.BlockSpec((1, s, d), lambda b, j: (b, 0, 0))
    return pl.pallas_call(
        functools.partial(_expert_scatter_kernel, cap=cap, eg=eg),
        out_shape=jax.ShapeDtypeStruct((bsz, s, d), F32),
        grid=(bsz, ne // eg),
        in_specs=[tok, pl.BlockSpec((1, eg, 1, s), lambda b, j: (b, j, 0, 0)),
                  pl.BlockSpec((1, eg, cap, d), lambda b, j: (b, j, 0, 0))],
        out_specs=tok,
        compiler_params=_params("parallel", "arbitrary"),
        name="ec_expert_scatter",
    )(x, pos4, ye)


def rwkv7_layer(x, v_first, norm_g, mu, w_r, w_k, w_v, w_o, w0, w1, w2, a0, a1, a2,
                g1, g2, k_k, k_a, r_k, gn_g, gn_b, vres):
    bsz, s, d = x.shape
    r, na, v, g, bonus, lw, k_d, b_d, v_first = rwkv_pre(
        x, v_first, norm_g, mu, w_r, w_k, w_v, w0, w1, w2, a0, a1, a2, g1, g2, k_k, k_a, r_k, vres)
    yf = wkv7(r, lw[0], k_d[0], v, na, b_d[0], reverse=False)
    yb = wkv7(r, lw[1], k_d[1], v, na, b_d[1], reverse=True)
    flat = lambda t: t.reshape(bsz * s, d)
    out = rwkv_post(flat(yf), flat(yb), flat(bonus), flat(g), flat(x), gn_g, gn_b, w_o)
    return out.reshape(bsz, s, d), v_first


def rope_tables(positions):
    inv_freq = ROPE_THETA ** (-jnp.arange(0, QK_ROPE, 2, dtype=F32) / QK_ROPE)
    ang = positions.astype(F32)[..., None] * inv_freq
    cos, sin = jnp.cos(ang), jnp.sin(ang)
    shape = cos.shape[:-1]
    z = lambda n: jnp.zeros(shape + (n,), F32)
    tail = HEAD_PAD - QK_NOPE - QK_ROPE
    cf = jnp.concatenate([jnp.ones(shape + (QK_NOPE,), F32), cos, cos, z(tail)], -1)
    sf = jnp.concatenate([z(QK_NOPE), sin, sin, z(tail)], -1)
    return cf.reshape(-1, HEAD_PAD), sf.reshape(-1, HEAD_PAD)


def mla_layer(x, tables, norm_g, w_in, q_norm_g, kv_norm_g, w_uq, w_ukv, q_head_g, k_head_g, w_o):
    bsz, s, d = x.shape
    q, k, vt = mla_pre(x, tables, norm_g, w_in, q_norm_g, kv_norm_g, w_uq, w_ukv, q_head_g, k_head_g)
    o = attention(q, k, vt)
    return proj_residual(o.reshape(bsz * s, -1), w_o, x.reshape(bsz * s, d)).reshape(bsz, s, d)


def kernel(x, positions, norm_mix_g, norm_ffn_g,
           rw_mu, rw_wr, rw_wk, rw_wv, rw_wo, rw_w0, rw_w1, rw_w2,
           rw_a0, rw_a1, rw_a2, rw_g1, rw_g2, rw_kk, rw_ka, rw_rk,
           rw_gn_g, rw_gn_b, rw_v0, rw_v1, rw_v2,
           mla_w_in, mla_q_norm_g, mla_kv_norm_g, mla_w_uq, mla_w_ukv,
           mla_q_head_g, mla_k_head_g, mla_w_o,
           moe_router, moe_w_gate, moe_w_up, moe_w_down):
    depth = norm_mix_g.shape[0]
    n_mixers = 2
    tables = rope_tables(positions)
    v_first = None
    for i in range(depth):
        j = i // n_mixers
        if i % n_mixers == 0:
            vres = None if j == 0 else (rw_v0[j - 1], rw_v1[j - 1], rw_v2[j - 1])
            x, v_first = rwkv7_layer(
                x, v_first, norm_mix_g[i], rw_mu[j], rw_wr[j], rw_wk[j], rw_wv[j], rw_wo[j],
                rw_w0[j], rw_w1[j], rw_w2[j], rw_a0[j], rw_a1[j], rw_a2[j],
                rw_g1[j], rw_g2[j], rw_kk[j], rw_ka[j], rw_rk[j],
                rw_gn_g[j], rw_gn_b[j], vres)
        else:
            x = mla_layer(x, tables, norm_mix_g[i], mla_w_in[j], mla_q_norm_g[j], mla_kv_norm_g[j],
                          mla_w_uq[j], mla_w_ukv[j], mla_q_head_g[j], mla_k_head_g[j], mla_w_o[j])
        x = expert_choice_ffn(x, norm_ffn_g[i], moe_router[i], moe_w_gate, moe_w_up, moe_w_down, layer=i)
    return x
```

```python
import functools

import jax
import jax.numpy as jnp
from jax import lax
from jax.experimental import pallas as pl
from jax.experimental.pallas import tpu as pltpu

F32 = jnp.float32
BF16 = jnp.bfloat16

NORM_EPS = 1e-6
GN_EPS = 64e-5
RWKV_HEAD = 64
QK_NOPE = 64
QK_ROPE = 32
V_HEAD = 64
Q_LORA = 384
KV_LORA = 256
ROPE_THETA = 10000.0
EC_CAPACITY = 2
LANES = 128
SUBLANES = 8
HEAD_PAD = LANES
ONES_ROWS = 16
LOG2_E = 1.4426950408889634
WKV_CHUNK = 64
VMEM_LIMIT = 56 * 1024 * 1024

NT_DIMS = (((1,), (1,)), ((), ()))
TN_DIMS = (((0,), (0,)), ((), ()))


def _dot(a, b):
    return jnp.dot(a.astype(BF16), b.astype(BF16), preferred_element_type=F32)


def _dot_nt(a, b):
    return lax.dot_general(a.astype(BF16), b.astype(BF16), NT_DIMS, preferred_element_type=F32)


def _dot_tn(a, b):
    return lax.dot_general(a.astype(BF16), b.astype(BF16), TN_DIMS, preferred_element_type=F32)


def _split_bf16(x):
    hi = x.astype(BF16)
    lo = (x - hi.astype(F32)).astype(BF16)
    return hi, lo


def _rms(x, gain):
    return x * lax.rsqrt(jnp.mean(x * x, axis=-1, keepdims=True) + NORM_EPS) * gain


def _seg_sum(t, seg):
    width = min(LANES, t.shape[1])
    ri = lax.broadcasted_iota(jnp.int32, (width, width), 0)
    ci = lax.broadcasted_iota(jnp.int32, (width, width), 1)
    ones = jnp.where((ri // seg) == (ci // seg), 1.0, 0.0).astype(BF16)
    parts = [_dot(t[:, j:j + width], ones) for j in range(0, t.shape[1], width)]
    return parts[0] if len(parts) == 1 else jnp.concatenate(parts, axis=1)


def _softplus(u):
    return jnp.maximum(u, 0.0) + jnp.log(1.0 + jnp.exp(-jnp.abs(u)))


def _pad_to(w, axis, mult):
    pad = -w.shape[axis] % mult
    if pad == 0:
        return w
    widths = [(0, 0)] * w.ndim
    widths[axis] = (0, pad)
    return jnp.pad(w, widths)


def _const_spec(shape):
    return pl.BlockSpec(shape, lambda *_: (0,) * len(shape))


def _params(*sem):
    return pltpu.CompilerParams(dimension_semantics=sem, vmem_limit_bytes=VMEM_LIMIT)


def _rwkv_pre_kernel(*refs, has_vres, seg):
    it = iter(refs)
    (x_ref, xp_ref, xn_ref, vec_ref, wr_ref, wk_ref, wv_ref,
     w1_ref, a1_ref, g1_ref, w2_ref, a2_ref, g2_ref) = [next(it) for _ in range(13)]
    if has_vres:
        v1_ref, v2_ref, vf_ref = next(it), next(it), next(it)
    r_o, na_o, v_o, g_o, bon_o = [next(it) for _ in range(5)]
    lw_o = [next(it), next(it)]
    k_o = [next(it), next(it)]
    b_o = [next(it), next(it)]
    if not has_vres:
        vf_o = next(it)

    i = pl.program_id(1)
    vec = vec_ref[...]
    row_of = lambda j: vec[j:j + 1]
    gain = row_of(0)
    x = x_ref[0]
    bm = x.shape[0]
    h = _rms(x, gain)
    hp = jnp.where(i == 0, 0.0, _rms(xp_ref[0][SUBLANES - 1:SUBLANES], gain))
    hn = jnp.where(i == pl.num_programs(1) - 1, 0.0, _rms(xn_ref[0][0:1], gain))
    row = lax.broadcasted_iota(jnp.int32, (bm, 1), 0)
    h_prev = jnp.where(row == 0, hp, pltpu.roll(h, 1, 0))
    h_next = jnp.where(row == bm - 1, hn, pltpu.roll(h, bm - 1, 0))
    xx = 0.5 * (h_prev + h_next) - h
    xr, xw, xk, xv, xa, xg = [(h + xx * row_of(1 + j)).astype(BF16) for j in range(6)]

    r = _dot(xr, wr_ref[...])
    k = _dot(xk, wk_ref[...])
    v = _dot(xv, wv_ref[...])
    if has_vres:
        vz = row_of(14) + _dot(_dot(xv, v1_ref[...]), v2_ref[...])
        v = v + (vf_ref[0] - v) * jax.nn.sigmoid(vz)
    else:
        vf_o[0] = v
    g = _dot(jax.nn.sigmoid(_dot(xg, g1_ref[...])), g2_ref[...])

    kkr = k * row_of(11)
    kk = kkr / jnp.maximum(jnp.sqrt(_seg_sum(kkr * kkr, seg)), 1e-12)
    tw = jnp.tanh(_dot(xw, w1_ref[...])).astype(BF16)
    al = _dot(xa, a1_ref[...]).astype(BF16)
    k_sum = jnp.zeros_like(k)
    for d in range(2):
        z = row_of(7 + d) + _dot(tw, w2_ref[d])
        lw_o[d][0] = -jnp.exp(-_softplus(-z) - 0.5)
        a = jax.nn.sigmoid(row_of(9 + d) + _dot(al, a2_ref[d]))
        k_d = k * (1.0 + (a - 1.0) * row_of(12))
        k_o[d][0] = k_d.astype(BF16)
        b_o[d][0] = (kk * a).astype(BF16)
        k_sum = k_sum + k_d
    bon_o[0] = (_seg_sum(r * k_sum * row_of(13), seg) * v).astype(BF16)
    r_o[0] = r.astype(BF16)
    na_o[0] = (-kk).astype(BF16)
    v_o[0] = v.astype(BF16)
    g_o[0] = g.astype(BF16)


def rwkv_pre(x, v_first, norm_g, mu, w_r, w_k, w_v, w0, w1, w2, a0, a1, a2, g1, g2, k_k, k_a, r_k,
             vres, bm=256):
    bsz, s, d = x.shape
    bm = min(bm, s)
    nt = s // bm
    has_vres = vres is not None
    zero = jnp.zeros((d,), F32)
    v0 = vres[0] if has_vres else zero
    vec = jnp.stack([norm_g, *mu, w0[0], w0[1], a0[0], a0[1], k_k, k_a, r_k.reshape(d), v0, zero])
    bf = lambda t: t.astype(BF16)
    cat2 = lambda t: jnp.concatenate([t[0], t[1]], axis=1)
    lo = w1.shape[2]
    second = lambda t: jnp.stack([jnp.pad(t[0], ((0, lo), (0, 0))), jnp.pad(t[1], ((lo, 0), (0, 0)))])
    ins = [x, x, x, vec, bf(w_r), bf(w_k), bf(w_v), bf(cat2(w1)), bf(cat2(a1)), bf(_pad_to(g1, 1, LANES)),
           bf(second(w2)), bf(second(a2)), bf(_pad_to(g2, 0, LANES))]
    tile = pl.BlockSpec((1, bm, d), lambda b, i: (b, i, 0))
    hb = bm // SUBLANES
    in_specs = [tile,
                pl.BlockSpec((1, SUBLANES, d), lambda b, i: (b, jnp.maximum(i * hb - 1, 0), 0)),
                pl.BlockSpec((1, SUBLANES, d), lambda b, i: (b, jnp.minimum((i + 1) * hb, s // SUBLANES - 1), 0))]
    in_specs += [_const_spec(t.shape) for t in ins[3:]]
    if has_vres:
        extra = [bf(_pad_to(vres[1], 1, LANES)), bf(_pad_to(vres[2], 0, LANES))]
        ins += extra + [v_first]
        in_specs += [_const_spec(t.shape) for t in extra] + [tile]
    n_bf, n_f32 = 5, 2
    out_shape = ([jax.ShapeDtypeStruct((bsz, s, d), BF16)] * n_bf + [jax.ShapeDtypeStruct((bsz, s, d), F32)] * n_f32
                 + [jax.ShapeDtypeStruct((bsz, s, d), BF16)] * 4)
    if not has_vres:
        out_shape.append(jax.ShapeDtypeStruct((bsz, s, d), F32))
    outs = pl.pallas_call(
        functools.partial(_rwkv_pre_kernel, has_vres=has_vres, seg=RWKV_HEAD),
        out_shape=tuple(out_shape),
        grid=(bsz, nt),
        in_specs=in_specs,
        out_specs=tuple([tile] * len(out_shape)),
        compiler_params=_params("parallel", "parallel"),
        name="rwkv_pre",
    )(*ins)
    r, na, v, g, bonus, lw0, lw1, k0, k1, b0, b1 = outs[:11]
    v_first = v_first if has_vres else outs[11]
    return r, na, v, g, bonus, (lw0, lw1), (k0, k1), (b0, b1), v_first


def _wkv_kernel(r_ref, lw_ref, k_ref, v_ref, a_ref, b_ref, y_ref, h_scr, *, reverse, nh, hd):
    c = pl.program_id(1)

    @pl.when(c == 0)
    def _():
        h_scr[...] = jnp.zeros_like(h_scr)

    nb, L, _ = lw_ref.shape
    row = lax.broadcasted_iota(jnp.int32, (L, L), 0)
    col = lax.broadcasted_iota(jnp.int32, (L, L), 1)
    incl = (col >= row) if reverse else (col <= row)
    tri = jnp.where(incl, 1.0, 0.0).astype(BF16)
    eye = jnp.where(row == col, 1.0, 0.0).astype(F32)
    last = 0 if reverse else L - 1

    def scaled(bi):
        lw = lw_ref[bi]
        lw_hi, lw_lo = _split_bf16(lw)
        cs = (jnp.dot(tri, lw_hi, preferred_element_type=F32)
              + jnp.dot(tri, lw_lo, preferred_element_type=F32))
        ctot = cs[last:last + 1, :]
        r, k, v, a, b = [t[bi].astype(F32) for t in (r_ref, k_ref, v_ref, a_ref, b_ref)]
        einv = jnp.exp(-cs)
        etot = jnp.exp(ctot - cs)
        return dict(rt=r * jnp.exp(cs), at=a * jnp.exp(cs - lw), bt=b * einv, kt=k * einv,
                    bb=b * etot, kb=k * etot, v=v, wtot=jnp.exp(ctot))

    sc_in = [scaled(bi) for bi in range(nb)]

    row2 = lax.broadcasted_iota(jnp.int32, (2 * L, 2 * L), 0)
    col2 = lax.broadcasted_iota(jnp.int32, (2 * L, 2 * L), 1)
    rt2, cs2 = row2 & (L - 1), col2 & (L - 1)
    before = (cs2 > rt2) if reverse else (cs2 < rt2)
    score_mask = before | ((row2 >= L) & (cs2 == rt2))
    right_half = lax.broadcasted_iota(jnp.int32, (L, 2 * L), 1) >= L
    diag_blk = (row ^ col) < 2
    level_masks = []
    size = 2
    while size < L:
        x = row ^ col
        level_masks.append((x >= size) & (x < 2 * size))
        size *= 2

    T = 2 * hd
    lane_lo = lax.broadcasted_iota(jnp.int32, (1, T), 1) < hd
    rowp = lax.broadcasted_iota(jnp.int32, (T, T), 0)
    colp = lax.broadcasted_iota(jnp.int32, (T, T), 1)
    diag_blocks = (rowp < hd) == (colp < hd)
    eye_t = jnp.where(rowp == colp, 1.0, 0.0).astype(F32)
    tiles = [(bi, p) for bi in range(nb) for p in range(nh // 2)]
    units = range(len(tiles))
    heads = [(u, q) for u in units for q in range(2)]
    tile = lambda name, u: sc_in[tiles[u][0]][name][:, tiles[u][1] * T:(tiles[u][1] + 1) * T]
    own = lambda q: lane_lo if q == 0 else jnp.logical_not(lane_lo)

    a2 = [jnp.concatenate([tile("at", u), tile("rt", u)], 0) for u in units]
    b2 = [jnp.concatenate([tile("bt", u), tile("kt", u)], 0).astype(BF16) for u in units]
    v_sw = [pltpu.roll(tile("v", u), hd, 1) for u in units]
    vv_sw = [jnp.concatenate([t_.astype(BF16)] * 2, 0) for t_ in v_sw]
    bbkb_t = [jnp.concatenate([tile("bb", u), tile("kb", u)], 0).T.astype(BF16) for u in units]

    sc = [jnp.where(score_mask, _dot_nt(jnp.where(own(q), a2[u], 0.0), b2[u]), 0.0) for u, q in heads]
    top = [s_[:L] for s_ in sc]
    bot = [s_[L:] for s_ in sc]
    n_ab = [t_[:, :L] for t_ in top]
    t = [eye + jnp.where(diag_blk, n_, 0.0) for n_ in n_ab]
    for lm in level_masks:
        w = [_dot(jnp.where(lm, n_, 0.0), t_) for n_, t_ in zip(n_ab, t)]
        t = [t_ + _dot(t_, w_) for t_, w_ in zip(t, w)]
    x = [_dot(jnp.where(right_half, top[i], 0.0), vv_sw[u]) for i, (u, q) in enumerate(heads)]
    tz = [_dot(t[i], jnp.where(own(q), tile("at", u), x[i])) for i, (u, q) in enumerate(heads)]
    rhs = [jnp.concatenate([tz[i].astype(BF16), jnp.where(own(q), 0.0, v_sw[u]).astype(BF16)], 0)
           for i, (u, q) in enumerate(heads)]
    qp = [_dot(jnp.concatenate([bot[i].astype(BF16), bbkb_t[u]], 0), rhs[i]) for i, (u, q) in enumerate(heads)]
    for u in units:
        bi, p = tiles[u]
        q0, q1 = qp[2 * u], qp[2 * u + 1]
        rh = tile("rt", u) + jnp.where(lane_lo, q0[:L], q1[:L])
        yh = pltpu.roll(jnp.where(lane_lo, q1[:L], q0[:L]), hd, 1)
        prow = jnp.concatenate([q0[L:L + hd], q1[L + hd:]], 0)
        m = jnp.where(diag_blocks, prow, 0.0) + eye_t * tile("wtot", u)
        g = pltpu.roll(jnp.where(diag_blocks, 0.0, prow), hd, 1)
        fin = _dot(jnp.concatenate([rh, m], 0), h_scr[bi, p])
        y_ref[bi, :, p * T:(p + 1) * T] = fin[:L] + yh
        h_scr[bi, p] = fin[L:] + g


def wkv7(r, lw, k, v, a, b, reverse, nb=2):
    bsz, s, d = r.shape
    hd = RWKV_HEAD
    nh = d // hd
    L = min(WKV_CHUNK, s)
    nb = min(nb, bsz)
    assert L == hd and 2 * hd == LANES and nh % 2 == 0 and s % L == 0 and bsz % nb == 0
    nc = s // L
    if reverse:
        idx = lambda bi, ci: (bi, nc - 1 - ci, 0)
    else:
        idx = lambda bi, ci: (bi, ci, 0)
    spec = pl.BlockSpec((nb, L, d), idx)
    return pl.pallas_call(
        functools.partial(_wkv_kernel, reverse=reverse, nh=nh, hd=hd),
        out_shape=jax.ShapeDtypeStruct((bsz, s, d), F32),
        grid=(bsz // nb, nc),
        in_specs=[spec] * 6,
        out_specs=spec,
        scratch_shapes=[pltpu.VMEM((nb, nh // 2, 2 * hd, 2 * hd), F32)],
        compiler_params=_params("parallel", "arbitrary"),
        name="wkv7_rev" if reverse else "wkv7_fwd",
    )(r, lw, k, v, a, b)


def _rwkv_post_kernel(yf_ref, yb_ref, bon_ref, g_ref, x_ref, vec_ref, wo_ref, o_ref, *, seg):
    y = yf_ref[...] + yb_ref[...]
    yc = y - _seg_sum(y, seg) * (1.0 / seg)
    var = _seg_sum(yc * yc, seg) * (1.0 / seg)
    vec = vec_ref[...]
    yn = yc * lax.rsqrt(var + GN_EPS) * vec[0:1] + vec[1:2] + bon_ref[...].astype(F32)
    o_ref[...] = x_ref[...] + _dot(yn * g_ref[...].astype(F32), wo_ref[...])


def rwkv_post(yf, yb, bonus, g, x, gn_g, gn_b, w_o, bm=512):
    m, d = x.shape
    bm = min(bm, m)
    vec = _pad_to(jnp.stack([gn_g, gn_b]), 0, SUBLANES)
    tile = pl.BlockSpec((bm, d), lambda i: (i, 0))
    return pl.pallas_call(
        functools.partial(_rwkv_post_kernel, seg=RWKV_HEAD),
        out_shape=jax.ShapeDtypeStruct((m, d), F32),
        grid=(m // bm,),
        in_specs=[tile] * 5 + [_const_spec(vec.shape), _const_spec(w_o.shape)],
        out_specs=tile,
        compiler_params=_params("parallel"),
        name="rwkv_post",
    )(yf, yb, bonus, g, x, vec, w_o.astype(BF16))


def _head_norm_rope(y2, ctab, stab, o_ref, nh):
    inv_dim = 1.0 / (QK_NOPE + QK_ROPE)
    for h in range(nh):
        sl = slice(h * HEAD_PAD, (h + 1) * HEAD_PAD)
        rh = slice((nh + h) * HEAD_PAD, (nh + h + 1) * HEAD_PAD)
        yh = y2[:, sl]
        ms = jnp.sum(yh * yh, axis=-1, keepdims=True) * inv_dim
        o_ref[0, :, sl] = ((yh * ctab + y2[:, rh] * stab) * lax.rsqrt(ms + NORM_EPS)).astype(o_ref.dtype)


def _mla_pre_kernel(x_ref, gx_ref, gc_ref, hg_ref, cf_ref, sf_ref, win_ref, wq_ref, wk_ref, wvt_ref,
                    q_ref, k_ref, vt_ref, *, nh):
    gc = gc_ref[...]
    h = _rms(x_ref[0], gx_ref[...])
    c = _dot(h, win_ref[...])
    c_q = _rms(c[:, :Q_LORA], gc[0:1])
    c_kv = _rms(c[:, Q_LORA:Q_LORA + KV_LORA], gc[1:2, :KV_LORA]).astype(BF16)
    cf, sf = cf_ref[...], sf_ref[...]
    hg = hg_ref[...]
    _head_norm_rope(_dot(c_q, wq_ref[...]), cf * hg[0:1], sf * hg[1:2], q_ref, nh)
    k_in = jnp.concatenate([c_kv, c[:, Q_LORA + KV_LORA:].astype(BF16)], axis=1)
    _head_norm_rope(_dot(k_in, wk_ref[...]), cf * hg[2:3], sf * hg[3:4], k_ref, nh)
    vt_ref[0] = _dot_nt(wvt_ref[...], c_kv).astype(vt_ref.dtype)


def _pad_heads(w, nh, width, take):
    k = w.shape[0]
    wh = w.reshape(k, nh, width)[:, :, take]
    wh = jnp.pad(wh, ((0, 0), (0, 0), (0, HEAD_PAD - wh.shape[-1])))
    return wh.reshape(k, nh * HEAD_PAD)


def _with_rotate_half(w_pad, nh):
    k = w_pad.shape[0]
    half = QK_ROPE // 2
    w3 = w_pad.reshape(k, nh, HEAD_PAD)
    x1 = w3[:, :, QK_NOPE:QK_NOPE + half]
    x2 = w3[:, :, QK_NOPE + half:QK_NOPE + QK_ROPE]
    z = lambda n: jnp.zeros((k, nh, n), w_pad.dtype)
    rh = jnp.concatenate([z(QK_NOPE), -x2, x1, z(HEAD_PAD - QK_NOPE - QK_ROPE)], axis=-1)
    return jnp.concatenate([w_pad, rh.reshape(k, nh * HEAD_PAD)], axis=1)


def _gain_rows(g):
    half = QK_ROPE // 2
    tail = jnp.zeros((HEAD_PAD - QK_NOPE - QK_ROPE,), F32)
    straight = jnp.concatenate([g, tail])
    swapped = jnp.concatenate([jnp.zeros((QK_NOPE,), F32), g[QK_NOPE + half:], g[QK_NOPE:QK_NOPE + half], tail])
    return [straight, swapped]


def mla_pre(x, tables, norm_g, w_in, q_norm_g, kv_norm_g, w_uq, w_ukv, q_head_g, k_head_g, bm=256):
    bsz, s, d = x.shape
    bm = min(bm, s)
    qk = QK_NOPE + QK_ROPE
    nh = w_uq.shape[1] // qk
    assert Q_LORA % LANES == 0 and KV_LORA % LANES == 0 and KV_LORA <= Q_LORA
    gx = norm_g.reshape(1, d)
    gc = _pad_to(jnp.stack([q_norm_g, jnp.pad(kv_norm_g, (0, Q_LORA - KV_LORA))]), 0, SUBLANES)
    hg = _pad_to(jnp.stack(_gain_rows(q_head_g * (qk ** -0.5 * LOG2_E)) + _gain_rows(k_head_g)), 0, SUBLANES)
    win = _pad_to(w_in, 1, LANES)
    rope_w = win.shape[1] - Q_LORA - KV_LORA
    wq = _with_rotate_half(_pad_heads(w_uq, nh, qk, slice(0, qk)), nh)
    wk = _pad_heads(w_ukv, nh, QK_NOPE + V_HEAD, slice(0, QK_NOPE))
    place = jnp.zeros((rope_w, nh, HEAD_PAD), F32)
    place = place.at[jnp.arange(QK_ROPE), :, QK_NOPE + jnp.arange(QK_ROPE)].set(1.0)
    wk = _with_rotate_half(jnp.concatenate([wk, place.reshape(rope_w, nh * HEAD_PAD)], 0), nh)
    wvt = w_ukv.reshape(KV_LORA, nh, QK_NOPE + V_HEAD)[:, :, QK_NOPE:].reshape(KV_LORA, nh * V_HEAD).T
    weights = [t.astype(BF16) for t in (win, wq, wk, wvt)]
    nt = s // bm
    tab = pl.BlockSpec((bm, HEAD_PAD), lambda b, i: (b * nt + i, 0))
    row = lambda n: pl.BlockSpec((1, bm, n), lambda b, i: (b, i, 0))
    return pl.pallas_call(
        functools.partial(_mla_pre_kernel, nh=nh),
        out_shape=(jax.ShapeDtypeStruct((bsz, s, nh * HEAD_PAD), BF16),
                   jax.ShapeDtypeStruct((bsz, s, nh * HEAD_PAD), BF16),
                   jax.ShapeDtypeStruct((bsz, nh * V_HEAD, s), BF16)),
        grid=(bsz, nt),
        in_specs=[row(d), _const_spec(gx.shape), _const_spec(gc.shape), _const_spec(hg.shape), tab, tab]
                 + [_const_spec(t.shape) for t in weights],
        out_specs=(row(nh * HEAD_PAD), row(nh * HEAD_PAD),
                   pl.BlockSpec((1, nh * V_HEAD, bm), lambda b, i: (b, 0, i))),
        compiler_params=_params("parallel", "parallel"),
        name="mla_pre",
    )(x, gx, gc, hg, *tables, *weights)


def _attn_kernel(q_ref, k_ref, vt_ref, o_ref, *, kc):
    s = k_ref.shape[1]
    heads = range(2)
    qs = [q_ref[0, :, j * HEAD_PAD:(j + 1) * HEAD_PAD] for j in heads]

    def scores(c):
        return [_dot_nt(k_ref[0, c * kc:(c + 1) * kc, j * HEAD_PAD:(j + 1) * HEAD_PAD], qs[j]) for j in heads]

    ones = jnp.ones((ONES_ROWS, kc), BF16)
    nxt = scores(0)
    m, acc = [None] * 2, [None] * 2
    for c in range(s // kc):
        cur = nxt
        if (c + 1) * kc < s:
            nxt = scores(c + 1)
        for j in heads:
            vt = jnp.concatenate([vt_ref[0, j * V_HEAD:(j + 1) * V_HEAD, c * kc:(c + 1) * kc], ones], axis=0)
            mc = jnp.max(cur[j], axis=0, keepdims=True)
            if c == 0:
                m[j] = mc
                acc[j] = jnp.dot(vt, jnp.exp2((cur[j] - mc).astype(BF16)), preferred_element_type=F32)
            else:
                m_new = jnp.maximum(m[j], mc)
                pt = jnp.exp2((cur[j] - m_new).astype(BF16))
                acc[j] = jnp.exp2(m[j] - m_new) * acc[j] + jnp.dot(vt, pt, preferred_element_type=F32)
                m[j] = m_new
    out = jnp.concatenate([acc[j][:V_HEAD] / acc[j][V_HEAD:V_HEAD + 1] for j in heads], axis=0)
    o_ref[0] = out.T.astype(o_ref.dtype)


def attention(q, k, vt, tq=512, kc=256):
    bsz, s, _ = q.shape
    nh = vt.shape[1] // V_HEAD
    tq = min(tq, s)
    kc = min(kc, s)
    return pl.pallas_call(
        functools.partial(_attn_kernel, kc=kc),
        out_shape=jax.ShapeDtypeStruct((bsz, s, nh * V_HEAD), BF16),
        grid=(bsz, nh // 2, s // tq),
        in_specs=[pl.BlockSpec((1, tq, 2 * HEAD_PAD), lambda b, h, i: (b, i, h)),
                  pl.BlockSpec((1, s, 2 * HEAD_PAD), lambda b, h, i: (b, 0, h)),
                  pl.BlockSpec((1, 2 * V_HEAD, s), lambda b, h, i: (b, h, 0))],
        out_specs=pl.BlockSpec((1, tq, 2 * V_HEAD), lambda b, h, i: (b, i, h)),
        compiler_params=_params("parallel", "parallel", "parallel"),
        name="mla_attention",
    )(q, k, vt)


def _proj_res_kernel(a_ref, w_ref, x_ref, o_ref):
    o_ref[...] = x_ref[...] + _dot(a_ref[...], w_ref[...])


def proj_residual(a, w, x, bm=512):
    m, k = a.shape
    n = w.shape[1]
    bm = min(bm, m)
    return pl.pallas_call(
        _proj_res_kernel,
        out_shape=jax.ShapeDtypeStruct((m, n), F32),
        grid=(m // bm,),
        in_specs=[pl.BlockSpec((bm, k), lambda i: (i, 0)), _const_spec(w.shape),
                  pl.BlockSpec((bm, n), lambda i: (i, 0))],
        out_specs=pl.BlockSpec((bm, n), lambda i: (i, 0)),
        compiler_params=_params("parallel"),
        name="proj_residual",
    )(a, w.astype(BF16), x)


def _route_kernel(x_ref, g_ref, wr_ref, pos_ref, gate_ref, hn_ref, *, cap):
    h = _rms(x_ref[0], g_ref[...])
    hn_ref[0] = h.astype(hn_ref.dtype)
    wr = wr_ref[...]
    s = h.shape[0]
    ne = wr.shape[0]
    h_hi, h_lo = _split_bf16(h)
    w_hi, w_lo = _split_bf16(wr)
    logits = (lax.dot_general(w_hi, h_hi, NT_DIMS, preferred_element_type=F32)
              + lax.dot_general(w_hi, h_lo, NT_DIMS, preferred_element_type=F32)
              + lax.dot_general(w_lo, h_hi, NT_DIMS, preferred_element_type=F32))
    ex = jnp.exp(logits - jnp.max(logits, axis=0, keepdims=True))
    aff = ex / jnp.sum(ex, axis=0, keepdims=True)
    gate_ref[0] = aff

    bits = pltpu.bitcast(aff, jnp.int32)

    def count(mask):
        return jnp.sum(jnp.where(mask, 1.0, 0.0), axis=1, keepdims=True)

    def thr_step(i, prefix):
        cand = prefix | lax.shift_left(jnp.int32(1), 30 - i)
        return jnp.where(count(bits >= cand) >= cap, cand, prefix)

    thr = lax.fori_loop(0, 31, thr_step, jnp.zeros((ne, 1), jnp.int32))
    gt = bits > thr
    eq = bits == thr
    need = cap - count(gt)
    idx = lax.broadcasted_iota(jnp.int32, (ne, s), 1)

    def cut_step(i, p):
        cand = p + lax.shift_left(jnp.int32(1), (s.bit_length() - 1) - i)
        return jnp.where(count(eq & (idx < cand)) <= need, cand, p)

    cut = lax.fori_loop(0, s.bit_length(), cut_step, jnp.zeros((ne, 1), jnp.int32))
    sel = gt | (eq & (idx < cut))

    blk = LANES if s % LANES == 0 else s
    ri = lax.broadcasted_iota(jnp.int32, (blk, blk), 0)
    ci = lax.broadcasted_iota(jnp.int32, (blk, blk), 1)
    upper = jnp.where(ri <= ci, 1.0, 0.0).astype(BF16)
    off = jnp.zeros((ne, 1), F32)
    for j in range(s // blk):
        sl = slice(j * blk, (j + 1) * blk)
        sel_j = sel[:, sl]
        inc = jnp.dot(jnp.where(sel_j, 1.0, 0.0).astype(BF16), upper, preferred_element_type=F32)
        pos_ref[0, :, sl] = jnp.where(sel_j, (off + inc - 1.0).astype(jnp.int32), -1)
        off = off + inc[:, blk - 1:blk]


def route(x, norm_g, w_router, cap):
    bsz, s, d = x.shape
    ne = w_router.shape[1]
    out = pl.BlockSpec((1, ne, s), lambda b: (b, 0, 0))
    tok = pl.BlockSpec((1, s, d), lambda b: (b, 0, 0))
    return pl.pallas_call(
        functools.partial(_route_kernel, cap=cap),
        out_shape=(jax.ShapeDtypeStruct((bsz, ne, s), jnp.int32),
                   jax.ShapeDtypeStruct((bsz, ne, s), F32),
                   jax.ShapeDtypeStruct((bsz, s, d), BF16)),
        grid=(bsz,),
        in_specs=[tok, _const_spec((1, d)), _const_spec((ne, d))],
        out_specs=(out, out, tok),
        compiler_params=_params("parallel"),
        name="ec_route",
    )(x, norm_g.reshape(1, d), w_router.T)


def _selection(pos_row, cap):
    return pos_row == lax.broadcasted_iota(jnp.int32, (cap, pos_row.shape[1]), 0)


def _expert_ffn_kernel(h_ref, pos_ref, gate_ref, wg_ref, wu_ref, wd_ref, ye_ref, wg_s, wu_s, wd_s, *, cap):
    @pl.when(pl.program_id(1) == 0)
    def _():
        wg_s[...] = wg_ref[0, 0].astype(BF16)
        wu_s[...] = wu_ref[0, 0].astype(BF16)
        wd_s[...] = wd_ref[0, 0].astype(BF16)

    onehot = _selection(pos_ref[0, 0], cap)
    sel = jnp.where(onehot, 1.0, 0.0).astype(BF16)
    gcol = jnp.sum(jnp.where(onehot, gate_ref[0, 0], 0.0), axis=1, keepdims=True)
    xe = jnp.dot(sel, h_ref[0], preferred_element_type=F32).astype(BF16)
    hg = jnp.dot(xe, wg_s[...], preferred_element_type=F32)
    hu = jnp.dot(xe, wu_s[...], preferred_element_type=F32)
    hid = (hg * jax.nn.sigmoid(hg) * hu).astype(BF16)
    ye_ref[0, 0] = (jnp.dot(hid, wd_s[...], preferred_element_type=F32) * gcol).astype(ye_ref.dtype)


def _expert_scatter_kernel(x_ref, pos_ref, ye_ref, o_ref, *, cap, eg):
    @pl.when(pl.program_id(1) == 0)
    def _():
        o_ref[...] = x_ref[...]

    sel = jnp.concatenate(
        [jnp.where(_selection(pos_ref[0, g], cap), 1.0, 0.0).astype(BF16) for g in range(eg)], axis=0)
    ye = ye_ref[0].reshape(eg * cap, ye_ref.shape[-1])
    o_ref[0] += lax.dot_general(sel, ye, TN_DIMS, preferred_element_type=F32)


def expert_choice_ffn(x, norm_g, w_router, w_gate, w_up, w_down, layer, eg=4):
    bsz, s, d = x.shape
    _, ne, _, ff = w_gate.shape
    cap = EC_CAPACITY * s // ne
    eg = min(eg, ne)
    pos, gate, hn = route(x, norm_g, w_router, cap)
    pos4 = pos.reshape(bsz, ne, 1, s)
    gate4 = gate.reshape(bsz, ne, 1, s)
    row = pl.BlockSpec((1, 1, 1, s), lambda e, b: (b, e, 0, 0))
    wspec = lambda shape: pl.BlockSpec((1, 1) + shape, lambda e, b: (layer, e, 0, 0))
    ye = pl.pallas_call(
        functools.partial(_expert_ffn_kernel, cap=cap),
        out_shape=jax.ShapeDtypeStruct((bsz, ne, cap, d), BF16),
        grid=(ne, bsz),
        in_specs=[pl.BlockSpec((1, s, d), lambda e, b: (b, 0, 0)), row, row,
                  wspec((d, ff)), wspec((d, ff)), wspec((ff, d))],
        out_specs=pl.BlockSpec((1, 1, cap, d), lambda e, b: (b, e, 0, 0)),
        scratch_shapes=[pltpu.VMEM((d, ff), BF16), pltpu.VMEM((d, ff), BF16), pltpu.VMEM((ff, d), BF16)],
        compiler_params=_params("arbitrary", "arbitrary"),
        name="ec_expert_ffn",
    )(hn, pos4, gate4, w_gate, w_up, w_down)
    tok = pl.BlockSpec((1, s, d), lambda b, j: (b, 0, 0))
    return pl.pallas_call(
        functools.partial(_expert_scatter_kernel, cap=cap, eg=eg),
        out_shape=jax.ShapeDtypeStruct((bsz, s, d), F32),
        grid=(bsz, ne // eg),
        in_specs=[tok, pl.BlockSpec((1, eg, 1, s), lambda b, j: (b, j, 0, 0)),
                  pl.BlockSpec((1, eg, cap, d), lambda b, j: (b, j, 0, 0))],
        out_specs=tok,
        compiler_params=_params("parallel", "arbitrary"),
        name="ec_expert_scatter",
    )(x, pos4, ye)


def rwkv7_layer(x, v_first, norm_g, mu, w_r, w_k, w_v, w_o, w0, w1, w2, a0, a1, a2,
                g1, g2, k_k, k_a, r_k, gn_g, gn_b, vres):
    bsz, s, d = x.shape
    r, na, v, g, bonus, lw, k_d, b_d, v_first = rwkv_pre(
        x, v_first, norm_g, mu, w_r, w_k, w_v, w0, w1, w2, a0, a1, a2, g1, g2, k_k, k_a, r_k, vres)
    yf = wkv7(r, lw[0], k_d[0], v, na, b_d[0], reverse=False)
    yb = wkv7(r, lw[1], k_d[1], v, na, b_d[1], reverse=True)
    flat = lambda t: t.reshape(bsz * s, d)
    out = rwkv_post(flat(yf), flat(yb), flat(bonus), flat(g), flat(x), gn_g, gn_b, w_o)
    return out.reshape(bsz, s, d), v_first


def rope_tables(positions):
    inv_freq = ROPE_THETA ** (-jnp.arange(0, QK_ROPE, 2, dtype=F32) / QK_ROPE)
    ang = positions.astype(F32)[..., None] * inv_freq
    cos, sin = jnp.cos(ang), jnp.sin(ang)
    shape = cos.shape[:-1]
    z = lambda n: jnp.zeros(shape + (n,), F32)
    tail = HEAD_PAD - QK_NOPE - QK_ROPE
    cf = jnp.concatenate([jnp.ones(shape + (QK_NOPE,), F32), cos, cos, z(tail)], -1)
    sf = jnp.concatenate([z(QK_NOPE), sin, sin, z(tail)], -1)
    return cf.reshape(-1, HEAD_PAD), sf.reshape(-1, HEAD_PAD)


def mla_layer(x, tables, norm_g, w_in, q_norm_g, kv_norm_g, w_uq, w_ukv, q_head_g, k_head_g, w_o):
    bsz, s, d = x.shape
    q, k, vt = mla_pre(x, tables, norm_g, w_in, q_norm_g, kv_norm_g, w_uq, w_ukv, q_head_g, k_head_g)
    o = attention(q, k, vt)
    return proj_residual(o.reshape(bsz * s, -1), w_o, x.reshape(bsz * s, d)).reshape(bsz, s, d)


def kernel(x, positions, norm_mix_g, norm_ffn_g,
           rw_mu, rw_wr, rw_wk, rw_wv, rw_wo, rw_w0, rw_w1, rw_w2,
           rw_a0, rw_a1, rw_a2, rw_g1, rw_g2, rw_kk, rw_ka, rw_rk,
           rw_gn_g, rw_gn_b, rw_v0, rw_v1, rw_v2,
           mla_w_in, mla_q_norm_g, mla_kv_norm_g, mla_w_uq, mla_w_ukv,
           mla_q_head_g, mla_k_head_g, mla_w_o,
           moe_router, moe_w_gate, moe_w_up, moe_w_down):
    depth = norm_mix_g.shape[0]
    n_mixers = 2
    tables = rope_tables(positions)
    v_first = None
    for i in range(depth):
        j = i // n_mixers
        if i % n_mixers == 0:
            vres = None if j == 0 else (rw_v0[j - 1], rw_v1[j - 1], rw_v2[j - 1])
            x, v_first = rwkv7_layer(
                x, v_first, norm_mix_g[i], rw_mu[j], rw_wr[j], rw_wk[j], rw_wv[j], rw_wo[j],
                rw_w0[j], rw_w1[j], rw_w2[j], rw_a0[j], rw_a1[j], rw_a2[j],
                rw_g1[j], rw_g2[j], rw_kk[j], rw_ka[j], rw_rk[j],
                rw_gn_g[j], rw_gn_b[j], vres)
        else:
            x = mla_layer(x, tables, norm_mix_g[i], mla_w_in[j], mla_q_norm_g[j], mla_kv_norm_g[j],
                          mla_w_uq[j], mla_w_ukv[j], mla_q_head_g[j], mla_k_head_g[j], mla_w_o[j])
        x = expert_choice_ffn(x, norm_ffn_g[i], moe_router[i], moe_w_gate, moe_w_up, moe_w_down, layer=i)
    return x
```

```python
import functools

import jax
import jax.numpy as jnp
from jax import lax
from jax.experimental import pallas as pl
from jax.experimental.pallas import tpu as pltpu

F32 = jnp.float32
BF16 = jnp.bfloat16

NORM_EPS = 1e-6
GN_EPS = 64e-5
RWKV_HEAD = 64
QK_NOPE = 64
QK_ROPE = 32
V_HEAD = 64
Q_LORA = 384
KV_LORA = 256
ROPE_THETA = 10000.0
EC_CAPACITY = 2
LANES = 128
SUBLANES = 8
HEAD_PAD = LANES
ONES_ROWS = 16
LOG2_E = 1.4426950408889634
DECAY_SCALE = 0.6065306597126334
WKV_CHUNK = 64
VMEM_LIMIT = 56 * 1024 * 1024

NT_DIMS = (((1,), (1,)), ((), ()))
TN_DIMS = (((0,), (0,)), ((), ()))


def _dot(a, b):
    return jnp.dot(a.astype(BF16), b.astype(BF16), preferred_element_type=F32)


def _dot_nt(a, b):
    return lax.dot_general(a.astype(BF16), b.astype(BF16), NT_DIMS, preferred_element_type=F32)


def _dot_tn(a, b):
    return lax.dot_general(a.astype(BF16), b.astype(BF16), TN_DIMS, preferred_element_type=F32)


def _split_bf16(x):
    hi = x.astype(BF16)
    lo = (x - hi.astype(F32)).astype(BF16)
    return hi, lo


def _rms(x, gain):
    return x * lax.rsqrt(jnp.mean(x * x, axis=-1, keepdims=True) + NORM_EPS) * gain


def _seg_sum(t, seg):
    width = min(LANES, t.shape[1])
    ri = lax.broadcasted_iota(jnp.int32, (width, width), 0)
    ci = lax.broadcasted_iota(jnp.int32, (width, width), 1)
    ones = jnp.where((ri // seg) == (ci // seg), 1.0, 0.0).astype(BF16)
    parts = [_dot(t[:, j:j + width], ones) for j in range(0, t.shape[1], width)]
    return parts[0] if len(parts) == 1 else jnp.concatenate(parts, axis=1)


def _pad_to(w, axis, mult):
    pad = -w.shape[axis] % mult
    if pad == 0:
        return w
    widths = [(0, 0)] * w.ndim
    widths[axis] = (0, pad)
    return jnp.pad(w, widths)


def _const_spec(shape):
    return pl.BlockSpec(shape, lambda *_: (0,) * len(shape))


def _params(*sem):
    return pltpu.CompilerParams(dimension_semantics=sem, vmem_limit_bytes=VMEM_LIMIT)


def _rwkv_pre_kernel(*refs, has_vres, seg):
    it = iter(refs)
    (x_ref, xp_ref, xn_ref, vec_ref, wr_ref, wk_ref, wv_ref,
     w1_ref, a1_ref, g1_ref, w2_ref, a2_ref, g2_ref) = [next(it) for _ in range(13)]
    if has_vres:
        v1_ref, v2_ref, vf_ref = next(it), next(it), next(it)
    r_o, na_o, v_o, g_o, bon_o = [next(it) for _ in range(5)]
    lw_o = [next(it), next(it)]
    k_o = [next(it), next(it)]
    b_o = [next(it), next(it)]
    if not has_vres:
        vf_o = next(it)

    i = pl.program_id(1)
    vec = vec_ref[...]
    row_of = lambda j: vec[j:j + 1]
    gain = row_of(0)
    x = x_ref[0]
    bm = x.shape[0]
    h = _rms(x, gain)
    hp = jnp.where(i == 0, 0.0, _rms(xp_ref[0][SUBLANES - 1:SUBLANES], gain))
    hn = jnp.where(i == pl.num_programs(1) - 1, 0.0, _rms(xn_ref[0][0:1], gain))
    row = lax.broadcasted_iota(jnp.int32, (bm, 1), 0)
    h_prev = jnp.where(row == 0, hp, pltpu.roll(h, 1, 0))
    h_next = jnp.where(row == bm - 1, hn, pltpu.roll(h, bm - 1, 0))
    xx = 0.5 * (h_prev + h_next) - h
    xr, xw, xk, xv, xa, xg = [(h + xx * row_of(1 + j)).astype(BF16) for j in range(6)]

    r = _dot(xr, wr_ref[...])
    k = _dot(xk, wk_ref[...])
    v = _dot(xv, wv_ref[...])
    if has_vres:
        vz = row_of(14) + _dot(_dot(xv, v1_ref[...]), v2_ref[...])
        v = v + (vf_ref[0] - v) * jax.nn.sigmoid(vz)
    else:
        vf_o[0] = v
    g = _dot(jax.nn.sigmoid(_dot(xg, g1_ref[...])), g2_ref[...])

    kkr = k * row_of(11)
    kk = kkr / jnp.maximum(jnp.sqrt(_seg_sum(kkr * kkr, seg)), 1e-12)
    tw = jnp.tanh(_dot(xw, w1_ref[...])).astype(BF16)
    al = _dot(xa, a1_ref[...]).astype(BF16)
    k_keep = k * (1.0 - row_of(12))
    k_mix = k * row_of(12)
    k_ds = []
    for d in range(2):
        z = row_of(7 + d) + _dot(tw, w2_ref[d])
        lw_o[d][0] = -DECAY_SCALE * jax.nn.sigmoid(z)
        a = jax.nn.sigmoid(row_of(9 + d) + _dot(al, a2_ref[d]))
        k_d = k_keep + k_mix * a
        k_o[d][0] = k_d.astype(BF16)
        b_o[d][0] = (kk * a).astype(BF16)
        k_ds.append(k_d)
    bon_o[0] = (_seg_sum(r * (k_ds[0] + k_ds[1]) * row_of(13), seg) * v).astype(BF16)
    r_o[0] = r.astype(BF16)
    na_o[0] = (-kk).astype(BF16)
    v_o[0] = v.astype(BF16)
    g_o[0] = g.astype(BF16)


def rwkv_pre(x, v_first, norm_g, mu, w_r, w_k, w_v, w0, w1, w2, a0, a1, a2, g1, g2, k_k, k_a, r_k,
             vres, bm=256):
    bsz, s, d = x.shape
    bm = min(bm, s)
    nt = s // bm
    has_vres = vres is not None
    zero = jnp.zeros((d,), F32)
    v0 = vres[0] if has_vres else zero
    vec = jnp.stack([norm_g, *mu, w0[0], w0[1], a0[0], a0[1], k_k, k_a, r_k.reshape(d), v0, zero])
    bf = lambda t: t.astype(BF16)
    cat2 = lambda t: jnp.concatenate([t[0], t[1]], axis=1)
    lo = w1.shape[2]
    second = lambda t: jnp.stack([jnp.pad(t[0], ((0, lo), (0, 0))), jnp.pad(t[1], ((lo, 0), (0, 0)))])
    ins = [x, x, x, vec, bf(w_r), bf(w_k), bf(w_v), bf(cat2(w1)), bf(cat2(a1)), bf(_pad_to(g1, 1, LANES)),
           bf(second(w2)), bf(second(a2)), bf(_pad_to(g2, 0, LANES))]
    tile = pl.BlockSpec((1, bm, d), lambda b, i: (b, i, 0))
    hb = bm // SUBLANES
    in_specs = [tile,
                pl.BlockSpec((1, SUBLANES, d), lambda b, i: (b, jnp.maximum(i * hb - 1, 0), 0)),
                pl.BlockSpec((1, SUBLANES, d), lambda b, i: (b, jnp.minimum((i + 1) * hb, s // SUBLANES - 1), 0))]
    in_specs += [_const_spec(t.shape) for t in ins[3:]]
    if has_vres:
        extra = [bf(_pad_to(vres[1], 1, LANES)), bf(_pad_to(vres[2], 0, LANES))]
        ins += extra + [v_first]
        in_specs += [_const_spec(t.shape) for t in extra] + [tile]
    n_bf, n_f32 = 5, 2
    out_shape = ([jax.ShapeDtypeStruct((bsz, s, d), BF16)] * n_bf + [jax.ShapeDtypeStruct((bsz, s, d), F32)] * n_f32
                 + [jax.ShapeDtypeStruct((bsz, s, d), BF16)] * 4)
    if not has_vres:
        out_shape.append(jax.ShapeDtypeStruct((bsz, s, d), F32))
    outs = pl.pallas_call(
        functools.partial(_rwkv_pre_kernel, has_vres=has_vres, seg=RWKV_HEAD),
        out_shape=tuple(out_shape),
        grid=(bsz, nt),
        in_specs=in_specs,
        out_specs=tuple([tile] * len(out_shape)),
        compiler_params=_params("parallel", "parallel"),
        name="rwkv_pre",
    )(*ins)
    r, na, v, g, bonus, lw0, lw1, k0, k1, b0, b1 = outs[:11]
    v_first = v_first if has_vres else outs[11]
    return r, na, v, g, bonus, (lw0, lw1), (k0, k1), (b0, b1), v_first


def _wkv_kernel(r_ref, lw_ref, k_ref, v_ref, a_ref, b_ref, y_ref, h_scr, *, reverse, nh, hd):
    c = pl.program_id(1)

    @pl.when(c == 0)
    def _():
        h_scr[...] = jnp.zeros_like(h_scr)

    nb, L, _ = lw_ref.shape
    row = lax.broadcasted_iota(jnp.int32, (L, L), 0)
    col = lax.broadcasted_iota(jnp.int32, (L, L), 1)
    incl = (col >= row) if reverse else (col <= row)
    tri = jnp.where(incl, 1.0, 0.0).astype(BF16)
    eye = jnp.where(row == col, 1.0, 0.0).astype(F32)
    last = 0 if reverse else L - 1

    def scaled(bi):
        lw = lw_ref[bi]
        lw_hi, lw_lo = _split_bf16(lw)
        cs = (jnp.dot(tri, lw_hi, preferred_element_type=F32)
              + jnp.dot(tri, lw_lo, preferred_element_type=F32))
        ctot = cs[last:last + 1, :]
        r, k, v, a, b = [t[bi].astype(F32) for t in (r_ref, k_ref, v_ref, a_ref, b_ref)]
        einv = jnp.exp(-cs)
        etot = jnp.exp(ctot - cs)
        return dict(rt=r * jnp.exp(cs), at=a * jnp.exp(cs - lw), bt=b * einv, kt=k * einv,
                    bb=b * etot, kb=k * etot, v=v, wtot=jnp.exp(ctot))

    sc_in = [scaled(bi) for bi in range(nb)]

    row2 = lax.broadcasted_iota(jnp.int32, (2 * L, 2 * L), 0)
    col2 = lax.broadcasted_iota(jnp.int32, (2 * L, 2 * L), 1)
    rt2, cs2 = row2 & (L - 1), col2 & (L - 1)
    before = (cs2 > rt2) if reverse else (cs2 < rt2)
    score_mask = before | ((row2 >= L) & (cs2 == rt2))
    right_half = lax.broadcasted_iota(jnp.int32, (L, 2 * L), 1) >= L
    diag_blk = (row ^ col) < 2
    level_masks = []
    size = 2
    while size < L:
        x = row ^ col
        level_masks.append((x >= size) & (x < 2 * size))
        size *= 2

    T = 2 * hd
    lane_lo = lax.broadcasted_iota(jnp.int32, (1, T), 1) < hd
    rowp = lax.broadcasted_iota(jnp.int32, (T, T), 0)
    colp = lax.broadcasted_iota(jnp.int32, (T, T), 1)
    diag_blocks = (rowp < hd) == (colp < hd)
    eye_t = jnp.where(rowp == colp, 1.0, 0.0).astype(F32)
    tiles = [(bi, p) for bi in range(nb) for p in range(nh // 2)]
    units = range(len(tiles))
    heads = [(u, q) for u in units for q in range(2)]
    tile = lambda name, u: sc_in[tiles[u][0]][name][:, tiles[u][1] * T:(tiles[u][1] + 1) * T]
    own = lambda q: lane_lo if q == 0 else jnp.logical_not(lane_lo)

    a2 = [jnp.concatenate([tile("at", u), tile("rt", u)], 0) for u in units]
    b2 = [jnp.concatenate([tile("bt", u), tile("kt", u)], 0).astype(BF16) for u in units]
    v_sw = [pltpu.roll(tile("v", u), hd, 1) for u in units]
    vv_sw = [jnp.concatenate([t_.astype(BF16)] * 2, 0) for t_ in v_sw]
    bbkb_t = [jnp.concatenate([tile("bb", u), tile("kb", u)], 0).T.astype(BF16) for u in units]

    sc = [jnp.where(score_mask, _dot_nt(jnp.where(own(q), a2[u], 0.0), b2[u]), 0.0) for u, q in heads]
    top = [s_[:L] for s_ in sc]
    bot = [s_[L:] for s_ in sc]
    n_ab = [t_[:, :L] for t_ in top]
    t = [eye + jnp.where(diag_blk, n_, 0.0) for n_ in n_ab]
    for lm in level_masks:
        w = [_dot(jnp.where(lm, n_, 0.0), t_) for n_, t_ in zip(n_ab, t)]
        t = [t_ + _dot(t_, w_) for t_, w_ in zip(t, w)]
    x = [_dot(jnp.where(right_half, top[i], 0.0), vv_sw[u]) for i, (u, q) in enumerate(heads)]
    tz = [_dot(t[i], jnp.where(own(q), tile("at", u), x[i])) for i, (u, q) in enumerate(heads)]
    rhs = [jnp.concatenate([tz[i].astype(BF16), jnp.where(own(q), 0.0, v_sw[u]).astype(BF16)], 0)
           for i, (u, q) in enumerate(heads)]
    qp = [_dot(jnp.concatenate([bot[i].astype(BF16), bbkb_t[u][q * hd:(q + 1) * hd]], 0), rhs[i])
          for i, (u, q) in enumerate(heads)]
    for u in units:
        bi, p = tiles[u]
        q0, q1 = qp[2 * u], qp[2 * u + 1]
        rh = tile("rt", u) + jnp.where(lane_lo, q0[:L], q1[:L])
        yh = pltpu.roll(jnp.where(lane_lo, q1[:L], q0[:L]), hd, 1)
        prow = jnp.concatenate([q0[L:], q1[L:]], 0)
        m = jnp.where(diag_blocks, prow, 0.0) + eye_t * tile("wtot", u)
        g = pltpu.roll(jnp.where(diag_blocks, 0.0, prow), hd, 1)
        fin = _dot(jnp.concatenate([rh, m], 0), h_scr[bi, p])
        y_ref[bi, :, p * T:(p + 1) * T] = (fin[:L] + yh).astype(y_ref.dtype)
        h_scr[bi, p] = fin[L:] + g


def wkv7(r, lw, k, v, a, b, reverse, nb=2):
    bsz, s, d = r.shape
    hd = RWKV_HEAD
    nh = d // hd
    L = min(WKV_CHUNK, s)
    nb = min(nb, bsz)
    assert L == hd and 2 * hd == LANES and nh % 2 == 0 and s % L == 0 and bsz % nb == 0
    nc = s // L
    if reverse:
        idx = lambda bi, ci: (bi, nc - 1 - ci, 0)
    else:
        idx = lambda bi, ci: (bi, ci, 0)
    spec = pl.BlockSpec((nb, L, d), idx)
    return pl.pallas_call(
        functools.partial(_wkv_kernel, reverse=reverse, nh=nh, hd=hd),
        out_shape=jax.ShapeDtypeStruct((bsz, s, d), BF16),
        grid=(bsz // nb, nc),
        in_specs=[spec] * 6,
        out_specs=spec,
        scratch_shapes=[pltpu.VMEM((nb, nh // 2, 2 * hd, 2 * hd), F32)],
        compiler_params=_params("parallel", "arbitrary"),
        name="wkv7_rev" if reverse else "wkv7_fwd",
    )(r, lw, k, v, a, b)


def _rwkv_post_kernel(yf_ref, yb_ref, bon_ref, g_ref, x_ref, vec_ref, wo_ref, o_ref, *, seg):
    y = yf_ref[...].astype(F32) + yb_ref[...].astype(F32)
    yc = y - _seg_sum(y, seg) * (1.0 / seg)
    var = _seg_sum(yc * yc, seg) * (1.0 / seg)
    vec = vec_ref[...]
    yn = yc * lax.rsqrt(var + GN_EPS) * vec[0:1] + vec[1:2] + bon_ref[...].astype(F32)
    o_ref[...] = x_ref[...] + _dot(yn * g_ref[...].astype(F32), wo_ref[...])


def rwkv_post(yf, yb, bonus, g, x, gn_g, gn_b, w_o, bm=512):
    m, d = x.shape
    bm = min(bm, m)
    vec = _pad_to(jnp.stack([gn_g, gn_b]), 0, SUBLANES)
    tile = pl.BlockSpec((bm, d), lambda i: (i, 0))
    return pl.pallas_call(
        functools.partial(_rwkv_post_kernel, seg=RWKV_HEAD),
        out_shape=jax.ShapeDtypeStruct((m, d), F32),
        grid=(m // bm,),
        in_specs=[tile] * 5 + [_const_spec(vec.shape), _const_spec(w_o.shape)],
        out_specs=tile,
        compiler_params=_params("parallel"),
        name="rwkv_post",
    )(yf, yb, bonus, g, x, vec, w_o.astype(BF16))


def _head_norm_rope(y2, ctab, stab, o_ref, nh):
    inv_dim = 1.0 / (QK_NOPE + QK_ROPE)
    for h in range(nh):
        sl = slice(h * HEAD_PAD, (h + 1) * HEAD_PAD)
        rh = slice((nh + h) * HEAD_PAD, (nh + h + 1) * HEAD_PAD)
        yh = y2[:, sl]
        ms = jnp.sum(yh * yh, axis=-1, keepdims=True) * inv_dim
        o_ref[0, :, sl] = ((yh * ctab + y2[:, rh] * stab) * lax.rsqrt(ms + NORM_EPS)).astype(o_ref.dtype)


def _mla_pre_kernel(x_ref, gx_ref, gc_ref, hg_ref, cf_ref, sf_ref, win_ref, wq_ref, wk_ref, wvt_ref,
                    q_ref, k_ref, vt_ref, *, nh):
    gc = gc_ref[...]
    h = _rms(x_ref[0], gx_ref[...])
    c = _dot(h, win_ref[...])
    c_q = _rms(c[:, :Q_LORA], gc[0:1])
    c_kv = _rms(c[:, Q_LORA:Q_LORA + KV_LORA], gc[1:2, :KV_LORA]).astype(BF16)
    cf, sf = cf_ref[...], sf_ref[...]
    hg = hg_ref[...]
    _head_norm_rope(_dot(c_q, wq_ref[...]), cf * hg[0:1], sf * hg[1:2], q_ref, nh)
    k_in = jnp.concatenate([c_kv, c[:, Q_LORA + KV_LORA:].astype(BF16)], axis=1)
    _head_norm_rope(_dot(k_in, wk_ref[...]), cf * hg[2:3], sf * hg[3:4], k_ref, nh)
    vt_ref[0] = _dot_nt(wvt_ref[...], c_kv).astype(vt_ref.dtype)


def _pad_heads(w, nh, width, take):
    k = w.shape[0]
    wh = w.reshape(k, nh, width)[:, :, take]
    wh = jnp.pad(wh, ((0, 0), (0, 0), (0, HEAD_PAD - wh.shape[-1])))
    return wh.reshape(k, nh * HEAD_PAD)


def _with_rotate_half(w_pad, nh):
    k = w_pad.shape[0]
    half = QK_ROPE // 2
    w3 = w_pad.reshape(k, nh, HEAD_PAD)
    x1 = w3[:, :, QK_NOPE:QK_NOPE + half]
    x2 = w3[:, :, QK_NOPE + half:QK_NOPE + QK_ROPE]
    z = lambda n: jnp.zeros((k, nh, n), w_pad.dtype)
    rh = jnp.concatenate([z(QK_NOPE), -x2, x1, z(HEAD_PAD - QK_NOPE - QK_ROPE)], axis=-1)
    return jnp.concatenate([w_pad, rh.reshape(k, nh * HEAD_PAD)], axis=1)


def _gain_rows(g):
    half = QK_ROPE // 2
    tail = jnp.zeros((HEAD_PAD - QK_NOPE - QK_ROPE,), F32)
    straight = jnp.concatenate([g, tail])
    swapped = jnp.concatenate([jnp.zeros((QK_NOPE,), F32), g[QK_NOPE + half:], g[QK_NOPE:QK_NOPE + half], tail])
    return [straight, swapped]


def mla_pre(x, tables, norm_g, w_in, q_norm_g, kv_norm_g, w_uq, w_ukv, q_head_g, k_head_g, bm=256):
    bsz, s, d = x.shape
    bm = min(bm, s)
    qk = QK_NOPE + QK_ROPE
    nh = w_uq.shape[1] // qk
    assert Q_LORA % LANES == 0 and KV_LORA % LANES == 0 and KV_LORA <= Q_LORA
    gx = norm_g.reshape(1, d)
    gc = _pad_to(jnp.stack([q_norm_g, jnp.pad(kv_norm_g, (0, Q_LORA - KV_LORA))]), 0, SUBLANES)
    hg = _pad_to(jnp.stack(_gain_rows(q_head_g * (qk ** -0.5 * LOG2_E)) + _gain_rows(k_head_g)), 0, SUBLANES)
    win = _pad_to(w_in, 1, LANES)
    rope_w = win.shape[1] - Q_LORA - KV_LORA
    wq = _with_rotate_half(_pad_heads(w_uq, nh, qk, slice(0, qk)), nh)
    wk = _pad_heads(w_ukv, nh, QK_NOPE + V_HEAD, slice(0, QK_NOPE))
    place = jnp.zeros((rope_w, nh, HEAD_PAD), F32)
    place = place.at[jnp.arange(QK_ROPE), :, QK_NOPE + jnp.arange(QK_ROPE)].set(1.0)
    wk = _with_rotate_half(jnp.concatenate([wk, place.reshape(rope_w, nh * HEAD_PAD)], 0), nh)
    wvt = w_ukv.reshape(KV_LORA, nh, QK_NOPE + V_HEAD)[:, :, QK_NOPE:].reshape(KV_LORA, nh * V_HEAD).T
    weights = [t.astype(BF16) for t in (win, wq, wk, wvt)]
    nt = s // bm
    tab = pl.BlockSpec((bm, HEAD_PAD), lambda b, i: (b * nt + i, 0))
    row = lambda n: pl.BlockSpec((1, bm, n), lambda b, i: (b, i, 0))
    return pl.pallas_call(
        functools.partial(_mla_pre_kernel, nh=nh),
        out_shape=(jax.ShapeDtypeStruct((bsz, s, nh * HEAD_PAD), BF16),
                   jax.ShapeDtypeStruct((bsz, s, nh * HEAD_PAD), BF16),
                   jax.ShapeDtypeStruct((bsz, nh * V_HEAD, s), BF16)),
        grid=(bsz, nt),
        in_specs=[row(d), _const_spec(gx.shape), _const_spec(gc.shape), _const_spec(hg.shape), tab, tab]
                 + [_const_spec(t.shape) for t in weights],
        out_specs=(row(nh * HEAD_PAD), row(nh * HEAD_PAD),
                   pl.BlockSpec((1, nh * V_HEAD, bm), lambda b, i: (b, 0, i))),
        compiler_params=_params("parallel", "parallel"),
        name="mla_pre",
    )(x, gx, gc, hg, *tables, *weights)


def _attn_kernel(q_ref, k_ref, vt_ref, o_ref, *, kc):
    s = k_ref.shape[1]
    heads = range(2)
    qs = [q_ref[0, :, j * HEAD_PAD:(j + 1) * HEAD_PAD] for j in heads]

    def scores(c):
        return [_dot_nt(k_ref[0, c * kc:(c + 1) * kc, j * HEAD_PAD:(j + 1) * HEAD_PAD], qs[j]) for j in heads]

    ones = jnp.ones((ONES_ROWS, kc), BF16)
    nxt = scores(0)
    m, acc = [None] * 2, [None] * 2
    for c in range(s // kc):
        cur = nxt
        if (c + 1) * kc < s:
            nxt = scores(c + 1)
        for j in heads:
            vt = jnp.concatenate([vt_ref[0, j * V_HEAD:(j + 1) * V_HEAD, c * kc:(c + 1) * kc], ones], axis=0)
            mc = jnp.max(cur[j], axis=0, keepdims=True)
            if c == 0:
                m[j] = mc
                acc[j] = jnp.dot(vt, jnp.exp2((cur[j] - mc).astype(BF16)), preferred_element_type=F32)
            else:
                m_new = jnp.maximum(m[j], mc)
                pt = jnp.exp2((cur[j] - m_new).astype(BF16))
                acc[j] = jnp.exp2(m[j] - m_new) * acc[j] + jnp.dot(vt, pt, preferred_element_type=F32)
                m[j] = m_new
    out = jnp.concatenate([acc[j][:V_HEAD] / acc[j][V_HEAD:V_HEAD + 1] for j in heads], axis=0)
    o_ref[0] = out.T.astype(o_ref.dtype)


def attention(q, k, vt, tq=512, kc=256):
    bsz, s, _ = q.shape
    nh = vt.shape[1] // V_HEAD
    tq = min(tq, s)
    kc = min(kc, s)
    return pl.pallas_call(
        functools.partial(_attn_kernel, kc=kc),
        out_shape=jax.ShapeDtypeStruct((bsz, s, nh * V_HEAD), BF16),
        grid=(bsz, nh // 2, s // tq),
        in_specs=[pl.BlockSpec((1, tq, 2 * HEAD_PAD), lambda b, h, i: (b, i, h)),
                  pl.BlockSpec((1, s, 2 * HEAD_PAD), lambda b, h, i: (b, 0, h)),
                  pl.BlockSpec((1, 2 * V_HEAD, s), lambda b, h, i: (b, h, 0))],
        out_specs=pl.BlockSpec((1, tq, 2 * V_HEAD), lambda b, h, i: (b, i, h)),
        compiler_params=_params("parallel", "parallel", "parallel"),
        name="mla_attention",
    )(q, k, vt)


def _proj_res_kernel(a_ref, w_ref, x_ref, o_ref):
    o_ref[...] = x_ref[...] + _dot(a_ref[...], w_ref[...])


def proj_residual(a, w, x, bm=512):
    m, k = a.shape
    n = w.shape[1]
    bm = min(bm, m)
    return pl.pallas_call(
        _proj_res_kernel,
        out_shape=jax.ShapeDtypeStruct((m, n), F32),
        grid=(m // bm,),
        in_specs=[pl.BlockSpec((bm, k), lambda i: (i, 0)), _const_spec(w.shape),
                  pl.BlockSpec((bm, n), lambda i: (i, 0))],
        out_specs=pl.BlockSpec((bm, n), lambda i: (i, 0)),
        compiler_params=_params("parallel"),
        name="proj_residual",
    )(a, w.astype(BF16), x)


def _route_kernel(x_ref, g_ref, wr_ref, pos_ref, gate_ref, hn_ref, *, cap):
    h = _rms(x_ref[0], g_ref[...])
    hn_ref[0] = h.astype(hn_ref.dtype)
    wr = wr_ref[...]
    s = h.shape[0]
    ne = wr.shape[0]
    h_hi, h_lo = _split_bf16(h)
    w_hi, w_lo = _split_bf16(wr)
    logits = (lax.dot_general(w_hi, h_hi, NT_DIMS, preferred_element_type=F32)
              + lax.dot_general(w_hi, h_lo, NT_DIMS, preferred_element_type=F32)
              + lax.dot_general(w_lo, h_hi, NT_DIMS, preferred_element_type=F32))
    ex = jnp.exp(logits - jnp.max(logits, axis=0, keepdims=True))
    aff = ex / jnp.sum(ex, axis=0, keepdims=True)
    gate_ref[0] = aff

    bits = pltpu.bitcast(aff, jnp.int32)

    def count(mask):
        return jnp.sum(jnp.where(mask, 1.0, 0.0), axis=1, keepdims=True)

    def search(nbits, accept):
        value = jnp.zeros((ne, 1), jnp.int32)
        shift = nbits
        while shift > 0:
            width = 2 if shift % 2 == 0 else 1
            shift -= width
            digit = jnp.zeros((ne, 1), jnp.int32)
            for j in range(1, 1 << width):
                digit = digit + jnp.where(accept(value + (j << shift)), 1, 0)
            value = value + jnp.left_shift(digit, shift)
        return value

    thr = search(31, lambda cand: count(bits >= cand) >= cap)
    gt = bits > thr
    eq = bits == thr
    need = cap - count(gt)
    idx = lax.broadcasted_iota(jnp.int32, (ne, s), 1)
    cut = search(s.bit_length(), lambda cand: count(eq & (idx < cand)) <= need)
    sel = gt | (eq & (idx < cut))

    blk = LANES if s % LANES == 0 else s
    ri = lax.broadcasted_iota(jnp.int32, (blk, blk), 0)
    ci = lax.broadcasted_iota(jnp.int32, (blk, blk), 1)
    upper = jnp.where(ri <= ci, 1.0, 0.0).astype(BF16)
    off = jnp.zeros((ne, 1), F32)
    for j in range(s // blk):
        sl = slice(j * blk, (j + 1) * blk)
        sel_j = sel[:, sl]
        inc = jnp.dot(jnp.where(sel_j, 1.0, 0.0).astype(BF16), upper, preferred_element_type=F32)
        pos_ref[0, :, sl] = jnp.where(sel_j, (off + inc - 1.0).astype(jnp.int32), -1)
        off = off + inc[:, blk - 1:blk]


def route(x, norm_g, w_router, cap):
    bsz, s, d = x.shape
    ne = w_router.shape[1]
    out = pl.BlockSpec((1, ne, s), lambda b: (b, 0, 0))
    tok = pl.BlockSpec((1, s, d), lambda b: (b, 0, 0))
    return pl.pallas_call(
        functools.partial(_route_kernel, cap=cap),
        out_shape=(jax.ShapeDtypeStruct((bsz, ne, s), jnp.int32),
                   jax.ShapeDtypeStruct((bsz, ne, s), F32),
                   jax.ShapeDtypeStruct((bsz, s, d), BF16)),
        grid=(bsz,),
        in_specs=[tok, _const_spec((1, d)), _const_spec((ne, d))],
        out_specs=(out, out, tok),
        compiler_params=_params("parallel"),
        name="ec_route",
    )(x, norm_g.reshape(1, d), w_router.T)


def _selection(pos_row, cap):
    return pos_row == lax.broadcasted_iota(jnp.int32, (cap, pos_row.shape[1]), 0)


def _expert_ffn_kernel(h_ref, pos_ref, gate_ref, wg_ref, wu_ref, wd_ref, ye_ref, wg_s, wu_s, wd_s, *, cap):
    @pl.when(pl.program_id(1) == 0)
    def _():
        wg_s[...] = wg_ref[0, 0].astype(BF16)
        wu_s[...] = wu_ref[0, 0].astype(BF16)
        wd_s[...] = wd_ref[0, 0].astype(BF16)

    onehot = _selection(pos_ref[0, 0], cap)
    sel = jnp.where(onehot, 1.0, 0.0).astype(BF16)
    gcol = jnp.sum(jnp.where(onehot, gate_ref[0, 0], 0.0), axis=1, keepdims=True)
    xe = jnp.dot(sel, h_ref[0], preferred_element_type=F32).astype(BF16)
    hg = jnp.dot(xe, wg_s[...], preferred_element_type=F32)
    hu = jnp.dot(xe, wu_s[...], preferred_element_type=F32)
    hid = (hg * jax.nn.sigmoid(hg) * hu).astype(BF16)
    ye_ref[0, 0] = (jnp.dot(hid, wd_s[...], preferred_element_type=F32) * gcol).astype(ye_ref.dtype)


def _expert_scatter_kernel(x_ref, pos_ref, ye_ref, o_ref, *, cap, eg):
    @pl.when(pl.program_id(1) == 0)
    def _():
        o_ref[...] = x_ref[...]

    sel = jnp.concatenate(
        [jnp.where(_selection(pos_ref[0, g], cap), 1.0, 0.0).astype(BF16) for g in range(eg)], axis=0)
    ye = ye_ref[0].reshape(eg * cap, ye_ref.shape[-1])
    o_ref[0] += lax.dot_general(sel, ye, TN_DIMS, preferred_element_type=F32)


def expert_choice_ffn(x, norm_g, w_router, w_gate, w_up, w_down, layer, eg=4):
    bsz, s, d = x.shape
    _, ne, _, ff = w_gate.shape
    cap = EC_CAPACITY * s // ne
    eg = min(eg, ne)
    pos, gate, hn = route(x, norm_g, w_router, cap)
    pos4 = pos.reshape(bsz, ne, 1, s)
    gate4 = gate.reshape(bsz, ne, 1, s)
    row = pl.BlockSpec((1, 1, 1, s), lambda e, b: (b, e, 0, 0))
    wspec = lambda shape: pl.BlockSpec((1, 1) + shape, lambda e, b: (layer, e, 0, 0))
    ye = pl.pallas_call(
        functools.partial(_expert_ffn_kernel, cap=cap),
        out_shape=jax.ShapeDtypeStruct((bsz, ne, cap, d), BF16),
        grid=(ne, bsz),
        in_specs=[pl.BlockSpec((1, s, d), lambda e, b: (b, 0, 0)), row, row,
                  wspec((d, ff)), wspec((d, ff)), wspec((ff, d))],
        out_specs=pl.BlockSpec((1, 1, cap, d), lambda e, b: (b, e, 0, 0)),
        scratch_shapes=[pltpu.VMEM((d, ff), BF16), pltpu.VMEM((d, ff), BF16), pltpu.VMEM((ff, d), BF16)],
        compiler_params=_params("arbitrary", "arbitrary"),
        name="ec_expert_ffn",
    )(hn, pos4, gate4, w_gate, w_up, w_down)
    tok = pl.BlockSpec((1, s, d), lambda b, j: (b, 0, 0))
    return pl.pallas_call(
        functools.partial(_expert_scatter_kernel, cap=cap, eg=eg),
        out_shape=jax.ShapeDtypeStruct((bsz, s, d), F32),
        grid=(bsz, ne // eg),
        in_specs=[tok, pl.BlockSpec((1, eg, 1, s), lambda b, j: (b, j, 0, 0)),
                  pl.BlockSpec((1, eg, cap, d), lambda b, j: (b, j, 0, 0))],
        out_specs=tok,
        compiler_params=_params("parallel", "arbitrary"),
        name="ec_expert_scatter",
    )(x, pos4, ye)


def rwkv7_layer(x, v_first, norm_g, mu, w_r, w_k, w_v, w_o, w0, w1, w2, a0, a1, a2,
                g1, g2, k_k, k_a, r_k, gn_g, gn_b, vres):
    bsz, s, d = x.shape
    r, na, v, g, bonus, lw, k_d, b_d, v_first = rwkv_pre(
        x, v_first, norm_g, mu, w_r, w_k, w_v, w0, w1, w2, a0, a1, a2, g1, g2, k_k, k_a, r_k, vres)
    yf = wkv7(r, lw[0], k_d[0], v, na, b_d[0], reverse=False)
    yb = wkv7(r, lw[1], k_d[1], v, na, b_d[1], reverse=True)
    flat = lambda t: t.reshape(bsz * s, d)
    out = rwkv_post(flat(yf), flat(yb), flat(bonus), flat(g), flat(x), gn_g, gn_b, w_o)
    return out.reshape(bsz, s, d), v_first


def rope_tables(positions):
    inv_freq = ROPE_THETA ** (-jnp.arange(0, QK_ROPE, 2, dtype=F32) / QK_ROPE)
    ang = positions.astype(F32)[..., None] * inv_freq
    cos, sin = jnp.cos(ang), jnp.sin(ang)
    shape = cos.shape[:-1]
    z = lambda n: jnp.zeros(shape + (n,), F32)
    tail = HEAD_PAD - QK_NOPE - QK_ROPE
    cf = jnp.concatenate([jnp.ones(shape + (QK_NOPE,), F32), cos, cos, z(tail)], -1)
    sf = jnp.concatenate([z(QK_NOPE), sin, sin, z(tail)], -1)
    return cf.reshape(-1, HEAD_PAD), sf.reshape(-1, HEAD_PAD)


def mla_layer(x, tables, norm_g, w_in, q_norm_g, kv_norm_g, w_uq, w_ukv, q_head_g, k_head_g, w_o):
    bsz, s, d = x.shape
    q, k, vt = mla_pre(x, tables, norm_g, w_in, q_norm_g, kv_norm_g, w_uq, w_ukv, q_head_g, k_head_g)
    o = attention(q, k, vt)
    return proj_residual(o.reshape(bsz * s, -1), w_o, x.reshape(bsz * s, d)).reshape(bsz, s, d)


def kernel(x, positions, norm_mix_g, norm_ffn_g,
           rw_mu, rw_wr, rw_wk, rw_wv, rw_wo, rw_w0, rw_w1, rw_w2,
           rw_a0, rw_a1, rw_a2, rw_g1, rw_g2, rw_kk, rw_ka, rw_rk,
           rw_gn_g, rw_gn_b, rw_v0, rw_v1, rw_v2,
           mla_w_in, mla_q_norm_g, mla_kv_norm_g, mla_w_uq, mla_w_ukv,
           mla_q_head_g, mla_k_head_g, mla_w_o,
           moe_router, moe_w_gate, moe_w_up, moe_w_down):
    depth = norm_mix_g.shape[0]
    n_mixers = 2
    tables = rope_tables(positions)
    v_first = None
    for i in range(depth):
        j = i // n_mixers
        if i % n_mixers == 0:
            vres = None if j == 0 else (rw_v0[j - 1], rw_v1[j - 1], rw_v2[j - 1])
            x, v_first = rwkv7_layer(
                x, v_first, norm_mix_g[i], rw_mu[j], rw_wr[j], rw_wk[j], rw_wv[j], rw_wo[j],
                rw_w0[j], rw_w1[j], rw_w2[j], rw_a0[j], rw_a1[j], rw_a2[j],
                rw_g1[j], rw_g2[j], rw_kk[j], rw_ka[j], rw_rk[j],
                rw_gn_g[j], rw_gn_b[j], vres)
        else:
            x = mla_layer(x, tables, norm_mix_g[i], mla_w_in[j], mla_q_norm_g[j], mla_kv_norm_g[j],
                          mla_w_uq[j], mla_w_ukv[j], mla_q_head_g[j], mla_k_head_g[j], mla_w_o[j])
        x = expert_choice_ffn(x, norm_ffn_g[i], moe_router[i], moe_w_gate, moe_w_up, moe_w_down, layer=i)
    return x
```

```python
import functools

import jax
import jax.numpy as jnp
from jax import lax
from jax.experimental import pallas as pl
from jax.experimental.pallas import tpu as pltpu

F32 = jnp.float32
BF16 = jnp.bfloat16

NORM_EPS = 1e-6
GN_EPS = 64e-5
RWKV_HEAD = 64
QK_NOPE = 64
QK_ROPE = 32
V_HEAD = 64
Q_LORA = 384
KV_LORA = 256
ROPE_THETA = 10000.0
EC_CAPACITY = 2
LANES = 128
SUBLANES = 8
HEAD_PAD = LANES
ONES_ROWS = 16
LOG2_E = 1.4426950408889634
DECAY_SCALE = 0.6065306597126334
WKV_CHUNK = 64
VMEM_LIMIT = 56 * 1024 * 1024


class Tiles:
    rwkv_pre_rows = 256
    mla_pre_rows = 512
    row_tile = 1024
    attn_queries = 1024
    attn_key_chunk = 256
    wkv_batch_rows = 2
    scatter_experts = 4

NT_DIMS = (((1,), (1,)), ((), ()))
TN_DIMS = (((0,), (0,)), ((), ()))


def _dot(a, b):
    return jnp.dot(a.astype(BF16), b.astype(BF16), preferred_element_type=F32)


def _dot_nt(a, b):
    return lax.dot_general(a.astype(BF16), b.astype(BF16), NT_DIMS, preferred_element_type=F32)


def _dot_tn(a, b):
    return lax.dot_general(a.astype(BF16), b.astype(BF16), TN_DIMS, preferred_element_type=F32)


def _split_bf16(x):
    hi = x.astype(BF16)
    lo = (x - hi.astype(F32)).astype(BF16)
    return hi, lo


def _rms(x, gain):
    return x * lax.rsqrt(jnp.mean(x * x, axis=-1, keepdims=True) + NORM_EPS) * gain


def _seg_sum(t, seg):
    width = min(LANES, t.shape[1])
    ri = lax.broadcasted_iota(jnp.int32, (width, width), 0)
    ci = lax.broadcasted_iota(jnp.int32, (width, width), 1)
    ones = jnp.where((ri // seg) == (ci // seg), 1.0, 0.0).astype(BF16)
    parts = [_dot(t[:, j:j + width], ones) for j in range(0, t.shape[1], width)]
    return parts[0] if len(parts) == 1 else jnp.concatenate(parts, axis=1)


def _pad_to(w, axis, mult):
    pad = -w.shape[axis] % mult
    if pad == 0:
        return w
    widths = [(0, 0)] * w.ndim
    widths[axis] = (0, pad)
    return jnp.pad(w, widths)


def _const_spec(shape):
    return pl.BlockSpec(shape, lambda *_: (0,) * len(shape))


def _params(*sem):
    return pltpu.CompilerParams(dimension_semantics=sem, vmem_limit_bytes=VMEM_LIMIT)


def _rwkv_pre_kernel(*refs, has_vres, seg):
    it = iter(refs)
    (x_ref, xp_ref, xn_ref, vec_ref, wr_ref, wk_ref, wv_ref,
     w1_ref, a1_ref, g1_ref, w2_ref, a2_ref, g2_ref) = [next(it) for _ in range(13)]
    if has_vres:
        v1_ref, v2_ref, vf_ref = next(it), next(it), next(it)
    r_o, na_o, v_o, g_o, bon_o = [next(it) for _ in range(5)]
    lw_o = [next(it), next(it)]
    k_o = [next(it), next(it)]
    b_o = [next(it), next(it)]
    if not has_vres:
        vf_o = next(it)

    i = pl.program_id(1)
    vec = vec_ref[...]
    row_of = lambda j: vec[j:j + 1]
    gain = row_of(0)
    x = x_ref[0]
    bm = x.shape[0]
    h = _rms(x, gain)
    hp = jnp.where(i == 0, 0.0, _rms(xp_ref[0][SUBLANES - 1:SUBLANES], gain))
    hn = jnp.where(i == pl.num_programs(1) - 1, 0.0, _rms(xn_ref[0][0:1], gain))
    row = lax.broadcasted_iota(jnp.int32, (bm, 1), 0)
    h_prev = jnp.where(row == 0, hp, pltpu.roll(h, 1, 0))
    h_next = jnp.where(row == bm - 1, hn, pltpu.roll(h, bm - 1, 0))
    xx = 0.5 * (h_prev + h_next) - h
    xr, xw, xk, xv, xa, xg = [(h + xx * row_of(1 + j)).astype(BF16) for j in range(6)]

    r = _dot(xr, wr_ref[...])
    k = _dot(xk, wk_ref[...])
    v = _dot(xv, wv_ref[...])
    if has_vres:
        vz = row_of(14) + _dot(_dot(xv, v1_ref[...]), v2_ref[...])
        v = v + (vf_ref[0] - v) * jax.nn.sigmoid(vz)
    else:
        vf_o[0] = v
    g = _dot(jax.nn.sigmoid(_dot(xg, g1_ref[...])), g2_ref[...])

    kkr = k * row_of(11)
    kk = kkr * lax.rsqrt(jnp.maximum(_seg_sum(kkr * kkr, seg), 1e-24))
    tw = jnp.tanh(_dot(xw, w1_ref[...])).astype(BF16)
    al = _dot(xa, a1_ref[...]).astype(BF16)
    k_keep = k * (1.0 - row_of(12))
    k_mix = k * row_of(12)
    k_ds = []
    for d in range(2):
        z = row_of(7 + d) + _dot(tw, w2_ref[d])
        lw_o[d][0] = -DECAY_SCALE * jax.nn.sigmoid(z)
        a = jax.nn.sigmoid(row_of(9 + d) + _dot(al, a2_ref[d]))
        k_d = k_keep + k_mix * a
        k_o[d][0] = k_d.astype(BF16)
        b_o[d][0] = (kk * a).astype(BF16)
        k_ds.append(k_d)
    bon_o[0] = (_seg_sum(r * (k_ds[0] + k_ds[1]) * row_of(13), seg) * v).astype(BF16)
    r_o[0] = r.astype(BF16)
    na_o[0] = (-kk).astype(BF16)
    v_o[0] = v.astype(BF16)
    g_o[0] = g.astype(BF16)


def rwkv_pre(x, v_first, norm_g, mu, w_r, w_k, w_v, w0, w1, w2, a0, a1, a2, g1, g2, k_k, k_a, r_k,
             vres, bm=Tiles.rwkv_pre_rows):
    bsz, s, d = x.shape
    bm = min(bm, s)
    nt = s // bm
    has_vres = vres is not None
    zero = jnp.zeros((d,), F32)
    v0 = vres[0] if has_vres else zero
    vec = jnp.stack([norm_g, *mu, w0[0], w0[1], a0[0], a0[1], k_k, k_a, r_k.reshape(d), v0, zero])
    bf = lambda t: t.astype(BF16)
    cat2 = lambda t: jnp.concatenate([t[0], t[1]], axis=1)
    lo = w1.shape[2]
    second = lambda t: jnp.stack([jnp.pad(t[0], ((0, lo), (0, 0))), jnp.pad(t[1], ((lo, 0), (0, 0)))])
    ins = [x, x, x, vec, bf(w_r), bf(w_k), bf(w_v), bf(cat2(w1)), bf(cat2(a1)), bf(_pad_to(g1, 1, LANES)),
           bf(second(w2)), bf(second(a2)), bf(_pad_to(g2, 0, LANES))]
    tile = pl.BlockSpec((1, bm, d), lambda b, i: (b, i, 0))
    hb = bm // SUBLANES
    in_specs = [tile,
                pl.BlockSpec((1, SUBLANES, d), lambda b, i: (b, jnp.maximum(i * hb - 1, 0), 0)),
                pl.BlockSpec((1, SUBLANES, d), lambda b, i: (b, jnp.minimum((i + 1) * hb, s // SUBLANES - 1), 0))]
    in_specs += [_const_spec(t.shape) for t in ins[3:]]
    if has_vres:
        extra = [bf(_pad_to(vres[1], 1, LANES)), bf(_pad_to(vres[2], 0, LANES))]
        ins += extra + [v_first]
        in_specs += [_const_spec(t.shape) for t in extra] + [tile]
    n_bf, n_f32 = 5, 2
    out_shape = ([jax.ShapeDtypeStruct((bsz, s, d), BF16)] * n_bf + [jax.ShapeDtypeStruct((bsz, s, d), F32)] * n_f32
                 + [jax.ShapeDtypeStruct((bsz, s, d), BF16)] * 4)
    if not has_vres:
        out_shape.append(jax.ShapeDtypeStruct((bsz, s, d), F32))
    outs = pl.pallas_call(
        functools.partial(_rwkv_pre_kernel, has_vres=has_vres, seg=RWKV_HEAD),
        out_shape=tuple(out_shape),
        grid=(bsz, nt),
        in_specs=in_specs,
        out_specs=tuple([tile] * len(out_shape)),
        compiler_params=_params("parallel", "parallel"),
        name="rwkv_pre",
    )(*ins)
    r, na, v, g, bonus, lw0, lw1, k0, k1, b0, b1 = outs[:11]
    v_first = v_first if has_vres else outs[11]
    return r, na, v, g, bonus, (lw0, lw1), (k0, k1), (b0, b1), v_first


def _wkv_kernel(r_ref, lw_ref, k_ref, v_ref, a_ref, b_ref, y_ref, h_scr, *, reverse, nh, hd):
    c = pl.program_id(1)

    @pl.when(c == 0)
    def _():
        h_scr[...] = jnp.zeros_like(h_scr)

    nb, L, _ = lw_ref.shape
    row = lax.broadcasted_iota(jnp.int32, (L, L), 0)
    col = lax.broadcasted_iota(jnp.int32, (L, L), 1)
    incl = (col >= row) if reverse else (col <= row)
    tri = jnp.where(incl, 1.0, 0.0).astype(BF16)
    eye = jnp.where(row == col, 1.0, 0.0).astype(F32)
    last = 0 if reverse else L - 1

    def scaled(bi):
        lw = lw_ref[bi]
        lw_hi, lw_lo = _split_bf16(lw)
        cs = (jnp.dot(tri, lw_hi, preferred_element_type=F32)
              + jnp.dot(tri, lw_lo, preferred_element_type=F32))
        ctot = cs[last:last + 1, :]
        r, k, v, a, b = [t[bi].astype(F32) for t in (r_ref, k_ref, v_ref, a_ref, b_ref)]
        einv = jnp.exp(-cs)
        etot = jnp.exp(ctot - cs)
        return dict(rt=r * jnp.exp(cs), at=a * jnp.exp(cs - lw), bt=b * einv, kt=k * einv,
                    bb=b * etot, kb=k * etot, v=v, wtot=jnp.exp(ctot))

    sc_in = [scaled(bi) for bi in range(nb)]

    row2 = lax.broadcasted_iota(jnp.int32, (2 * L, 2 * L), 0)
    col2 = lax.broadcasted_iota(jnp.int32, (2 * L, 2 * L), 1)
    rt2, cs2 = row2 & (L - 1), col2 & (L - 1)
    before = (cs2 > rt2) if reverse else (cs2 < rt2)
    score_mask = before | ((row2 >= L) & (cs2 == rt2))
    right_half = lax.broadcasted_iota(jnp.int32, (L, 2 * L), 1) >= L
    diag_blk = (row ^ col) < 2
    level_masks = []
    size = 2
    while size < L:
        x = row ^ col
        level_masks.append((x >= size) & (x < 2 * size))
        size *= 2

    T = 2 * hd
    lane_lo = lax.broadcasted_iota(jnp.int32, (1, T), 1) < hd
    rowp = lax.broadcasted_iota(jnp.int32, (T, T), 0)
    colp = lax.broadcasted_iota(jnp.int32, (T, T), 1)
    diag_blocks = (rowp < hd) == (colp < hd)
    eye_t = jnp.where(rowp == colp, 1.0, 0.0).astype(F32)
    tiles = [(bi, p) for bi in range(nb) for p in range(nh // 2)]
    units = range(len(tiles))
    heads = [(u, q) for u in units for q in range(2)]
    tile = lambda name, u: sc_in[tiles[u][0]][name][:, tiles[u][1] * T:(tiles[u][1] + 1) * T]
    own = lambda q: lane_lo if q == 0 else jnp.logical_not(lane_lo)

    a2 = [jnp.concatenate([tile("at", u), tile("rt", u)], 0) for u in units]
    b2 = [jnp.concatenate([tile("bt", u), tile("kt", u)], 0).astype(BF16) for u in units]
    v_sw = [pltpu.roll(tile("v", u), hd, 1) for u in units]
    vv_sw = [jnp.concatenate([t_.astype(BF16)] * 2, 0) for t_ in v_sw]
    bbkb_t = [jnp.concatenate([tile("bb", u), tile("kb", u)], 0).T.astype(BF16) for u in units]

    sc = [jnp.where(score_mask, _dot_nt(jnp.where(own(q), a2[u], 0.0), b2[u]), 0.0) for u, q in heads]
    top = [s_[:L] for s_ in sc]
    bot = [s_[L:] for s_ in sc]
    n_ab = [t_[:, :L] for t_ in top]
    t = [eye + jnp.where(diag_blk, n_, 0.0) for n_ in n_ab]
    for lm in level_masks:
        w = [_dot(jnp.where(lm, n_, 0.0), t_) for n_, t_ in zip(n_ab, t)]
        t = [t_ + _dot(t_, w_) for t_, w_ in zip(t, w)]
    x = [_dot(jnp.where(right_half, top[i], 0.0), vv_sw[u]) for i, (u, q) in enumerate(heads)]
    tz = [_dot(t[i], jnp.where(own(q), tile("at", u), x[i])) for i, (u, q) in enumerate(heads)]
    rhs = [jnp.concatenate([tz[i].astype(BF16), jnp.where(own(q), 0.0, v_sw[u]).astype(BF16)], 0)
           for i, (u, q) in enumerate(heads)]
    qp = [_dot(jnp.concatenate([bot[i].astype(BF16), bbkb_t[u][q * hd:(q + 1) * hd]], 0), rhs[i])
          for i, (u, q) in enumerate(heads)]
    for u in units:
        bi, p = tiles[u]
        q0, q1 = qp[2 * u], qp[2 * u + 1]
        rh = tile("rt", u) + jnp.where(lane_lo, q0[:L], q1[:L])
        yh = pltpu.roll(jnp.where(lane_lo, q1[:L], q0[:L]), hd, 1)
        prow = jnp.concatenate([q0[L:], q1[L:]], 0)
        m = jnp.where(diag_blocks, prow, 0.0) + eye_t * tile("wtot", u)
        g = pltpu.roll(jnp.where(diag_blocks, 0.0, prow), hd, 1)
        fin = _dot(jnp.concatenate([rh, m], 0), h_scr[bi, p])
        y_ref[bi, :, p * T:(p + 1) * T] = (fin[:L] + yh).astype(y_ref.dtype)
        h_scr[bi, p] = fin[L:] + g


def wkv7(r, lw, k, v, a, b, reverse, nb=Tiles.wkv_batch_rows):
    bsz, s, d = r.shape
    hd = RWKV_HEAD
    nh = d // hd
    L = min(WKV_CHUNK, s)
    nb = min(nb, bsz)
    assert L == hd and 2 * hd == LANES and nh % 2 == 0 and s % L == 0 and bsz % nb == 0
    nc = s // L
    if reverse:
        idx = lambda bi, ci: (bi, nc - 1 - ci, 0)
    else:
        idx = lambda bi, ci: (bi, ci, 0)
    spec = pl.BlockSpec((nb, L, d), idx)
    return pl.pallas_call(
        functools.partial(_wkv_kernel, reverse=reverse, nh=nh, hd=hd),
        out_shape=jax.ShapeDtypeStruct((bsz, s, d), BF16),
        grid=(bsz // nb, nc),
        in_specs=[spec] * 6,
        out_specs=spec,
        scratch_shapes=[pltpu.VMEM((nb, nh // 2, 2 * hd, 2 * hd), F32)],
        compiler_params=_params("parallel", "arbitrary"),
        name="wkv7_rev" if reverse else "wkv7_fwd",
    )(r, lw, k, v, a, b)


def _rwkv_post_kernel(yf_ref, yb_ref, bon_ref, g_ref, x_ref, vec_ref, wo_ref, o_ref, *, seg):
    y = yf_ref[...].astype(F32) + yb_ref[...].astype(F32)
    yc = y - _seg_sum(y, seg) * (1.0 / seg)
    var = _seg_sum(yc * yc, seg) * (1.0 / seg)
    vec = vec_ref[...]
    yn = yc * lax.rsqrt(var + GN_EPS) * vec[0:1] + vec[1:2] + bon_ref[...].astype(F32)
    o_ref[...] = x_ref[...] + _dot(yn * g_ref[...].astype(F32), wo_ref[...])


def rwkv_post(yf, yb, bonus, g, x, gn_g, gn_b, w_o, bm=Tiles.row_tile):
    m, d = x.shape
    bm = min(bm, m)
    vec = _pad_to(jnp.stack([gn_g, gn_b]), 0, SUBLANES)
    tile = pl.BlockSpec((bm, d), lambda i: (i, 0))
    return pl.pallas_call(
        functools.partial(_rwkv_post_kernel, seg=RWKV_HEAD),
        out_shape=jax.ShapeDtypeStruct((m, d), F32),
        grid=(m // bm,),
        in_specs=[tile] * 5 + [_const_spec(vec.shape), _const_spec(w_o.shape)],
        out_specs=tile,
        compiler_params=_params("parallel"),
        name="rwkv_post",
    )(yf, yb, bonus, g, x, vec, w_o.astype(BF16))


def _head_norm_rope(y2, ctab, stab, o_ref, nh):
    inv_dim = 1.0 / (QK_NOPE + QK_ROPE)
    for h in range(nh):
        sl = slice(h * HEAD_PAD, (h + 1) * HEAD_PAD)
        rh = slice((nh + h) * HEAD_PAD, (nh + h + 1) * HEAD_PAD)
        yh = y2[:, sl]
        ms = jnp.sum(yh * yh, axis=-1, keepdims=True) * inv_dim
        o_ref[0, :, sl] = ((yh * ctab + y2[:, rh] * stab) * lax.rsqrt(ms + NORM_EPS)).astype(o_ref.dtype)


def _mla_pre_kernel(x_ref, gx_ref, gc_ref, hg_ref, cf_ref, sf_ref, win_ref, wq_ref, wk_ref, wvt_ref,
                    q_ref, k_ref, vt_ref, *, nh):
    gc = gc_ref[...]
    h = _rms(x_ref[0], gx_ref[...])
    c = _dot(h, win_ref[...])
    c_q = _rms(c[:, :Q_LORA], gc[0:1])
    c_kv = _rms(c[:, Q_LORA:Q_LORA + KV_LORA], gc[1:2, :KV_LORA]).astype(BF16)
    cf, sf = cf_ref[...], sf_ref[...]
    hg = hg_ref[...]
    _head_norm_rope(_dot(c_q, wq_ref[...]), cf * hg[0:1], sf * hg[1:2], q_ref, nh)
    k_in = jnp.concatenate([c_kv, c[:, Q_LORA + KV_LORA:].astype(BF16)], axis=1)
    _head_norm_rope(_dot(k_in, wk_ref[...]), cf * hg[2:3], sf * hg[3:4], k_ref, nh)
    vt_ref[0] = _dot_nt(wvt_ref[...], c_kv).astype(vt_ref.dtype)


def _pad_heads(w, nh, width, take):
    k = w.shape[0]
    wh = w.reshape(k, nh, width)[:, :, take]
    wh = jnp.pad(wh, ((0, 0), (0, 0), (0, HEAD_PAD - wh.shape[-1])))
    return wh.reshape(k, nh * HEAD_PAD)


def _with_rotate_half(w_pad, nh):
    k = w_pad.shape[0]
    half = QK_ROPE // 2
    w3 = w_pad.reshape(k, nh, HEAD_PAD)
    x1 = w3[:, :, QK_NOPE:QK_NOPE + half]
    x2 = w3[:, :, QK_NOPE + half:QK_NOPE + QK_ROPE]
    z = lambda n: jnp.zeros((k, nh, n), w_pad.dtype)
    rh = jnp.concatenate([z(QK_NOPE), -x2, x1, z(HEAD_PAD - QK_NOPE - QK_ROPE)], axis=-1)
    return jnp.concatenate([w_pad, rh.reshape(k, nh * HEAD_PAD)], axis=1)


def _gain_rows(g):
    half = QK_ROPE // 2
    tail = jnp.zeros((HEAD_PAD - QK_NOPE - QK_ROPE,), F32)
    straight = jnp.concatenate([g, tail])
    swapped = jnp.concatenate([jnp.zeros((QK_NOPE,), F32), g[QK_NOPE + half:], g[QK_NOPE:QK_NOPE + half], tail])
    return [straight, swapped]


def mla_pre(x, tables, norm_g, w_in, q_norm_g, kv_norm_g, w_uq, w_ukv, q_head_g, k_head_g,
            bm=Tiles.mla_pre_rows):
    bsz, s, d = x.shape
    bm = min(bm, s)
    qk = QK_NOPE + QK_ROPE
    nh = w_uq.shape[1] // qk
    assert Q_LORA % LANES == 0 and KV_LORA % LANES == 0 and KV_LORA <= Q_LORA
    gx = norm_g.reshape(1, d)
    gc = _pad_to(jnp.stack([q_norm_g, jnp.pad(kv_norm_g, (0, Q_LORA - KV_LORA))]), 0, SUBLANES)
    hg = _pad_to(jnp.stack(_gain_rows(q_head_g * (qk ** -0.5 * LOG2_E)) + _gain_rows(k_head_g)), 0, SUBLANES)
    win = _pad_to(w_in, 1, LANES)
    rope_w = win.shape[1] - Q_LORA - KV_LORA
    wq = _with_rotate_half(_pad_heads(w_uq, nh, qk, slice(0, qk)), nh)
    wk = _pad_heads(w_ukv, nh, QK_NOPE + V_HEAD, slice(0, QK_NOPE))
    place = jnp.zeros((rope_w, nh, HEAD_PAD), F32)
    place = place.at[jnp.arange(QK_ROPE), :, QK_NOPE + jnp.arange(QK_ROPE)].set(1.0)
    wk = _with_rotate_half(jnp.concatenate([wk, place.reshape(rope_w, nh * HEAD_PAD)], 0), nh)
    wvt = w_ukv.reshape(KV_LORA, nh, QK_NOPE + V_HEAD)[:, :, QK_NOPE:].reshape(KV_LORA, nh * V_HEAD).T
    weights = [t.astype(BF16) for t in (win, wq, wk, wvt)]
    nt = s // bm
    tab = pl.BlockSpec((bm, HEAD_PAD), lambda b, i: (b * nt + i, 0))
    row = lambda n: pl.BlockSpec((1, bm, n), lambda b, i: (b, i, 0))
    return pl.pallas_call(
        functools.partial(_mla_pre_kernel, nh=nh),
        out_shape=(jax.ShapeDtypeStruct((bsz, s, nh * HEAD_PAD), BF16),
                   jax.ShapeDtypeStruct((bsz, s, nh * HEAD_PAD), BF16),
                   jax.ShapeDtypeStruct((bsz, nh * V_HEAD, s), BF16)),
        grid=(bsz, nt),
        in_specs=[row(d), _const_spec(gx.shape), _const_spec(gc.shape), _const_spec(hg.shape), tab, tab]
                 + [_const_spec(t.shape) for t in weights],
        out_specs=(row(nh * HEAD_PAD), row(nh * HEAD_PAD),
                   pl.BlockSpec((1, nh * V_HEAD, bm), lambda b, i: (b, 0, i))),
        compiler_params=_params("parallel", "parallel"),
        name="mla_pre",
    )(x, gx, gc, hg, *tables, *weights)


def _attn_kernel(q_ref, k_ref, vt_ref, o_ref, *, kc):
    s = k_ref.shape[1]
    heads = range(2)
    qs = [q_ref[0, :, j * HEAD_PAD:(j + 1) * HEAD_PAD] for j in heads]

    def scores(c):
        return [_dot_nt(k_ref[0, c * kc:(c + 1) * kc, j * HEAD_PAD:(j + 1) * HEAD_PAD], qs[j]) for j in heads]

    ones = jnp.ones((ONES_ROWS, kc), BF16)
    nxt = scores(0)
    m, acc = [None] * 2, [None] * 2
    for c in range(s // kc):
        cur = nxt
        if (c + 1) * kc < s:
            nxt = scores(c + 1)
        for j in heads:
            vt = jnp.concatenate([vt_ref[0, j * V_HEAD:(j + 1) * V_HEAD, c * kc:(c + 1) * kc], ones], axis=0)
            mc = jnp.max(cur[j], axis=0, keepdims=True)
            if c == 0:
                m[j] = mc
                acc[j] = jnp.dot(vt, jnp.exp2((cur[j] - mc).astype(BF16)), preferred_element_type=F32)
            else:
                m_new = jnp.maximum(m[j], mc)
                pt = jnp.exp2((cur[j] - m_new).astype(BF16))
                acc[j] = jnp.exp2(m[j] - m_new) * acc[j] + jnp.dot(vt, pt, preferred_element_type=F32)
                m[j] = m_new
    out = jnp.concatenate([acc[j][:V_HEAD] / acc[j][V_HEAD:V_HEAD + 1] for j in heads], axis=0)
    o_ref[0] = out.T.astype(o_ref.dtype)


def attention(q, k, vt, tq=Tiles.attn_queries, kc=Tiles.attn_key_chunk):
    bsz, s, _ = q.shape
    nh = vt.shape[1] // V_HEAD
    tq = min(tq, s)
    kc = min(kc, s)
    return pl.pallas_call(
        functools.partial(_attn_kernel, kc=kc),
        out_shape=jax.ShapeDtypeStruct((bsz, s, nh * V_HEAD), BF16),
        grid=(bsz, nh // 2, s // tq),
        in_specs=[pl.BlockSpec((1, tq, 2 * HEAD_PAD), lambda b, h, i: (b, i, h)),
                  pl.BlockSpec((1, s, 2 * HEAD_PAD), lambda b, h, i: (b, 0, h)),
                  pl.BlockSpec((1, 2 * V_HEAD, s), lambda b, h, i: (b, h, 0))],
        out_specs=pl.BlockSpec((1, tq, 2 * V_HEAD), lambda b, h, i: (b, i, h)),
        compiler_params=_params("parallel", "parallel", "parallel"),
        name="mla_attention",
    )(q, k, vt)


def _proj_res_kernel(a_ref, w_ref, x_ref, o_ref):
    o_ref[...] = x_ref[...] + _dot(a_ref[...], w_ref[...])


def proj_residual(a, w, x, bm=Tiles.row_tile):
    m, k = a.shape
    n = w.shape[1]
    bm = min(bm, m)
    return pl.pallas_call(
        _proj_res_kernel,
        out_shape=jax.ShapeDtypeStruct((m, n), F32),
        grid=(m // bm,),
        in_specs=[pl.BlockSpec((bm, k), lambda i: (i, 0)), _const_spec(w.shape),
                  pl.BlockSpec((bm, n), lambda i: (i, 0))],
        out_specs=pl.BlockSpec((bm, n), lambda i: (i, 0)),
        compiler_params=_params("parallel"),
        name="proj_residual",
    )(a, w.astype(BF16), x)


def _route_kernel(x_ref, g_ref, wr_ref, pos_ref, gate_ref, hn_ref, *, cap):
    h = _rms(x_ref[0], g_ref[...])
    hn_ref[0] = h.astype(hn_ref.dtype)
    wr = wr_ref[...]
    s = h.shape[0]
    ne = wr.shape[0]
    h_hi, h_lo = _split_bf16(h)
    w_hi, w_lo = _split_bf16(wr)
    logits = (lax.dot_general(w_hi, h_hi, NT_DIMS, preferred_element_type=F32)
              + lax.dot_general(w_hi, h_lo, NT_DIMS, preferred_element_type=F32)
              + lax.dot_general(w_lo, h_hi, NT_DIMS, preferred_element_type=F32))
    ex = jnp.exp(logits - jnp.max(logits, axis=0, keepdims=True))
    aff = ex / jnp.sum(ex, axis=0, keepdims=True)
    gate_ref[0] = aff

    bits = pltpu.bitcast(aff, jnp.int32)

    def count(mask):
        return jnp.sum(jnp.where(mask, 1.0, 0.0), axis=1, keepdims=True)

    def search(nbits, accept):
        value = jnp.zeros((ne, 1), jnp.int32)
        shift = nbits
        while shift > 0:
            width = 2 if shift % 2 == 0 else 1
            shift -= width
            digit = jnp.zeros((ne, 1), jnp.int32)
            for j in range(1, 1 << width):
                digit = digit + jnp.where(accept(value + (j << shift)), 1, 0)
            value = value + jnp.left_shift(digit, shift)
        return value

    thr = search(31, lambda cand: count(bits >= cand) >= cap)
    gt = bits > thr
    eq = bits == thr
    need = cap - count(gt)
    idx = lax.broadcasted_iota(jnp.int32, (ne, s), 1)
    cut = search(s.bit_length(), lambda cand: count(eq & (idx < cand)) <= need)
    sel = gt | (eq & (idx < cut))

    blk = LANES if s % LANES == 0 else s
    ri = lax.broadcasted_iota(jnp.int32, (blk, blk), 0)
    ci = lax.broadcasted_iota(jnp.int32, (blk, blk), 1)
    upper = jnp.where(ri <= ci, 1.0, 0.0).astype(BF16)
    off = jnp.zeros((ne, 1), F32)
    for j in range(s // blk):
        sl = slice(j * blk, (j + 1) * blk)
        sel_j = sel[:, sl]
        inc = jnp.dot(jnp.where(sel_j, 1.0, 0.0).astype(BF16), upper, preferred_element_type=F32)
        pos_ref[0, :, sl] = jnp.where(sel_j, (off + inc - 1.0).astype(jnp.int32), -1)
        off = off + inc[:, blk - 1:blk]


def route(x, norm_g, w_router, cap):
    bsz, s, d = x.shape
    ne = w_router.shape[1]
    out = pl.BlockSpec((1, ne, s), lambda b: (b, 0, 0))
    tok = pl.BlockSpec((1, s, d), lambda b: (b, 0, 0))
    return pl.pallas_call(
        functools.partial(_route_kernel, cap=cap),
        out_shape=(jax.ShapeDtypeStruct((bsz, ne, s), jnp.int32),
                   jax.ShapeDtypeStruct((bsz, ne, s), F32),
                   jax.ShapeDtypeStruct((bsz, s, d), BF16)),
        grid=(bsz,),
        in_specs=[tok, _const_spec((1, d)), _const_spec((ne, d))],
        out_specs=(out, out, tok),
        compiler_params=_params("parallel"),
        name="ec_route",
    )(x, norm_g.reshape(1, d), w_router.T)


def _selection(pos_row, cap):
    return pos_row == lax.broadcasted_iota(jnp.int32, (cap, pos_row.shape[1]), 0)


def _expert_ffn_kernel(h_ref, pos_ref, gate_ref, wg_ref, wu_ref, wd_ref, ye_ref, wg_s, wu_s, wd_s, *, cap):
    @pl.when(pl.program_id(1) == 0)
    def _():
        wg_s[...] = wg_ref[0, 0].astype(BF16)
        wu_s[...] = wu_ref[0, 0].astype(BF16)
        wd_s[...] = wd_ref[0, 0].astype(BF16)

    onehot = _selection(pos_ref[0, 0], cap)
    sel = jnp.where(onehot, 1.0, 0.0).astype(BF16)
    gcol = jnp.sum(jnp.where(onehot, gate_ref[0, 0], 0.0), axis=1, keepdims=True)
    xe = jnp.dot(sel, h_ref[0], preferred_element_type=F32).astype(BF16)
    hg = jnp.dot(xe, wg_s[...], preferred_element_type=F32)
    hu = jnp.dot(xe, wu_s[...], preferred_element_type=F32)
    hid = (hg * jax.nn.sigmoid(hg) * hu).astype(BF16)
    ye_ref[0, 0] = (jnp.dot(hid, wd_s[...], preferred_element_type=F32) * gcol).astype(ye_ref.dtype)


def _expert_scatter_kernel(x_ref, pos_ref, ye_ref, o_ref, *, cap, eg):
    @pl.when(pl.program_id(1) == 0)
    def _():
        o_ref[...] = x_ref[...]

    sel = jnp.concatenate(
        [jnp.where(_selection(pos_ref[0, g], cap), 1.0, 0.0).astype(BF16) for g in range(eg)], axis=0)
    ye = ye_ref[0].reshape(eg * cap, ye_ref.shape[-1])
    o_ref[0] += lax.dot_general(sel, ye, TN_DIMS, preferred_element_type=F32)


def expert_choice_ffn(x, norm_g, w_router, w_gate, w_up, w_down, layer, eg=Tiles.scatter_experts):
    bsz, s, d = x.shape
    _, ne, _, ff = w_gate.shape
    cap = EC_CAPACITY * s // ne
    eg = min(eg, ne)
    pos, gate, hn = route(x, norm_g, w_router, cap)
    pos4 = pos.reshape(bsz, ne, 1, s)
    gate4 = gate.reshape(bsz, ne, 1, s)
    row = pl.BlockSpec((1, 1, 1, s), lambda e, b: (b, e, 0, 0))
    wspec = lambda shape: pl.BlockSpec((1, 1) + shape, lambda e, b: (layer, e, 0, 0))
    ye = pl.pallas_call(
        functools.partial(_expert_ffn_kernel, cap=cap),
        out_shape=jax.ShapeDtypeStruct((bsz, ne, cap, d), BF16),
        grid=(ne, bsz),
        in_specs=[pl.BlockSpec((1, s, d), lambda e, b: (b, 0, 0)), row, row,
                  wspec((d, ff)), wspec((d, ff)), wspec((ff, d))],
        out_specs=pl.BlockSpec((1, 1, cap, d), lambda e, b: (b, e, 0, 0)),
        scratch_shapes=[pltpu.VMEM((d, ff), BF16), pltpu.VMEM((d, ff), BF16), pltpu.VMEM((ff, d), BF16)],
        compiler_params=_params("arbitrary", "arbitrary"),
        name="ec_expert_ffn",
    )(hn, pos4, gate4, w_gate, w_up, w_down)
    tok = pl.BlockSpec((1, s, d), lambda b, j: (b, 0, 0))
    return pl.pallas_call(
        functools.partial(_expert_scatter_kernel, cap=cap, eg=eg),
        out_shape=jax.ShapeDtypeStruct((bsz, s, d), F32),
        grid=(bsz, ne // eg),
        in_specs=[tok, pl.BlockSpec((1, eg, 1, s), lambda b, j: (b, j, 0, 0)),
                  pl.BlockSpec((1, eg, cap, d), lambda b, j: (b, j, 0, 0))],
        out_specs=tok,
        compiler_params=_params("parallel", "arbitrary"),
        name="ec_expert_scatter",
    )(x, pos4, ye)


def rwkv7_layer(x, v_first, norm_g, mu, w_r, w_k, w_v, w_o, w0, w1, w2, a0, a1, a2,
                g1, g2, k_k, k_a, r_k, gn_g, gn_b, vres):
    bsz, s, d = x.shape
    r, na, v, g, bonus, lw, k_d, b_d, v_first = rwkv_pre(
        x, v_first, norm_g, mu, w_r, w_k, w_v, w0, w1, w2, a0, a1, a2, g1, g2, k_k, k_a, r_k, vres)
    yf = wkv7(r, lw[0], k_d[0], v, na, b_d[0], reverse=False)
    yb = wkv7(r, lw[1], k_d[1], v, na, b_d[1], reverse=True)
    flat = lambda t: t.reshape(bsz * s, d)
    out = rwkv_post(flat(yf), flat(yb), flat(bonus), flat(g), flat(x), gn_g, gn_b, w_o)
    return out.reshape(bsz, s, d), v_first


def rope_tables(positions):
    inv_freq = ROPE_THETA ** (-jnp.arange(0, QK_ROPE, 2, dtype=F32) / QK_ROPE)
    ang = positions.astype(F32)[..., None] * inv_freq
    cos, sin = jnp.cos(ang), jnp.sin(ang)
    shape = cos.shape[:-1]
    z = lambda n: jnp.zeros(shape + (n,), F32)
    tail = HEAD_PAD - QK_NOPE - QK_ROPE
    cf = jnp.concatenate([jnp.ones(shape + (QK_NOPE,), F32), cos, cos, z(tail)], -1)
    sf = jnp.concatenate([z(QK_NOPE), sin, sin, z(tail)], -1)
    return cf.reshape(-1, HEAD_PAD), sf.reshape(-1, HEAD_PAD)


def mla_layer(x, tables, norm_g, w_in, q_norm_g, kv_norm_g, w_uq, w_ukv, q_head_g, k_head_g, w_o):
    bsz, s, d = x.shape
    q, k, vt = mla_pre(x, tables, norm_g, w_in, q_norm_g, kv_norm_g, w_uq, w_ukv, q_head_g, k_head_g)
    o = attention(q, k, vt)
    return proj_residual(o.reshape(bsz * s, -1), w_o, x.reshape(bsz * s, d)).reshape(bsz, s, d)


def kernel(x, positions, norm_mix_g, norm_ffn_g,
           rw_mu, rw_wr, rw_wk, rw_wv, rw_wo, rw_w0, rw_w1, rw_w2,
           rw_a0, rw_a1, rw_a2, rw_g1, rw_g2, rw_kk, rw_ka, rw_rk,
           rw_gn_g, rw_gn_b, rw_v0, rw_v1, rw_v2,
           mla_w_in, mla_q_norm_g, mla_kv_norm_g, mla_w_uq, mla_w_ukv,
           mla_q_head_g, mla_k_head_g, mla_w_o,
           moe_router, moe_w_gate, moe_w_up, moe_w_down):
    depth = norm_mix_g.shape[0]
    n_mixers = 2
    tables = rope_tables(positions)
    v_first = None
    for i in range(depth):
        j = i // n_mixers
        if i % n_mixers == 0:
            vres = None if j == 0 else (rw_v0[j - 1], rw_v1[j - 1], rw_v2[j - 1])
            x, v_first = rwkv7_layer(
                x, v_first, norm_mix_g[i], rw_mu[j], rw_wr[j], rw_wk[j], rw_wv[j], rw_wo[j],
                rw_w0[j], rw_w1[j], rw_w2[j], rw_a0[j], rw_a1[j], rw_a2[j],
                rw_g1[j], rw_g2[j], rw_kk[j], rw_ka[j], rw_rk[j],
                rw_gn_g[j], rw_gn_b[j], vres)
        else:
            x = mla_layer(x, tables, norm_mix_g[i], mla_w_in[j], mla_q_norm_g[j], mla_kv_norm_g[j],
                          mla_w_uq[j], mla_w_ukv[j], mla_q_head_g[j], mla_k_head_g[j], mla_w_o[j])
        x = expert_choice_ffn(x, norm_ffn_g[i], moe_router[i], moe_w_gate, moe_w_up, moe_w_down, layer=i)
    return x
```

```python
import functools

import jax
import jax.numpy as jnp
from jax import lax
from jax.experimental import pallas as pl
from jax.experimental.pallas import tpu as pltpu

F32 = jnp.float32
BF16 = jnp.bfloat16

NORM_EPS = 1e-6
GN_EPS = 64e-5
RWKV_HEAD = 64
QK_NOPE = 64
QK_ROPE = 32
V_HEAD = 64
Q_LORA = 384
KV_LORA = 256
ROPE_THETA = 10000.0
EC_CAPACITY = 2
LANES = 128
SUBLANES = 8
HEAD_PAD = LANES
ONES_ROWS = 16
LOG2_E = 1.4426950408889634
DECAY_SCALE = 0.6065306597126334
WKV_CHUNK = 64
VMEM_LIMIT = 56 * 1024 * 1024


class Tiles:
    rwkv_pre_rows = 256
    mla_pre_rows = 512
    row_tile = 1024
    attn_queries = 1024
    attn_key_chunk = 256
    wkv_batch_rows = 2
    scatter_experts = 4

NT_DIMS = (((1,), (1,)), ((), ()))
TN_DIMS = (((0,), (0,)), ((), ()))


def _dot(a, b):
    return jnp.dot(a.astype(BF16), b.astype(BF16), preferred_element_type=F32)


def _dot_nt(a, b):
    return lax.dot_general(a.astype(BF16), b.astype(BF16), NT_DIMS, preferred_element_type=F32)


def _split_bf16(x):
    hi = x.astype(BF16)
    lo = (x - hi.astype(F32)).astype(BF16)
    return hi, lo


def _rms(x, gain):
    return x * lax.rsqrt(jnp.mean(x * x, axis=-1, keepdims=True) + NORM_EPS) * gain


def _seg_sum(t, seg):
    width = min(LANES, t.shape[1])
    ri = lax.broadcasted_iota(jnp.int32, (width, width), 0)
    ci = lax.broadcasted_iota(jnp.int32, (width, width), 1)
    ones = jnp.where((ri // seg) == (ci // seg), 1.0, 0.0).astype(BF16)
    parts = [_dot(t[:, j:j + width], ones) for j in range(0, t.shape[1], width)]
    return parts[0] if len(parts) == 1 else jnp.concatenate(parts, axis=1)


def _pad_to(w, axis, mult):
    pad = -w.shape[axis] % mult
    if pad == 0:
        return w
    widths = [(0, 0)] * w.ndim
    widths[axis] = (0, pad)
    return jnp.pad(w, widths)


def _const_spec(shape):
    return pl.BlockSpec(shape, lambda *_: (0,) * len(shape))


def _params(*sem):
    return pltpu.CompilerParams(dimension_semantics=sem, vmem_limit_bytes=VMEM_LIMIT)


def _rwkv_pre_kernel(*refs, has_vres, seg):
    it = iter(refs)
    (x_ref, xp_ref, xn_ref, vec_ref, wr_ref, wk_ref, wv_ref,
     w1_ref, a1_ref, g1_ref, w2_ref, a2_ref, g2_ref) = [next(it) for _ in range(13)]
    if has_vres:
        v1_ref, v2_ref, vf_ref = next(it), next(it), next(it)
    r_o, na_o, v_o, g_o, bon_o = [next(it) for _ in range(5)]
    lw_o = [next(it), next(it)]
    k_o = [next(it), next(it)]
    b_o = [next(it), next(it)]
    if not has_vres:
        vf_o = next(it)

    i = pl.program_id(1)
    vec = vec_ref[...]
    row_of = lambda j: vec[j:j + 1]
    gain = row_of(0)
    x = x_ref[0]
    bm = x.shape[0]
    h = _rms(x, gain)
    hp = jnp.where(i == 0, 0.0, _rms(xp_ref[0][SUBLANES - 1:SUBLANES], gain))
    hn = jnp.where(i == pl.num_programs(1) - 1, 0.0, _rms(xn_ref[0][0:1], gain))
    row = lax.broadcasted_iota(jnp.int32, (bm, 1), 0)
    h_prev = jnp.where(row == 0, hp, pltpu.roll(h, 1, 0))
    h_next = jnp.where(row == bm - 1, hn, pltpu.roll(h, bm - 1, 0))
    xx = 0.5 * (h_prev + h_next) - h
    xr, xw, xk, xv, xa, xg = [(h + xx * row_of(1 + j)).astype(BF16) for j in range(6)]

    r = _dot(xr, wr_ref[...])
    k = _dot(xk, wk_ref[...])
    v = _dot(xv, wv_ref[...])
    if has_vres:
        vz = row_of(14) + _dot(_dot(xv, v1_ref[...]), v2_ref[...])
        v = v + (vf_ref[0] - v) * jax.nn.sigmoid(vz)
    else:
        vf_o[0] = v
    g = _dot(jax.nn.sigmoid(_dot(xg, g1_ref[...])), g2_ref[...])

    kkr = k * row_of(11)
    kk = kkr * lax.rsqrt(jnp.maximum(_seg_sum(kkr * kkr, seg), 1e-24))
    tw = jnp.tanh(_dot(xw, w1_ref[...])).astype(BF16)
    al = _dot(xa, a1_ref[...]).astype(BF16)
    k_keep = k * (1.0 - row_of(12))
    k_mix = k * row_of(12)
    k_ds = []
    for d in range(2):
        z = row_of(7 + d) + _dot(tw, w2_ref[d])
        lw_o[d][0] = -DECAY_SCALE * jax.nn.sigmoid(z)
        a = jax.nn.sigmoid(row_of(9 + d) + _dot(al, a2_ref[d]))
        k_d = k_keep + k_mix * a
        k_o[d][0] = k_d.astype(BF16)
        b_o[d][0] = (kk * a).astype(BF16)
        k_ds.append(k_d)
    bon_o[0] = (_seg_sum(r * (k_ds[0] + k_ds[1]) * row_of(13), seg) * v).astype(BF16)
    r_o[0] = r.astype(BF16)
    na_o[0] = (-kk).astype(BF16)
    v_o[0] = v.astype(BF16)
    g_o[0] = g.astype(BF16)


def rwkv_pre(x, v_first, norm_g, mu, w_r, w_k, w_v, w0, w1, w2, a0, a1, a2, g1, g2, k_k, k_a, r_k,
             vres, bm=Tiles.rwkv_pre_rows):
    bsz, s, d = x.shape
    bm = min(bm, s)
    nt = s // bm
    has_vres = vres is not None
    zero = jnp.zeros((d,), F32)
    v0 = vres[0] if has_vres else zero
    vec = jnp.stack([norm_g, *mu, w0[0], w0[1], a0[0], a0[1], k_k, k_a, r_k.reshape(d), v0, zero])
    bf = lambda t: t.astype(BF16)
    cat2 = lambda t: jnp.concatenate([t[0], t[1]], axis=1)
    lo = w1.shape[2]
    second = lambda t: jnp.stack([jnp.pad(t[0], ((0, lo), (0, 0))), jnp.pad(t[1], ((lo, 0), (0, 0)))])
    ins = [x, x, x, vec, bf(w_r), bf(w_k), bf(w_v), bf(cat2(w1)), bf(cat2(a1)), bf(_pad_to(g1, 1, LANES)),
           bf(second(w2)), bf(second(a2)), bf(_pad_to(g2, 0, LANES))]
    tile = pl.BlockSpec((1, bm, d), lambda b, i: (b, i, 0))
    hb = bm // SUBLANES
    in_specs = [tile,
                pl.BlockSpec((1, SUBLANES, d), lambda b, i: (b, jnp.maximum(i * hb - 1, 0), 0)),
                pl.BlockSpec((1, SUBLANES, d), lambda b, i: (b, jnp.minimum((i + 1) * hb, s // SUBLANES - 1), 0))]
    in_specs += [_const_spec(t.shape) for t in ins[3:]]
    if has_vres:
        extra = [bf(_pad_to(vres[1], 1, LANES)), bf(_pad_to(vres[2], 0, LANES))]
        ins += extra + [v_first]
        in_specs += [_const_spec(t.shape) for t in extra] + [tile]
    n_bf, n_f32 = 5, 2
    out_shape = ([jax.ShapeDtypeStruct((bsz, s, d), BF16)] * n_bf + [jax.ShapeDtypeStruct((bsz, s, d), F32)] * n_f32
                 + [jax.ShapeDtypeStruct((bsz, s, d), BF16)] * 4)
    if not has_vres:
        out_shape.append(jax.ShapeDtypeStruct((bsz, s, d), F32))
    outs = pl.pallas_call(
        functools.partial(_rwkv_pre_kernel, has_vres=has_vres, seg=RWKV_HEAD),
        out_shape=tuple(out_shape),
        grid=(bsz, nt),
        in_specs=in_specs,
        out_specs=tuple([tile] * len(out_shape)),
        compiler_params=_params("parallel", "parallel"),
        name="rwkv_pre",
    )(*ins)
    r, na, v, g, bonus, lw0, lw1, k0, k1, b0, b1 = outs[:11]
    v_first = v_first if has_vres else outs[11]
    return r, na, v, g, bonus, (lw0, lw1), (k0, k1), (b0, b1), v_first


def _wkv_kernel(r_ref, lw_ref, k_ref, v_ref, a_ref, b_ref, y_ref, h_scr, *, reverse, nh, hd):
    c = pl.program_id(1)

    @pl.when(c == 0)
    def _():
        h_scr[...] = jnp.zeros_like(h_scr)

    nb, L, _ = lw_ref.shape
    row = lax.broadcasted_iota(jnp.int32, (L, L), 0)
    col = lax.broadcasted_iota(jnp.int32, (L, L), 1)
    incl = (col >= row) if reverse else (col <= row)
    tri = jnp.where(incl, 1.0, 0.0).astype(BF16)
    eye = jnp.where(row == col, 1.0, 0.0).astype(F32)
    last = 0 if reverse else L - 1

    def scaled(bi):
        lw = lw_ref[bi]
        lw_hi, lw_lo = _split_bf16(lw)
        cs = (jnp.dot(tri, lw_hi, preferred_element_type=F32)
              + jnp.dot(tri, lw_lo, preferred_element_type=F32))
        ctot = cs[last:last + 1, :]
        r, k, v, a, b = [t[bi].astype(F32) for t in (r_ref, k_ref, v_ref, a_ref, b_ref)]
        einv = jnp.exp(-cs)
        etot = jnp.exp(ctot - cs)
        return dict(rt=r * jnp.exp(cs), at=a * jnp.exp(cs - lw), bt=b * einv, kt=k * einv,
                    bb=b * etot, kb=k * etot, v=v, wtot=jnp.exp(ctot))

    sc_in = [scaled(bi) for bi in range(nb)]

    row2 = lax.broadcasted_iota(jnp.int32, (2 * L, 2 * L), 0)
    col2 = lax.broadcasted_iota(jnp.int32, (2 * L, 2 * L), 1)
    rt2, cs2 = row2 & (L - 1), col2 & (L - 1)
    before = (cs2 > rt2) if reverse else (cs2 < rt2)
    score_mask = before | ((row2 >= L) & (cs2 == rt2))
    right_half = lax.broadcasted_iota(jnp.int32, (L, 2 * L), 1) >= L
    diag_blk = (row ^ col) < 2
    level_masks = []
    size = 2
    while size < L:
        x = row ^ col
        level_masks.append((x >= size) & (x < 2 * size))
        size *= 2

    T = 2 * hd
    lane_lo = lax.broadcasted_iota(jnp.int32, (1, T), 1) < hd
    rowp = lax.broadcasted_iota(jnp.int32, (T, T), 0)
    colp = lax.broadcasted_iota(jnp.int32, (T, T), 1)
    diag_blocks = (rowp < hd) == (colp < hd)
    eye_t = jnp.where(rowp == colp, 1.0, 0.0).astype(F32)
    tiles = [(bi, p) for bi in range(nb) for p in range(nh // 2)]
    units = range(len(tiles))
    heads = [(u, q) for u in units for q in range(2)]
    tile = lambda name, u: sc_in[tiles[u][0]][name][:, tiles[u][1] * T:(tiles[u][1] + 1) * T]
    own = lambda q: lane_lo if q == 0 else jnp.logical_not(lane_lo)

    a2 = [jnp.concatenate([tile("at", u), tile("rt", u)], 0) for u in units]
    b2 = [jnp.concatenate([tile("bt", u), tile("kt", u)], 0).astype(BF16) for u in units]
    v_sw = [pltpu.roll(tile("v", u), hd, 1) for u in units]
    vv_sw = [jnp.concatenate([t_.astype(BF16)] * 2, 0) for t_ in v_sw]
    bbkb_t = [jnp.concatenate([tile("bb", u), tile("kb", u)], 0).T.astype(BF16) for u in units]

    sc = [jnp.where(score_mask, _dot_nt(jnp.where(own(q), a2[u], 0.0), b2[u]), 0.0) for u, q in heads]
    top = [s_[:L] for s_ in sc]
    bot = [s_[L:] for s_ in sc]
    n_ab = [t_[:, :L] for t_ in top]
    t = [eye + jnp.where(diag_blk, n_, 0.0) for n_ in n_ab]
    for lm in level_masks:
        w = [_dot(jnp.where(lm, n_, 0.0), t_) for n_, t_ in zip(n_ab, t)]
        t = [t_ + _dot(t_, w_) for t_, w_ in zip(t, w)]
    x = [_dot(jnp.where(right_half, top[i], 0.0), vv_sw[u]) for i, (u, q) in enumerate(heads)]
    tz = [_dot(t[i], jnp.where(own(q), tile("at", u), x[i])) for i, (u, q) in enumerate(heads)]
    rhs = [jnp.concatenate([tz[i].astype(BF16), jnp.where(own(q), 0.0, v_sw[u]).astype(BF16)], 0)
           for i, (u, q) in enumerate(heads)]
    qp = [_dot(jnp.concatenate([bot[i].astype(BF16), bbkb_t[u][q * hd:(q + 1) * hd]], 0), rhs[i])
          for i, (u, q) in enumerate(heads)]
    for u in units:
        bi, p = tiles[u]
        q0, q1 = qp[2 * u], qp[2 * u + 1]
        rh = tile("rt", u) + jnp.where(lane_lo, q0[:L], q1[:L])
        yh = pltpu.roll(jnp.where(lane_lo, q1[:L], q0[:L]), hd, 1)
        prow = jnp.concatenate([q0[L:], q1[L:]], 0)
        m = jnp.where(diag_blocks, prow, 0.0) + eye_t * tile("wtot", u)
        g = pltpu.roll(jnp.where(diag_blocks, 0.0, prow), hd, 1)
        fin = _dot(jnp.concatenate([rh, m], 0), h_scr[bi, p])
        y_ref[bi, :, p * T:(p + 1) * T] = (fin[:L] + yh).astype(y_ref.dtype)
        h_scr[bi, p] = fin[L:] + g


def wkv7(r, lw, k, v, a, b, reverse, nb=Tiles.wkv_batch_rows):
    bsz, s, d = r.shape
    hd = RWKV_HEAD
    nh = d // hd
    L = min(WKV_CHUNK, s)
    nb = min(nb, bsz)
    assert L == hd and 2 * hd == LANES and nh % 2 == 0 and s % L == 0 and bsz % nb == 0
    nc = s // L
    if reverse:
        idx = lambda bi, ci: (bi, nc - 1 - ci, 0)
    else:
        idx = lambda bi, ci: (bi, ci, 0)
    spec = pl.BlockSpec((nb, L, d), idx)
    return pl.pallas_call(
        functools.partial(_wkv_kernel, reverse=reverse, nh=nh, hd=hd),
        out_shape=jax.ShapeDtypeStruct((bsz, s, d), BF16),
        grid=(bsz // nb, nc),
        in_specs=[spec] * 6,
        out_specs=spec,
        scratch_shapes=[pltpu.VMEM((nb, nh // 2, 2 * hd, 2 * hd), F32)],
        compiler_params=_params("parallel", "arbitrary"),
        name="wkv7_rev" if reverse else "wkv7_fwd",
    )(r, lw, k, v, a, b)


def _rwkv_post_kernel(yf_ref, yb_ref, bon_ref, g_ref, x_ref, vec_ref, wo_ref, o_ref, *, seg):
    y = yf_ref[...].astype(F32) + yb_ref[...].astype(F32)
    yc = y - _seg_sum(y, seg) * (1.0 / seg)
    var = _seg_sum(yc * yc, seg) * (1.0 / seg)
    vec = vec_ref[...]
    yn = yc * lax.rsqrt(var + GN_EPS) * vec[0:1] + vec[1:2] + bon_ref[...].astype(F32)
    o_ref[...] = x_ref[...] + _dot(yn * g_ref[...].astype(F32), wo_ref[...])


def rwkv_post(yf, yb, bonus, g, x, gn_g, gn_b, w_o, bm=Tiles.row_tile):
    m, d = x.shape
    bm = min(bm, m)
    vec = _pad_to(jnp.stack([gn_g, gn_b]), 0, SUBLANES)
    tile = pl.BlockSpec((bm, d), lambda i: (i, 0))
    return pl.pallas_call(
        functools.partial(_rwkv_post_kernel, seg=RWKV_HEAD),
        out_shape=jax.ShapeDtypeStruct((m, d), F32),
        grid=(m // bm,),
        in_specs=[tile] * 5 + [_const_spec(vec.shape), _const_spec(w_o.shape)],
        out_specs=tile,
        compiler_params=_params("parallel"),
        name="rwkv_post",
    )(yf, yb, bonus, g, x, vec, w_o.astype(BF16))


def _head_norm_rope(y2, ctab, stab, o_ref, nh):
    inv_dim = 1.0 / (QK_NOPE + QK_ROPE)
    for h in range(nh):
        sl = slice(h * HEAD_PAD, (h + 1) * HEAD_PAD)
        rh = slice((nh + h) * HEAD_PAD, (nh + h + 1) * HEAD_PAD)
        yh = y2[:, sl]
        ms = jnp.sum(yh * yh, axis=-1, keepdims=True) * inv_dim
        o_ref[0, :, sl] = ((yh * ctab + y2[:, rh] * stab) * lax.rsqrt(ms + NORM_EPS)).astype(o_ref.dtype)


def _mla_pre_kernel(x_ref, gx_ref, gc_ref, hg_ref, cf_ref, sf_ref, win_ref, wq_ref, wk_ref, wvt_ref,
                    q_ref, k_ref, vt_ref, *, nh):
    gc = gc_ref[...]
    h = _rms(x_ref[0], gx_ref[...])
    c = _dot(h, win_ref[...])
    c_q = _rms(c[:, :Q_LORA], gc[0:1])
    c_kv = _rms(c[:, Q_LORA:Q_LORA + KV_LORA], gc[1:2, :KV_LORA]).astype(BF16)
    cf, sf = cf_ref[...], sf_ref[...]
    hg = hg_ref[...]
    _head_norm_rope(_dot(c_q, wq_ref[...]), cf * hg[0:1], sf * hg[1:2], q_ref, nh)
    k_in = jnp.concatenate([c_kv, c[:, Q_LORA + KV_LORA:].astype(BF16)], axis=1)
    _head_norm_rope(_dot(k_in, wk_ref[...]), cf * hg[2:3], sf * hg[3:4], k_ref, nh)
    vt_ref[0] = _dot_nt(wvt_ref[...], c_kv).astype(vt_ref.dtype)


def _pad_heads(w, nh, width, take):
    k = w.shape[0]
    wh = w.reshape(k, nh, width)[:, :, take]
    wh = jnp.pad(wh, ((0, 0), (0, 0), (0, HEAD_PAD - wh.shape[-1])))
    return wh.reshape(k, nh * HEAD_PAD)


def _with_rotate_half(w_pad, nh):
    k = w_pad.shape[0]
    half = QK_ROPE // 2
    w3 = w_pad.reshape(k, nh, HEAD_PAD)
    x1 = w3[:, :, QK_NOPE:QK_NOPE + half]
    x2 = w3[:, :, QK_NOPE + half:QK_NOPE + QK_ROPE]
    z = lambda n: jnp.zeros((k, nh, n), w_pad.dtype)
    rh = jnp.concatenate([z(QK_NOPE), -x2, x1, z(HEAD_PAD - QK_NOPE - QK_ROPE)], axis=-1)
    return jnp.concatenate([w_pad, rh.reshape(k, nh * HEAD_PAD)], axis=1)


def _gain_rows(g):
    half = QK_ROPE // 2
    tail = jnp.zeros((HEAD_PAD - QK_NOPE - QK_ROPE,), F32)
    straight = jnp.concatenate([g, tail])
    swapped = jnp.concatenate([jnp.zeros((QK_NOPE,), F32), g[QK_NOPE + half:], g[QK_NOPE:QK_NOPE + half], tail])
    return [straight, swapped]


def mla_pre(x, tables, norm_g, w_in, q_norm_g, kv_norm_g, w_uq, w_ukv, q_head_g, k_head_g,
            bm=Tiles.mla_pre_rows):
    bsz, s, d = x.shape
    bm = min(bm, s)
    qk = QK_NOPE + QK_ROPE
    nh = w_uq.shape[1] // qk
    assert Q_LORA % LANES == 0 and KV_LORA % LANES == 0 and KV_LORA <= Q_LORA
    gx = norm_g.reshape(1, d)
    gc = _pad_to(jnp.stack([q_norm_g, jnp.pad(kv_norm_g, (0, Q_LORA - KV_LORA))]), 0, SUBLANES)
    hg = _pad_to(jnp.stack(_gain_rows(q_head_g * (qk ** -0.5 * LOG2_E)) + _gain_rows(k_head_g)), 0, SUBLANES)
    win = _pad_to(w_in, 1, LANES)
    rope_w = win.shape[1] - Q_LORA - KV_LORA
    wq = _with_rotate_half(_pad_heads(w_uq, nh, qk, slice(0, qk)), nh)
    wk = _pad_heads(w_ukv, nh, QK_NOPE + V_HEAD, slice(0, QK_NOPE))
    place = jnp.zeros((rope_w, nh, HEAD_PAD), F32)
    place = place.at[jnp.arange(QK_ROPE), :, QK_NOPE + jnp.arange(QK_ROPE)].set(1.0)
    wk = _with_rotate_half(jnp.concatenate([wk, place.reshape(rope_w, nh * HEAD_PAD)], 0), nh)
    wvt = w_ukv.reshape(KV_LORA, nh, QK_NOPE + V_HEAD)[:, :, QK_NOPE:].reshape(KV_LORA, nh * V_HEAD).T
    weights = [t.astype(BF16) for t in (win, wq, wk, wvt)]
    nt = s // bm
    tab = pl.BlockSpec((bm, HEAD_PAD), lambda b, i: (b * nt + i, 0))
    row = lambda n: pl.BlockSpec((1, bm, n), lambda b, i: (b, i, 0))
    return pl.pallas_call(
        functools.partial(_mla_pre_kernel, nh=nh),
        out_shape=(jax.ShapeDtypeStruct((bsz, s, nh * HEAD_PAD), BF16),
                   jax.ShapeDtypeStruct((bsz, s, nh * HEAD_PAD), BF16),
                   jax.ShapeDtypeStruct((bsz, nh * V_HEAD, s), BF16)),
        grid=(bsz, nt),
        in_specs=[row(d), _const_spec(gx.shape), _const_spec(gc.shape), _const_spec(hg.shape), tab, tab]
                 + [_const_spec(t.shape) for t in weights],
        out_specs=(row(nh * HEAD_PAD), row(nh * HEAD_PAD),
                   pl.BlockSpec((1, nh * V_HEAD, bm), lambda b, i: (b, 0, i))),
        compiler_params=_params("parallel", "parallel"),
        name="mla_pre",
    )(x, gx, gc, hg, *tables, *weights)


def _attn_kernel(q_ref, k_ref, vt_ref, o_ref, *, kc):
    s = k_ref.shape[1]
    heads = range(2)
    qs = [q_ref[0, :, j * HEAD_PAD:(j + 1) * HEAD_PAD] for j in heads]

    def scores(c):
        return [_dot_nt(k_ref[0, c * kc:(c + 1) * kc, j * HEAD_PAD:(j + 1) * HEAD_PAD], qs[j]) for j in heads]

    ones = jnp.ones((ONES_ROWS, kc), BF16)
    nxt = scores(0)
    m, acc = [None] * 2, [None] * 2
    for c in range(s // kc):
        cur = nxt
        if (c + 1) * kc < s:
            nxt = scores(c + 1)
        for j in heads:
            vt = jnp.concatenate([vt_ref[0, j * V_HEAD:(j + 1) * V_HEAD, c * kc:(c + 1) * kc], ones], axis=0)
            mc = jnp.max(cur[j], axis=0, keepdims=True)
            if c == 0:
                m[j] = mc
                acc[j] = jnp.dot(vt, jnp.exp2((cur[j] - mc).astype(BF16)), preferred_element_type=F32)
            else:
                m_new = jnp.maximum(m[j], mc)
                pt = jnp.exp2((cur[j] - m_new).astype(BF16))
                acc[j] = jnp.exp2(m[j] - m_new) * acc[j] + jnp.dot(vt, pt, preferred_element_type=F32)
                m[j] = m_new
    out = jnp.concatenate([acc[j][:V_HEAD] / acc[j][V_HEAD:V_HEAD + 1] for j in heads], axis=0)
    o_ref[0] = out.T.astype(o_ref.dtype)


def attention(q, k, vt, tq=Tiles.attn_queries, kc=Tiles.attn_key_chunk):
    bsz, s, _ = q.shape
    nh = vt.shape[1] // V_HEAD
    tq = min(tq, s)
    kc = min(kc, s)
    return pl.pallas_call(
        functools.partial(_attn_kernel, kc=kc),
        out_shape=jax.ShapeDtypeStruct((bsz, s, nh * V_HEAD), BF16),
        grid=(bsz, nh // 2, s // tq),
        in_specs=[pl.BlockSpec((1, tq, 2 * HEAD_PAD), lambda b, h, i: (b, i, h)),
                  pl.BlockSpec((1, s, 2 * HEAD_PAD), lambda b, h, i: (b, 0, h)),
                  pl.BlockSpec((1, 2 * V_HEAD, s), lambda b, h, i: (b, h, 0))],
        out_specs=pl.BlockSpec((1, tq, 2 * V_HEAD), lambda b, h, i: (b, i, h)),
        compiler_params=_params("parallel", "parallel", "parallel"),
        name="mla_attention",
    )(q, k, vt)


def _proj_res_kernel(a_ref, w_ref, x_ref, o_ref):
    o_ref[...] = x_ref[...] + _dot(a_ref[...], w_ref[...])


def proj_residual(a, w, x, bm=Tiles.row_tile):
    m, k = a.shape
    n = w.shape[1]
    bm = min(bm, m)
    return pl.pallas_call(
        _proj_res_kernel,
        out_shape=jax.ShapeDtypeStruct((m, n), F32),
        grid=(m // bm,),
        in_specs=[pl.BlockSpec((bm, k), lambda i: (i, 0)), _const_spec(w.shape),
                  pl.BlockSpec((bm, n), lambda i: (i, 0))],
        out_specs=pl.BlockSpec((bm, n), lambda i: (i, 0)),
        compiler_params=_params("parallel"),
        name="proj_residual",
    )(a, w.astype(BF16), x)


def _affinity_kernel(x_ref, g_ref, wr_ref, gate_ref, hn_ref):
    h = _rms(x_ref[0], g_ref[...])
    hn_ref[0] = h.astype(hn_ref.dtype)
    h_hi, h_lo = _split_bf16(h)
    w_hi, w_lo = _split_bf16(wr_ref[...])
    logits = (lax.dot_general(w_hi, h_hi, NT_DIMS, preferred_element_type=F32)
              + lax.dot_general(w_hi, h_lo, NT_DIMS, preferred_element_type=F32)
              + lax.dot_general(w_lo, h_hi, NT_DIMS, preferred_element_type=F32))
    ex = jnp.exp(logits - jnp.max(logits, axis=0, keepdims=True))
    gate_ref[0] = ex / jnp.sum(ex, axis=0, keepdims=True)


def _select_kernel(aff_ref, pos_ref, *, cap):
    aff = aff_ref[...]
    ne, s = aff.shape
    bits = pltpu.bitcast(aff, jnp.int32)

    def count(mask):
        return jnp.sum(jnp.where(mask, 1.0, 0.0), axis=1, keepdims=True)

    def search(nbits, accept):
        value = jnp.zeros((ne, 1), jnp.int32)
        shift = nbits
        while shift > 0:
            width = 2 if shift % 2 == 0 else 1
            shift -= width
            digit = jnp.zeros((ne, 1), jnp.int32)
            for j in range(1, 1 << width):
                digit = digit + jnp.where(accept(value + (j << shift)), 1, 0)
            value = value + jnp.left_shift(digit, shift)
        return value

    thr = search(31, lambda cand: count(bits >= cand) >= cap)
    gt = bits > thr
    eq = bits == thr
    need = cap - count(gt)
    idx = lax.broadcasted_iota(jnp.int32, (ne, s), 1)
    cut = search(s.bit_length(), lambda cand: count(eq & (idx < cand)) <= need)
    sel = gt | (eq & (idx < cut))

    blk = LANES if s % LANES == 0 else s
    ri = lax.broadcasted_iota(jnp.int32, (blk, blk), 0)
    ci = lax.broadcasted_iota(jnp.int32, (blk, blk), 1)
    upper = jnp.where(ri <= ci, 1.0, 0.0).astype(BF16)
    off = jnp.zeros((ne, 1), F32)
    for j in range(s // blk):
        sl = slice(j * blk, (j + 1) * blk)
        sel_j = sel[:, sl]
        inc = jnp.dot(jnp.where(sel_j, 1.0, 0.0).astype(BF16), upper, preferred_element_type=F32)
        pos_ref[:, sl] = jnp.where(sel_j, (off + inc - 1.0).astype(jnp.int32), -1)
        off = off + inc[:, blk - 1:blk]


def route(x, norm_g, w_router, cap):
    bsz, s, d = x.shape
    ne = w_router.shape[1]
    out = pl.BlockSpec((1, ne, s), lambda b: (b, 0, 0))
    tok = pl.BlockSpec((1, s, d), lambda b: (b, 0, 0))
    gate, hn = pl.pallas_call(
        _affinity_kernel,
        out_shape=(jax.ShapeDtypeStruct((bsz, ne, s), F32), jax.ShapeDtypeStruct((bsz, s, d), BF16)),
        grid=(bsz,),
        in_specs=[tok, _const_spec((1, d)), _const_spec((ne, d))],
        out_specs=(out, tok),
        compiler_params=_params("parallel"),
        name="ec_affinity",
    )(x, norm_g.reshape(1, d), w_router.T)
    pos = pl.pallas_call(
        functools.partial(_select_kernel, cap=cap),
        out_shape=jax.ShapeDtypeStruct((bsz * ne, s), jnp.int32),
        grid=(1,),
        in_specs=[_const_spec((bsz * ne, s))],
        out_specs=_const_spec((bsz * ne, s)),
        compiler_params=_params("arbitrary"),
        name="ec_select",
    )(gate.reshape(bsz * ne, s))
    return pos.reshape(bsz, ne, s), gate, hn


def _selection(pos_row, cap):
    return pos_row == lax.broadcasted_iota(jnp.int32, (cap, pos_row.shape[1]), 0)


def _expert_ffn_kernel(h_ref, pos_ref, gate_ref, wg_ref, wu_ref, wd_ref, ye_ref, wg_s, wu_s, wd_s, *, cap):
    @pl.when(pl.program_id(1) == 0)
    def _():
        wg_s[...] = wg_ref[0, 0].astype(BF16)
        wu_s[...] = wu_ref[0, 0].astype(BF16)
        wd_s[...] = wd_ref[0, 0].astype(BF16)

    onehot = _selection(pos_ref[0, 0], cap)
    sel = jnp.where(onehot, 1.0, 0.0).astype(BF16)
    gcol = jnp.sum(jnp.where(onehot, gate_ref[0, 0], 0.0), axis=1, keepdims=True)
    xe = jnp.dot(sel, h_ref[0], preferred_element_type=F32).astype(BF16)
    hg = jnp.dot(xe, wg_s[...], preferred_element_type=F32)
    hu = jnp.dot(xe, wu_s[...], preferred_element_type=F32)
    hid = (hg * jax.nn.sigmoid(hg) * hu).astype(BF16)
    ye_ref[0, 0] = (jnp.dot(hid, wd_s[...], preferred_element_type=F32) * gcol).astype(ye_ref.dtype)


def _expert_scatter_kernel(x_ref, pos_ref, ye_ref, o_ref, *, cap, eg):
    @pl.when(pl.program_id(1) == 0)
    def _():
        o_ref[...] = x_ref[...]

    sel = jnp.concatenate(
        [jnp.where(_selection(pos_ref[0, g], cap), 1.0, 0.0).astype(BF16) for g in range(eg)], axis=0)
    ye = ye_ref[0].reshape(eg * cap, ye_ref.shape[-1])
    o_ref[0] += lax.dot_general(sel, ye, TN_DIMS, preferred_element_type=F32)


def expert_choice_ffn(x, norm_g, w_router, w_gate, w_up, w_down, layer, eg=Tiles.scatter_experts):
    bsz, s, d = x.shape
    _, ne, _, ff = w_gate.shape
    cap = EC_CAPACITY * s // ne
    eg = min(eg, ne)
    pos, gate, hn = route(x, norm_g, w_router, cap)
    pos4 = pos.reshape(bsz, ne, 1, s)
    gate4 = gate.reshape(bsz, ne, 1, s)
    row = pl.BlockSpec((1, 1, 1, s), lambda e, b: (b, e, 0, 0))
    wspec = lambda shape: pl.BlockSpec((1, 1) + shape, lambda e, b: (layer, e, 0, 0))
    ye = pl.pallas_call(
        functools.partial(_expert_ffn_kernel, cap=cap),
        out_shape=jax.ShapeDtypeStruct((bsz, ne, cap, d), BF16),
        grid=(ne, bsz),
        in_specs=[pl.BlockSpec((1, s, d), lambda e, b: (b, 0, 0)), row, row,
                  wspec((d, ff)), wspec((d, ff)), wspec((ff, d))],
        out_specs=pl.BlockSpec((1, 1, cap, d), lambda e, b: (b, e, 0, 0)),
        scratch_shapes=[pltpu.VMEM((d, ff), BF16), pltpu.VMEM((d, ff), BF16), pltpu.VMEM((ff, d), BF16)],
        compiler_params=_params("arbitrary", "arbitrary"),
        name="ec_expert_ffn",
    )(hn, pos4, gate4, w_gate, w_up, w_down)
    tok = pl.BlockSpec((1, s, d), lambda b, j: (b, 0, 0))
    return pl.pallas_call(
        functools.partial(_expert_scatter_kernel, cap=cap, eg=eg),
        out_shape=jax.ShapeDtypeStruct((bsz, s, d), F32),
        grid=(bsz, ne // eg),
        in_specs=[tok, pl.BlockSpec((1, eg, 1, s), lambda b, j: (b, j, 0, 0)),
                  pl.BlockSpec((1, eg, cap, d), lambda b, j: (b, j, 0, 0))],
        out_specs=tok,
        compiler_params=_params("parallel", "arbitrary"),
        name="ec_expert_scatter",
    )(x, pos4, ye)


def rwkv7_layer(x, v_first, norm_g, mu, w_r, w_k, w_v, w_o, w0, w1, w2, a0, a1, a2,
                g1, g2, k_k, k_a, r_k, gn_g, gn_b, vres):
    bsz, s, d = x.shape
    r, na, v, g, bonus, lw, k_d, b_d, v_first = rwkv_pre(
        x, v_first, norm_g, mu, w_r, w_k, w_v, w0, w1, w2, a0, a1, a2, g1, g2, k_k, k_a, r_k, vres)
    yf = wkv7(r, lw[0], k_d[0], v, na, b_d[0], reverse=False)
    yb = wkv7(r, lw[1], k_d[1], v, na, b_d[1], reverse=True)
    flat = lambda t: t.reshape(bsz * s, d)
    out = rwkv_post(flat(yf), flat(yb), flat(bonus), flat(g), flat(x), gn_g, gn_b, w_o)
    return out.reshape(bsz, s, d), v_first


def rope_tables(positions):
    inv_freq = ROPE_THETA ** (-jnp.arange(0, QK_ROPE, 2, dtype=F32) / QK_ROPE)
    ang = positions.astype(F32)[..., None] * inv_freq
    cos, sin = jnp.cos(ang), jnp.sin(ang)
    shape = cos.shape[:-1]
    z = lambda n: jnp.zeros(shape + (n,), F32)
    tail = HEAD_PAD - QK_NOPE - QK_ROPE
    cf = jnp.concatenate([jnp.ones(shape + (QK_NOPE,), F32), cos, cos, z(tail)], -1)
    sf = jnp.concatenate([z(QK_NOPE), sin, sin, z(tail)], -1)
    return cf.reshape(-1, HEAD_PAD), sf.reshape(-1, HEAD_PAD)


def mla_layer(x, tables, norm_g, w_in, q_norm_g, kv_norm_g, w_uq, w_ukv, q_head_g, k_head_g, w_o):
    bsz, s, d = x.shape
    q, k, vt = mla_pre(x, tables, norm_g, w_in, q_norm_g, kv_norm_g, w_uq, w_ukv, q_head_g, k_head_g)
    o = attention(q, k, vt)
    return proj_residual(o.reshape(bsz * s, -1), w_o, x.reshape(bsz * s, d)).reshape(bsz, s, d)


def kernel(x, positions, norm_mix_g, norm_ffn_g,
           rw_mu, rw_wr, rw_wk, rw_wv, rw_wo, rw_w0, rw_w1, rw_w2,
           rw_a0, rw_a1, rw_a2, rw_g1, rw_g2, rw_kk, rw_ka, rw_rk,
           rw_gn_g, rw_gn_b, rw_v0, rw_v1, rw_v2,
           mla_w_in, mla_q_norm_g, mla_kv_norm_g, mla_w_uq, mla_w_ukv,
           mla_q_head_g, mla_k_head_g, mla_w_o,
           moe_router, moe_w_gate, moe_w_up, moe_w_down):
    depth = norm_mix_g.shape[0]
    n_mixers = 2
    tables = rope_tables(positions)
    v_first = None
    for i in range(depth):
        j = i // n_mixers
        if i % n_mixers == 0:
            vres = None if j == 0 else (rw_v0[j - 1], rw_v1[j - 1], rw_v2[j - 1])
            x, v_first = rwkv7_layer(
                x, v_first, norm_mix_g[i], rw_mu[j], rw_wr[j], rw_wk[j], rw_wv[j], rw_wo[j],
                rw_w0[j], rw_w1[j], rw_w2[j], rw_a0[j], rw_a1[j], rw_a2[j],
                rw_g1[j], rw_g2[j], rw_kk[j], rw_ka[j], rw_rk[j],
                rw_gn_g[j], rw_gn_b[j], vres)
        else:
            x = mla_layer(x, tables, norm_mix_g[i], mla_w_in[j], mla_q_norm_g[j], mla_kv_norm_g[j],
                          mla_w_uq[j], mla_w_ukv[j], mla_q_head_g[j], mla_k_head_g[j], mla_w_o[j])
        x = expert_choice_ffn(x, norm_ffn_g[i], moe_router[i], moe_w_gate, moe_w_up, moe_w_down, layer=i)
    return x
```

```python
import functools

import jax
import jax.numpy as jnp
from jax import lax
from jax.experimental import pallas as pl
from jax.experimental.pallas import tpu as pltpu

F32 = jnp.float32
BF16 = jnp.bfloat16

NORM_EPS = 1e-6
GN_EPS = 64e-5
RWKV_HEAD = 64
QK_NOPE = 64
QK_ROPE = 32
V_HEAD = 64
Q_LORA = 384
KV_LORA = 256
ROPE_THETA = 10000.0
EC_CAPACITY = 2
LANES = 128
SUBLANES = 8
HEAD_PAD = LANES
ONES_ROWS = 16
LOG2_E = 1.4426950408889634
DECAY_SCALE = 0.6065306597126334
WKV_CHUNK = 64
VMEM_LIMIT = 56 * 1024 * 1024


class Tiles:
    rwkv_pre_rows = 256
    mla_pre_rows = 512
    row_tile = 1024
    attn_queries = 2048
    attn_key_chunk = 256
    wkv_batch_rows = 2
    scatter_experts = 4
    expert_batch_rows = 2

NT_DIMS = (((1,), (1,)), ((), ()))
TN_DIMS = (((0,), (0,)), ((), ()))


def _dot(a, b):
    return jnp.dot(a.astype(BF16), b.astype(BF16), preferred_element_type=F32)


def _dot_nt(a, b):
    return lax.dot_general(a.astype(BF16), b.astype(BF16), NT_DIMS, preferred_element_type=F32)


def _split_bf16(x):
    hi = x.astype(BF16)
    lo = (x - hi.astype(F32)).astype(BF16)
    return hi, lo


def _rms(x, gain):
    return x * lax.rsqrt(jnp.mean(x * x, axis=-1, keepdims=True) + NORM_EPS) * gain


def _seg_sum(t, seg):
    width = min(LANES, t.shape[1])
    ri = lax.broadcasted_iota(jnp.int32, (width, width), 0)
    ci = lax.broadcasted_iota(jnp.int32, (width, width), 1)
    ones = jnp.where((ri // seg) == (ci // seg), 1.0, 0.0).astype(BF16)
    parts = [_dot(t[:, j:j + width], ones) for j in range(0, t.shape[1], width)]
    return parts[0] if len(parts) == 1 else jnp.concatenate(parts, axis=1)


def _pad_to(w, axis, mult):
    pad = -w.shape[axis] % mult
    if pad == 0:
        return w
    widths = [(0, 0)] * w.ndim
    widths[axis] = (0, pad)
    return jnp.pad(w, widths)


def _const_spec(shape):
    return pl.BlockSpec(shape, lambda *_: (0,) * len(shape))


def _params(*sem):
    return pltpu.CompilerParams(dimension_semantics=sem, vmem_limit_bytes=VMEM_LIMIT)


def _rwkv_pre_kernel(*refs, has_vres, seg):
    it = iter(refs)
    (x_ref, xp_ref, xn_ref, vec_ref, wr_ref, wk_ref, wv_ref,
     w1_ref, a1_ref, g1_ref, w2_ref, a2_ref, g2_ref) = [next(it) for _ in range(13)]
    if has_vres:
        v1_ref, v2_ref, vf_ref = next(it), next(it), next(it)
    r_o, na_o, v_o, g_o, bon_o = [next(it) for _ in range(5)]
    lw_o = [next(it), next(it)]
    k_o = [next(it), next(it)]
    b_o = [next(it), next(it)]
    if not has_vres:
        vf_o = next(it)

    i = pl.program_id(1)
    vec = vec_ref[...]
    row_of = lambda j: vec[j:j + 1]
    gain = row_of(0)
    x = x_ref[0]
    bm = x.shape[0]
    h = _rms(x, gain)
    hp = jnp.where(i == 0, 0.0, _rms(xp_ref[0][SUBLANES - 1:SUBLANES], gain))
    hn = jnp.where(i == pl.num_programs(1) - 1, 0.0, _rms(xn_ref[0][0:1], gain))
    row = lax.broadcasted_iota(jnp.int32, (bm, 1), 0)
    h_prev = jnp.where(row == 0, hp, pltpu.roll(h, 1, 0))
    h_next = jnp.where(row == bm - 1, hn, pltpu.roll(h, bm - 1, 0))
    xx = 0.5 * (h_prev + h_next) - h
    xr, xw, xk, xv, xa, xg = [(h + xx * row_of(1 + j)).astype(BF16) for j in range(6)]

    r = _dot(xr, wr_ref[...])
    k = _dot(xk, wk_ref[...])
    v = _dot(xv, wv_ref[...])
    if has_vres:
        vz = row_of(14) + _dot(_dot(xv, v1_ref[...]), v2_ref[...])
        v = v + (vf_ref[0] - v) * jax.nn.sigmoid(vz)
    else:
        vf_o[0] = v
    g = _dot(jax.nn.sigmoid(_dot(xg, g1_ref[...])), g2_ref[...])

    kkr = k * row_of(11)
    kk = kkr * lax.rsqrt(jnp.maximum(_seg_sum(kkr * kkr, seg), 1e-24))
    tw = jnp.tanh(_dot(xw, w1_ref[...])).astype(BF16)
    al = _dot(xa, a1_ref[...]).astype(BF16)
    k_keep = k * (1.0 - row_of(12))
    k_mix = k * row_of(12)
    k_ds = []
    for d in range(2):
        z = row_of(7 + d) + _dot(tw, w2_ref[d])
        lw_o[d][0] = -DECAY_SCALE * jax.nn.sigmoid(z)
        a = jax.nn.sigmoid(row_of(9 + d) + _dot(al, a2_ref[d]))
        k_d = k_keep + k_mix * a
        k_o[d][0] = k_d.astype(BF16)
        b_o[d][0] = (kk * a).astype(BF16)
        k_ds.append(k_d)
    bon_o[0] = (_seg_sum(r * (k_ds[0] + k_ds[1]) * row_of(13), seg) * v).astype(BF16)
    r_o[0] = r.astype(BF16)
    na_o[0] = (-kk).astype(BF16)
    v_o[0] = v.astype(BF16)
    g_o[0] = g.astype(BF16)


def rwkv_pre(x, v_first, norm_g, mu, w_r, w_k, w_v, w0, w1, w2, a0, a1, a2, g1, g2, k_k, k_a, r_k,
             vres, bm=Tiles.rwkv_pre_rows):
    bsz, s, d = x.shape
    bm = min(bm, s)
    nt = s // bm
    has_vres = vres is not None
    zero = jnp.zeros((d,), F32)
    v0 = vres[0] if has_vres else zero
    vec = jnp.stack([norm_g, *mu, w0[0], w0[1], a0[0], a0[1], k_k, k_a, r_k.reshape(d), v0, zero])
    bf = lambda t: t.astype(BF16)
    cat2 = lambda t: jnp.concatenate([t[0], t[1]], axis=1)
    lo = w1.shape[2]
    second = lambda t: jnp.stack([jnp.pad(t[0], ((0, lo), (0, 0))), jnp.pad(t[1], ((lo, 0), (0, 0)))])
    ins = [x, x, x, vec, bf(w_r), bf(w_k), bf(w_v), bf(cat2(w1)), bf(cat2(a1)), bf(_pad_to(g1, 1, LANES)),
           bf(second(w2)), bf(second(a2)), bf(_pad_to(g2, 0, LANES))]
    tile = pl.BlockSpec((1, bm, d), lambda b, i: (b, i, 0))
    hb = bm // SUBLANES
    in_specs = [tile,
                pl.BlockSpec((1, SUBLANES, d), lambda b, i: (b, jnp.maximum(i * hb - 1, 0), 0)),
                pl.BlockSpec((1, SUBLANES, d), lambda b, i: (b, jnp.minimum((i + 1) * hb, s // SUBLANES - 1), 0))]
    in_specs += [_const_spec(t.shape) for t in ins[3:]]
    if has_vres:
        extra = [bf(_pad_to(vres[1], 1, LANES)), bf(_pad_to(vres[2], 0, LANES))]
        ins += extra + [v_first]
        in_specs += [_const_spec(t.shape) for t in extra] + [tile]
    n_bf, n_f32 = 5, 2
    out_shape = ([jax.ShapeDtypeStruct((bsz, s, d), BF16)] * n_bf + [jax.ShapeDtypeStruct((bsz, s, d), F32)] * n_f32
                 + [jax.ShapeDtypeStruct((bsz, s, d), BF16)] * 4)
    if not has_vres:
        out_shape.append(jax.ShapeDtypeStruct((bsz, s, d), F32))
    outs = pl.pallas_call(
        functools.partial(_rwkv_pre_kernel, has_vres=has_vres, seg=RWKV_HEAD),
        out_shape=tuple(out_shape),
        grid=(bsz, nt),
        in_specs=in_specs,
        out_specs=tuple([tile] * len(out_shape)),
        compiler_params=_params("parallel", "parallel"),
        name="rwkv_pre",
    )(*ins)
    r, na, v, g, bonus, lw0, lw1, k0, k1, b0, b1 = outs[:11]
    v_first = v_first if has_vres else outs[11]
    return r, na, v, g, bonus, (lw0, lw1), (k0, k1), (b0, b1), v_first


def _wkv_kernel(r_ref, lw_ref, k_ref, v_ref, a_ref, b_ref, y_ref, h_scr, *, reverse, nh, hd):
    c = pl.program_id(1)

    @pl.when(c == 0)
    def _():
        h_scr[...] = jnp.zeros_like(h_scr)

    nb, L, _ = lw_ref.shape
    row = lax.broadcasted_iota(jnp.int32, (L, L), 0)
    col = lax.broadcasted_iota(jnp.int32, (L, L), 1)
    incl = (col >= row) if reverse else (col <= row)
    tri = jnp.where(incl, 1.0, 0.0).astype(BF16)
    eye = jnp.where(row == col, 1.0, 0.0).astype(F32)
    last = 0 if reverse else L - 1

    def scaled(bi):
        lw = lw_ref[bi]
        lw_hi, lw_lo = _split_bf16(lw)
        cs = (jnp.dot(tri, lw_hi, preferred_element_type=F32)
              + jnp.dot(tri, lw_lo, preferred_element_type=F32))
        ctot = cs[last:last + 1, :]
        r, k, v, a, b = [t[bi].astype(F32) for t in (r_ref, k_ref, v_ref, a_ref, b_ref)]
        einv = jnp.exp(-cs)
        etot = jnp.exp(ctot - cs)
        return dict(rt=r * jnp.exp(cs), at=a * jnp.exp(cs - lw), bt=b * einv, kt=k * einv,
                    bb=b * etot, kb=k * etot, v=v, wtot=jnp.exp(ctot))

    sc_in = [scaled(bi) for bi in range(nb)]

    row2 = lax.broadcasted_iota(jnp.int32, (2 * L, 2 * L), 0)
    col2 = lax.broadcasted_iota(jnp.int32, (2 * L, 2 * L), 1)
    rt2, cs2 = row2 & (L - 1), col2 & (L - 1)
    before = (cs2 > rt2) if reverse else (cs2 < rt2)
    score_mask = before | ((row2 >= L) & (cs2 == rt2))
    right_half = lax.broadcasted_iota(jnp.int32, (L, 2 * L), 1) >= L
    diag_blk = (row ^ col) < 2
    level_masks = []
    size = 2
    while size < L:
        x = row ^ col
        level_masks.append((x >= size) & (x < 2 * size))
        size *= 2

    T = 2 * hd
    lane_lo = lax.broadcasted_iota(jnp.int32, (1, T), 1) < hd
    rowp = lax.broadcasted_iota(jnp.int32, (T, T), 0)
    colp = lax.broadcasted_iota(jnp.int32, (T, T), 1)
    diag_blocks = (rowp < hd) == (colp < hd)
    eye_t = jnp.where(rowp == colp, 1.0, 0.0).astype(F32)
    tiles = [(bi, p) for bi in range(nb) for p in range(nh // 2)]
    units = range(len(tiles))
    heads = [(u, q) for u in units for q in range(2)]
    tile = lambda name, u: sc_in[tiles[u][0]][name][:, tiles[u][1] * T:(tiles[u][1] + 1) * T]
    own = lambda q: lane_lo if q == 0 else jnp.logical_not(lane_lo)

    a2 = [jnp.concatenate([tile("at", u), tile("rt", u)], 0) for u in units]
    b2 = [jnp.concatenate([tile("bt", u), tile("kt", u)], 0).astype(BF16) for u in units]
    v_sw = [pltpu.roll(tile("v", u), hd, 1) for u in units]
    vv_sw = [jnp.concatenate([t_.astype(BF16)] * 2, 0) for t_ in v_sw]
    bbkb_t = [jnp.concatenate([tile("bb", u), tile("kb", u)], 0).T.astype(BF16) for u in units]

    sc = [jnp.where(score_mask, _dot_nt(jnp.where(own(q), a2[u], 0.0), b2[u]), 0.0) for u, q in heads]
    top = [s_[:L] for s_ in sc]
    bot = [s_[L:] for s_ in sc]
    n_ab = [t_[:, :L] for t_ in top]
    t = [eye + jnp.where(diag_blk, n_, 0.0) for n_ in n_ab]
    for lm in level_masks:
        w = [_dot(jnp.where(lm, n_, 0.0), t_) for n_, t_ in zip(n_ab, t)]
        t = [t_ + _dot(t_, w_) for t_, w_ in zip(t, w)]
    x = [_dot(jnp.where(right_half, top[i], 0.0), vv_sw[u]) for i, (u, q) in enumerate(heads)]
    tz = [_dot(t[i], jnp.where(own(q), tile("at", u), x[i])) for i, (u, q) in enumerate(heads)]
    rhs = [jnp.concatenate([tz[i].astype(BF16), jnp.where(own(q), 0.0, v_sw[u]).astype(BF16)], 0)
           for i, (u, q) in enumerate(heads)]
    qp = [_dot(jnp.concatenate([bot[i].astype(BF16), bbkb_t[u][q * hd:(q + 1) * hd]], 0), rhs[i])
          for i, (u, q) in enumerate(heads)]
    for u in units:
        bi, p = tiles[u]
        q0, q1 = qp[2 * u], qp[2 * u + 1]
        rh = tile("rt", u) + jnp.where(lane_lo, q0[:L], q1[:L])
        yh = pltpu.roll(jnp.where(lane_lo, q1[:L], q0[:L]), hd, 1)
        prow = jnp.concatenate([q0[L:], q1[L:]], 0)
        m = jnp.where(diag_blocks, prow, 0.0) + eye_t * tile("wtot", u)
        g = pltpu.roll(jnp.where(diag_blocks, 0.0, prow), hd, 1)
        fin = _dot(jnp.concatenate([rh, m], 0), h_scr[bi, p])
        y_ref[bi, :, p * T:(p + 1) * T] = (fin[:L] + yh).astype(y_ref.dtype)
        h_scr[bi, p] = fin[L:] + g


def wkv7(r, lw, k, v, a, b, reverse, nb=Tiles.wkv_batch_rows):
    bsz, s, d = r.shape
    hd = RWKV_HEAD
    nh = d // hd
    L = min(WKV_CHUNK, s)
    nb = min(nb, bsz)
    assert L == hd and 2 * hd == LANES and nh % 2 == 0 and s % L == 0 and bsz % nb == 0
    nc = s // L
    if reverse:
        idx = lambda bi, ci: (bi, nc - 1 - ci, 0)
    else:
        idx = lambda bi, ci: (bi, ci, 0)
    spec = pl.BlockSpec((nb, L, d), idx)
    return pl.pallas_call(
        functools.partial(_wkv_kernel, reverse=reverse, nh=nh, hd=hd),
        out_shape=jax.ShapeDtypeStruct((bsz, s, d), BF16),
        grid=(bsz // nb, nc),
        in_specs=[spec] * 6,
        out_specs=spec,
        scratch_shapes=[pltpu.VMEM((nb, nh // 2, 2 * hd, 2 * hd), F32)],
        compiler_params=_params("parallel", "arbitrary"),
        name="wkv7_rev" if reverse else "wkv7_fwd",
    )(r, lw, k, v, a, b)


def _rwkv_post_kernel(yf_ref, yb_ref, bon_ref, g_ref, x_ref, vec_ref, wo_ref, o_ref, *, seg):
    y = yf_ref[...].astype(F32) + yb_ref[...].astype(F32)
    yc = y - _seg_sum(y, seg) * (1.0 / seg)
    var = _seg_sum(yc * yc, seg) * (1.0 / seg)
    vec = vec_ref[...]
    yn = yc * lax.rsqrt(var + GN_EPS) * vec[0:1] + vec[1:2] + bon_ref[...].astype(F32)
    o_ref[...] = x_ref[...] + _dot(yn * g_ref[...].astype(F32), wo_ref[...])


def rwkv_post(yf, yb, bonus, g, x, gn_g, gn_b, w_o, bm=Tiles.row_tile):
    m, d = x.shape
    bm = min(bm, m)
    vec = _pad_to(jnp.stack([gn_g, gn_b]), 0, SUBLANES)
    tile = pl.BlockSpec((bm, d), lambda i: (i, 0))
    return pl.pallas_call(
        functools.partial(_rwkv_post_kernel, seg=RWKV_HEAD),
        out_shape=jax.ShapeDtypeStruct((m, d), F32),
        grid=(m // bm,),
        in_specs=[tile] * 5 + [_const_spec(vec.shape), _const_spec(w_o.shape)],
        out_specs=tile,
        compiler_params=_params("parallel"),
        name="rwkv_post",
    )(yf, yb, bonus, g, x, vec, w_o.astype(BF16))


def _head_norm_rope(y2, ctab, stab, o_ref, nh):
    inv_dim = 1.0 / (QK_NOPE + QK_ROPE)
    for h in range(nh):
        sl = slice(h * HEAD_PAD, (h + 1) * HEAD_PAD)
        rh = slice((nh + h) * HEAD_PAD, (nh + h + 1) * HEAD_PAD)
        yh = y2[:, sl]
        ms = jnp.sum(yh * yh, axis=-1, keepdims=True) * inv_dim
        o_ref[0, :, sl] = ((yh * ctab + y2[:, rh] * stab) * lax.rsqrt(ms + NORM_EPS)).astype(o_ref.dtype)


def _mla_pre_kernel(x_ref, gx_ref, gc_ref, hg_ref, cf_ref, sf_ref, win_ref, wq_ref, wk_ref, wvt_ref,
                    q_ref, k_ref, vt_ref, *, nh):
    gc = gc_ref[...]
    h = _rms(x_ref[0], gx_ref[...])
    c = _dot(h, win_ref[...])
    c_q = _rms(c[:, :Q_LORA], gc[0:1])
    c_kv = _rms(c[:, Q_LORA:Q_LORA + KV_LORA], gc[1:2, :KV_LORA]).astype(BF16)
    cf, sf = cf_ref[...], sf_ref[...]
    hg = hg_ref[...]
    _head_norm_rope(_dot(c_q, wq_ref[...]), cf * hg[0:1], sf * hg[1:2], q_ref, nh)
    k_in = jnp.concatenate([c_kv, c[:, Q_LORA + KV_LORA:].astype(BF16)], axis=1)
    _head_norm_rope(_dot(k_in, wk_ref[...]), cf * hg[2:3], sf * hg[3:4], k_ref, nh)
    vt_ref[0] = _dot_nt(wvt_ref[...], c_kv).astype(vt_ref.dtype)


def _pad_heads(w, nh, width, take):
    k = w.shape[0]
    wh = w.reshape(k, nh, width)[:, :, take]
    wh = jnp.pad(wh, ((0, 0), (0, 0), (0, HEAD_PAD - wh.shape[-1])))
    return wh.reshape(k, nh * HEAD_PAD)


def _with_rotate_half(w_pad, nh):
    k = w_pad.shape[0]
    half = QK_ROPE // 2
    w3 = w_pad.reshape(k, nh, HEAD_PAD)
    x1 = w3[:, :, QK_NOPE:QK_NOPE + half]
    x2 = w3[:, :, QK_NOPE + half:QK_NOPE + QK_ROPE]
    z = lambda n: jnp.zeros((k, nh, n), w_pad.dtype)
    rh = jnp.concatenate([z(QK_NOPE), -x2, x1, z(HEAD_PAD - QK_NOPE - QK_ROPE)], axis=-1)
    return jnp.concatenate([w_pad, rh.reshape(k, nh * HEAD_PAD)], axis=1)


def _gain_rows(g):
    half = QK_ROPE // 2
    tail = jnp.zeros((HEAD_PAD - QK_NOPE - QK_ROPE,), F32)
    straight = jnp.concatenate([g, tail])
    swapped = jnp.concatenate([jnp.zeros((QK_NOPE,), F32), g[QK_NOPE + half:], g[QK_NOPE:QK_NOPE + half], tail])
    return [straight, swapped]


def mla_pre(x, tables, norm_g, w_in, q_norm_g, kv_norm_g, w_uq, w_ukv, q_head_g, k_head_g,
            bm=Tiles.mla_pre_rows):
    bsz, s, d = x.shape
    bm = min(bm, s)
    qk = QK_NOPE + QK_ROPE
    nh = w_uq.shape[1] // qk
    assert Q_LORA % LANES == 0 and KV_LORA % LANES == 0 and KV_LORA <= Q_LORA
    gx = norm_g.reshape(1, d)
    gc = _pad_to(jnp.stack([q_norm_g, jnp.pad(kv_norm_g, (0, Q_LORA - KV_LORA))]), 0, SUBLANES)
    hg = _pad_to(jnp.stack(_gain_rows(q_head_g * (qk ** -0.5 * LOG2_E)) + _gain_rows(k_head_g)), 0, SUBLANES)
    win = _pad_to(w_in, 1, LANES)
    rope_w = win.shape[1] - Q_LORA - KV_LORA
    wq = _with_rotate_half(_pad_heads(w_uq, nh, qk, slice(0, qk)), nh)
    wk = _pad_heads(w_ukv, nh, QK_NOPE + V_HEAD, slice(0, QK_NOPE))
    place = jnp.zeros((rope_w, nh, HEAD_PAD), F32)
    place = place.at[jnp.arange(QK_ROPE), :, QK_NOPE + jnp.arange(QK_ROPE)].set(1.0)
    wk = _with_rotate_half(jnp.concatenate([wk, place.reshape(rope_w, nh * HEAD_PAD)], 0), nh)
    wvt = w_ukv.reshape(KV_LORA, nh, QK_NOPE + V_HEAD)[:, :, QK_NOPE:].reshape(KV_LORA, nh * V_HEAD).T
    weights = [t.astype(BF16) for t in (win, wq, wk, wvt)]
    nt = s // bm
    tab = pl.BlockSpec((bm, HEAD_PAD), lambda b, i: (b * nt + i, 0))
    row = lambda n: pl.BlockSpec((1, bm, n), lambda b, i: (b, i, 0))
    return pl.pallas_call(
        functools.partial(_mla_pre_kernel, nh=nh),
        out_shape=(jax.ShapeDtypeStruct((bsz, s, nh * HEAD_PAD), BF16),
                   jax.ShapeDtypeStruct((bsz, s, nh * HEAD_PAD), BF16),
                   jax.ShapeDtypeStruct((bsz, nh * V_HEAD, s), BF16)),
        grid=(bsz, nt),
        in_specs=[row(d), _const_spec(gx.shape), _const_spec(gc.shape), _const_spec(hg.shape), tab, tab]
                 + [_const_spec(t.shape) for t in weights],
        out_specs=(row(nh * HEAD_PAD), row(nh * HEAD_PAD),
                   pl.BlockSpec((1, nh * V_HEAD, bm), lambda b, i: (b, 0, i))),
        compiler_params=_params("parallel", "parallel"),
        name="mla_pre",
    )(x, gx, gc, hg, *tables, *weights)


def _attn_kernel(q_ref, k_ref, vt_ref, o_ref, *, kc):
    s = k_ref.shape[1]
    heads = range(2)
    qs = [q_ref[0, :, j * HEAD_PAD:(j + 1) * HEAD_PAD] for j in heads]

    def scores(c):
        return [_dot_nt(k_ref[0, c * kc:(c + 1) * kc, j * HEAD_PAD:(j + 1) * HEAD_PAD], qs[j]) for j in heads]

    ones = jnp.ones((ONES_ROWS, kc), BF16)
    nxt = scores(0)
    m, acc = [None] * 2, [None] * 2
    for c in range(s // kc):
        cur = nxt
        if (c + 1) * kc < s:
            nxt = scores(c + 1)
        for j in heads:
            vt = jnp.concatenate([vt_ref[0, j * V_HEAD:(j + 1) * V_HEAD, c * kc:(c + 1) * kc], ones], axis=0)
            mc = jnp.max(cur[j], axis=0, keepdims=True)
            if c == 0:
                m[j] = mc
                acc[j] = jnp.dot(vt, jnp.exp2((cur[j] - mc).astype(BF16)), preferred_element_type=F32)
            else:
                m_new = jnp.maximum(m[j], mc)
                pt = jnp.exp2((cur[j] - m_new).astype(BF16))
                acc[j] = jnp.exp2(m[j] - m_new) * acc[j] + jnp.dot(vt, pt, preferred_element_type=F32)
                m[j] = m_new
    out = jnp.concatenate([acc[j][:V_HEAD] / acc[j][V_HEAD:V_HEAD + 1] for j in heads], axis=0)
    o_ref[0] = out.T.astype(o_ref.dtype)


def attention(q, k, vt, tq=Tiles.attn_queries, kc=Tiles.attn_key_chunk):
    bsz, s, _ = q.shape
    nh = vt.shape[1] // V_HEAD
    tq = min(tq, s)
    kc = min(kc, s)
    return pl.pallas_call(
        functools.partial(_attn_kernel, kc=kc),
        out_shape=jax.ShapeDtypeStruct((bsz, s, nh * V_HEAD), BF16),
        grid=(bsz, nh // 2, s // tq),
        in_specs=[pl.BlockSpec((1, tq, 2 * HEAD_PAD), lambda b, h, i: (b, i, h)),
                  pl.BlockSpec((1, s, 2 * HEAD_PAD), lambda b, h, i: (b, 0, h)),
                  pl.BlockSpec((1, 2 * V_HEAD, s), lambda b, h, i: (b, h, 0))],
        out_specs=pl.BlockSpec((1, tq, 2 * V_HEAD), lambda b, h, i: (b, i, h)),
        compiler_params=_params("parallel", "parallel", "parallel"),
        name="mla_attention",
    )(q, k, vt)


def _proj_res_kernel(a_ref, w_ref, x_ref, o_ref):
    o_ref[...] = x_ref[...] + _dot(a_ref[...], w_ref[...])


def proj_residual(a, w, x, bm=Tiles.row_tile):
    m, k = a.shape
    n = w.shape[1]
    bm = min(bm, m)
    return pl.pallas_call(
        _proj_res_kernel,
        out_shape=jax.ShapeDtypeStruct((m, n), F32),
        grid=(m // bm,),
        in_specs=[pl.BlockSpec((bm, k), lambda i: (i, 0)), _const_spec(w.shape),
                  pl.BlockSpec((bm, n), lambda i: (i, 0))],
        out_specs=pl.BlockSpec((bm, n), lambda i: (i, 0)),
        compiler_params=_params("parallel"),
        name="proj_residual",
    )(a, w.astype(BF16), x)


def _affinity_kernel(x_ref, g_ref, wr_ref, gate_ref, hn_ref):
    h = _rms(x_ref[0], g_ref[...])
    hn_ref[0] = h.astype(hn_ref.dtype)
    h_hi, h_lo = _split_bf16(h)
    w_hi, w_lo = _split_bf16(wr_ref[...])
    logits = (lax.dot_general(w_hi, h_hi, NT_DIMS, preferred_element_type=F32)
              + lax.dot_general(w_hi, h_lo, NT_DIMS, preferred_element_type=F32)
              + lax.dot_general(w_lo, h_hi, NT_DIMS, preferred_element_type=F32))
    ex = jnp.exp(logits - jnp.max(logits, axis=0, keepdims=True))
    gate_ref[0] = ex / jnp.sum(ex, axis=0, keepdims=True)


def _select_kernel(aff_ref, pos_ref, *, cap):
    aff = aff_ref[...]
    ne, s = aff.shape
    bits = pltpu.bitcast(aff, jnp.int32)

    def count(mask):
        return jnp.sum(jnp.where(mask, 1.0, 0.0), axis=1, keepdims=True)

    def search(nbits, accept):
        value = jnp.zeros((ne, 1), jnp.int32)
        shift = nbits
        while shift > 0:
            width = 2 if shift % 2 == 0 else 1
            shift -= width
            digit = jnp.zeros((ne, 1), jnp.int32)
            for j in range(1, 1 << width):
                digit = digit + jnp.where(accept(value + (j << shift)), 1, 0)
            value = value + jnp.left_shift(digit, shift)
        return value

    thr = search(31, lambda cand: count(bits >= cand) >= cap)
    gt = bits > thr
    eq = bits == thr
    need = cap - count(gt)
    idx = lax.broadcasted_iota(jnp.int32, (ne, s), 1)
    cut = search(s.bit_length(), lambda cand: count(eq & (idx < cand)) <= need)
    sel = gt | (eq & (idx < cut))

    blk = LANES if s % LANES == 0 else s
    ri = lax.broadcasted_iota(jnp.int32, (blk, blk), 0)
    ci = lax.broadcasted_iota(jnp.int32, (blk, blk), 1)
    upper = jnp.where(ri <= ci, 1.0, 0.0).astype(BF16)
    off = jnp.zeros((ne, 1), F32)
    for j in range(s // blk):
        sl = slice(j * blk, (j + 1) * blk)
        sel_j = sel[:, sl]
        inc = jnp.dot(jnp.where(sel_j, 1.0, 0.0).astype(BF16), upper, preferred_element_type=F32)
        pos_ref[:, sl] = jnp.where(sel_j, (off + inc - 1.0).astype(jnp.int32), -1)
        off = off + inc[:, blk - 1:blk]


def route(x, norm_g, w_router, cap):
    bsz, s, d = x.shape
    ne = w_router.shape[1]
    out = pl.BlockSpec((1, ne, s), lambda b: (b, 0, 0))
    tok = pl.BlockSpec((1, s, d), lambda b: (b, 0, 0))
    gate, hn = pl.pallas_call(
        _affinity_kernel,
        out_shape=(jax.ShapeDtypeStruct((bsz, ne, s), F32), jax.ShapeDtypeStruct((bsz, s, d), BF16)),
        grid=(bsz,),
        in_specs=[tok, _const_spec((1, d)), _const_spec((ne, d))],
        out_specs=(out, tok),
        compiler_params=_params("parallel"),
        name="ec_affinity",
    )(x, norm_g.reshape(1, d), w_router.T)
    pos = pl.pallas_call(
        functools.partial(_select_kernel, cap=cap),
        out_shape=jax.ShapeDtypeStruct((bsz * ne, s), jnp.int32),
        grid=(1,),
        in_specs=[_const_spec((bsz * ne, s))],
        out_specs=_const_spec((bsz * ne, s)),
        compiler_params=_params("arbitrary"),
        name="ec_select",
    )(gate.reshape(bsz * ne, s))
    return pos.reshape(bsz, ne, s), gate, hn


def _selection(pos_row, cap):
    return pos_row == lax.broadcasted_iota(jnp.int32, (cap, pos_row.shape[1]), 0)


def _expert_ffn_kernel(h_ref, pos_ref, gate_ref, wg_ref, wu_ref, wd_ref, ye_ref, wg_s, wu_s, wd_s, *, cap):
    @pl.when(pl.program_id(1) == 0)
    def _():
        wg_s[...] = wg_ref[0, 0].astype(BF16)
        wu_s[...] = wu_ref[0, 0].astype(BF16)
        wd_s[...] = wd_ref[0, 0].astype(BF16)

    rows = range(h_ref.shape[0])
    onehot = [_selection(pos_ref[i, 0], cap) for i in rows]
    sel = [jnp.where(o, 1.0, 0.0).astype(BF16) for o in onehot]
    gcol = [jnp.sum(jnp.where(onehot[i], gate_ref[i, 0], 0.0), axis=1, keepdims=True) for i in rows]
    xe = [jnp.dot(sel[i], h_ref[i], preferred_element_type=F32).astype(BF16) for i in rows]
    hg = [jnp.dot(x_, wg_s[...], preferred_element_type=F32) for x_ in xe]
    hu = [jnp.dot(x_, wu_s[...], preferred_element_type=F32) for x_ in xe]
    hid = [(g_ * jax.nn.sigmoid(g_) * u_).astype(BF16) for g_, u_ in zip(hg, hu)]
    for i in rows:
        ye_ref[i, 0] = (jnp.dot(hid[i], wd_s[...], preferred_element_type=F32) * gcol[i]).astype(ye_ref.dtype)


def _expert_scatter_kernel(x_ref, pos_ref, ye_ref, o_ref, *, cap, eg):
    @pl.when(pl.program_id(1) == 0)
    def _():
        o_ref[...] = x_ref[...]

    sel = jnp.concatenate(
        [jnp.where(_selection(pos_ref[0, g], cap), 1.0, 0.0).astype(BF16) for g in range(eg)], axis=0)
    ye = ye_ref[0].reshape(eg * cap, ye_ref.shape[-1])
    o_ref[0] += lax.dot_general(sel, ye, TN_DIMS, preferred_element_type=F32)


def expert_choice_ffn(x, norm_g, w_router, w_gate, w_up, w_down, layer, eg=Tiles.scatter_experts):
    bsz, s, d = x.shape
    _, ne, _, ff = w_gate.shape
    cap = EC_CAPACITY * s // ne
    eg = min(eg, ne)
    pos, gate, hn = route(x, norm_g, w_router, cap)
    pos4 = pos.reshape(bsz, ne, 1, s)
    gate4 = gate.reshape(bsz, ne, 1, s)
    nb = min(Tiles.expert_batch_rows, bsz)
    assert bsz % nb == 0
    row = pl.BlockSpec((nb, 1, 1, s), lambda e, b: (b, e, 0, 0))
    wspec = lambda shape: pl.BlockSpec((1, 1) + shape, lambda e, b: (layer, e, 0, 0))
    ye = pl.pallas_call(
        functools.partial(_expert_ffn_kernel, cap=cap),
        out_shape=jax.ShapeDtypeStruct((bsz, ne, cap, d), BF16),
        grid=(ne, bsz // nb),
        in_specs=[pl.BlockSpec((nb, s, d), lambda e, b: (b, 0, 0)), row, row,
                  wspec((d, ff)), wspec((d, ff)), wspec((ff, d))],
        out_specs=pl.BlockSpec((nb, 1, cap, d), lambda e, b: (b, e, 0, 0)),
        scratch_shapes=[pltpu.VMEM((d, ff), BF16), pltpu.VMEM((d, ff), BF16), pltpu.VMEM((ff, d), BF16)],
        compiler_params=_params("arbitrary", "arbitrary"),
        name="ec_expert_ffn",
    )(hn, pos4, gate4, w_gate, w_up, w_down)
    tok = pl.BlockSpec((1, s, d), lambda b, j: (b, 0, 0))
    return pl.pallas_call(
        functools.partial(_expert_scatter_kernel, cap=cap, eg=eg),
        out_shape=jax.ShapeDtypeStruct((bsz, s, d), F32),
        grid=(bsz, ne // eg),
        in_specs=[tok, pl.BlockSpec((1, eg, 1, s), lambda b, j: (b, j, 0, 0)),
                  pl.BlockSpec((1, eg, cap, d), lambda b, j: (b, j, 0, 0))],
        out_specs=tok,
        compiler_params=_params("parallel", "arbitrary"),
        name="ec_expert_scatter",
    )(x, pos4, ye)


def rwkv7_layer(x, v_first, norm_g, mu, w_r, w_k, w_v, w_o, w0, w1, w2, a0, a1, a2,
                g1, g2, k_k, k_a, r_k, gn_g, gn_b, vres):
    bsz, s, d = x.shape
    r, na, v, g, bonus, lw, k_d, b_d, v_first = rwkv_pre(
        x, v_first, norm_g, mu, w_r, w_k, w_v, w0, w1, w2, a0, a1, a2, g1, g2, k_k, k_a, r_k, vres)
    yf = wkv7(r, lw[0], k_d[0], v, na, b_d[0], reverse=False)
    yb = wkv7(r, lw[1], k_d[1], v, na, b_d[1], reverse=True)
    flat = lambda t: t.reshape(bsz * s, d)
    out = rwkv_post(flat(yf), flat(yb), flat(bonus), flat(g), flat(x), gn_g, gn_b, w_o)
    return out.reshape(bsz, s, d), v_first


def rope_tables(positions):
    inv_freq = ROPE_THETA ** (-jnp.arange(0, QK_ROPE, 2, dtype=F32) / QK_ROPE)
    ang = positions.astype(F32)[..., None] * inv_freq
    cos, sin = jnp.cos(ang), jnp.sin(ang)
    shape = cos.shape[:-1]
    z = lambda n: jnp.zeros(shape + (n,), F32)
    tail = HEAD_PAD - QK_NOPE - QK_ROPE
    cf = jnp.concatenate([jnp.ones(shape + (QK_NOPE,), F32), cos, cos, z(tail)], -1)
    sf = jnp.concatenate([z(QK_NOPE), sin, sin, z(tail)], -1)
    return cf.reshape(-1, HEAD_PAD), sf.reshape(-1, HEAD_PAD)


def mla_layer(x, tables, norm_g, w_in, q_norm_g, kv_norm_g, w_uq, w_ukv, q_head_g, k_head_g, w_o):
    bsz, s, d = x.shape
    q, k, vt = mla_pre(x, tables, norm_g, w_in, q_norm_g, kv_norm_g, w_uq, w_ukv, q_head_g, k_head_g)
    o = attention(q, k, vt)
    return proj_residual(o.reshape(bsz * s, -1), w_o, x.reshape(bsz * s, d)).reshape(bsz, s, d)


def kernel(x, positions, norm_mix_g, norm_ffn_g,
           rw_mu, rw_wr, rw_wk, rw_wv, rw_wo, rw_w0, rw_w1, rw_w2,
           rw_a0, rw_a1, rw_a2, rw_g1, rw_g2, rw_kk, rw_ka, rw_rk,
           rw_gn_g, rw_gn_b, rw_v0, rw_v1, rw_v2,
           mla_w_in, mla_q_norm_g, mla_kv_norm_g, mla_w_uq, mla_w_ukv,
           mla_q_head_g, mla_k_head_g, mla_w_o,
           moe_router, moe_w_gate, moe_w_up, moe_w_down):
    depth = norm_mix_g.shape[0]
    n_mixers = 2
    tables = rope_tables(positions)
    v_first = None
    for i in range(depth):
        j = i // n_mixers
        if i % n_mixers == 0:
            vres = None if j == 0 else (rw_v0[j - 1], rw_v1[j - 1], rw_v2[j - 1])
            x, v_first = rwkv7_layer(
                x, v_first, norm_mix_g[i], rw_mu[j], rw_wr[j], rw_wk[j], rw_wv[j], rw_wo[j],
                rw_w0[j], rw_w1[j], rw_w2[j], rw_a0[j], rw_a1[j], rw_a2[j],
                rw_g1[j], rw_g2[j], rw_kk[j], rw_ka[j], rw_rk[j],
                rw_gn_g[j], rw_gn_b[j], vres)
        else:
            x = mla_layer(x, tables, norm_mix_g[i], mla_w_in[j], mla_q_norm_g[j], mla_kv_norm_g[j],
                          mla_w_uq[j], mla_w_ukv[j], mla_q_head_g[j], mla_k_head_g[j], mla_w_o[j])
        x = expert_choice_ffn(x, norm_ffn_g[i], moe_router[i], moe_w_gate, moe_w_up, moe_w_down, layer=i)
    return x
```

```python
import functools

import jax
import jax.numpy as jnp
from jax import lax
from jax.experimental import pallas as pl
from jax.experimental.pallas import tpu as pltpu

F32 = jnp.float32
BF16 = jnp.bfloat16

NORM_EPS = 1e-6
GN_EPS = 64e-5
RWKV_HEAD = 64
QK_NOPE = 64
QK_ROPE = 32
V_HEAD = 64
Q_LORA = 384
KV_LORA = 256
ROPE_THETA = 10000.0
EC_CAPACITY = 2
LANES = 128
SUBLANES = 8
HEAD_PAD = LANES
ONES_ROWS = 16
LOG2_E = 1.4426950408889634
DECAY_SCALE = 0.6065306597126334
WKV_CHUNK = 64
VMEM_LIMIT = 56 * 1024 * 1024


class Tiles:
    rwkv_pre_rows = 256
    mla_pre_rows = 512
    row_tile = 1024
    attn_queries = 2048
    attn_key_chunk = 256
    attn_heads = 4
    wkv_batch_rows = 2
    scatter_experts = 4
    expert_batch_rows = 2

NT_DIMS = (((1,), (1,)), ((), ()))
TN_DIMS = (((0,), (0,)), ((), ()))


def _dot(a, b):
    return jnp.dot(a.astype(BF16), b.astype(BF16), preferred_element_type=F32)


def _dot_nt(a, b):
    return lax.dot_general(a.astype(BF16), b.astype(BF16), NT_DIMS, preferred_element_type=F32)


def _split_bf16(x):
    hi = x.astype(BF16)
    lo = (x - hi.astype(F32)).astype(BF16)
    return hi, lo


def _rms(x, gain):
    return x * lax.rsqrt(jnp.mean(x * x, axis=-1, keepdims=True) + NORM_EPS) * gain


def _seg_sum(t, seg):
    width = min(LANES, t.shape[1])
    ri = lax.broadcasted_iota(jnp.int32, (width, width), 0)
    ci = lax.broadcasted_iota(jnp.int32, (width, width), 1)
    ones = jnp.where((ri // seg) == (ci // seg), 1.0, 0.0).astype(BF16)
    parts = [_dot(t[:, j:j + width], ones) for j in range(0, t.shape[1], width)]
    return parts[0] if len(parts) == 1 else jnp.concatenate(parts, axis=1)


def _pad_to(w, axis, mult):
    pad = -w.shape[axis] % mult
    if pad == 0:
        return w
    widths = [(0, 0)] * w.ndim
    widths[axis] = (0, pad)
    return jnp.pad(w, widths)


def _const_spec(shape):
    return pl.BlockSpec(shape, lambda *_: (0,) * len(shape))


def _params(*sem):
    return pltpu.CompilerParams(dimension_semantics=sem, vmem_limit_bytes=VMEM_LIMIT)


def _rwkv_pre_kernel(*refs, has_vres, seg):
    it = iter(refs)
    (x_ref, xp_ref, xn_ref, vec_ref, wr_ref, wk_ref, wv_ref,
     w1_ref, a1_ref, g1_ref, w2_ref, a2_ref, g2_ref) = [next(it) for _ in range(13)]
    if has_vres:
        v1_ref, v2_ref, vf_ref = next(it), next(it), next(it)
    r_o, na_o, v_o, g_o, bon_o = [next(it) for _ in range(5)]
    lw_o = [next(it), next(it)]
    k_o = [next(it), next(it)]
    b_o = [next(it), next(it)]
    if not has_vres:
        vf_o = next(it)

    i = pl.program_id(1)
    vec = vec_ref[...]
    row_of = lambda j: vec[j:j + 1]
    gain = row_of(0)
    x = x_ref[0]
    bm = x.shape[0]
    h = _rms(x, gain)
    hp = jnp.where(i == 0, 0.0, _rms(xp_ref[0][SUBLANES - 1:SUBLANES], gain))
    hn = jnp.where(i == pl.num_programs(1) - 1, 0.0, _rms(xn_ref[0][0:1], gain))
    row = lax.broadcasted_iota(jnp.int32, (bm, 1), 0)
    h_prev = jnp.where(row == 0, hp, pltpu.roll(h, 1, 0))
    h_next = jnp.where(row == bm - 1, hn, pltpu.roll(h, bm - 1, 0))
    xx = 0.5 * (h_prev + h_next) - h
    xr, xw, xk, xv, xa, xg = [(h + xx * row_of(1 + j)).astype(BF16) for j in range(6)]

    r = _dot(xr, wr_ref[...])
    k = _dot(xk, wk_ref[...])
    v = _dot(xv, wv_ref[...])
    if has_vres:
        vz = row_of(14) + _dot(_dot(xv, v1_ref[...]), v2_ref[...])
        v = v + (vf_ref[0] - v) * jax.nn.sigmoid(vz)
    else:
        vf_o[0] = v
    g = _dot(jax.nn.sigmoid(_dot(xg, g1_ref[...])), g2_ref[...])

    kkr = k * row_of(11)
    kk = kkr * lax.rsqrt(jnp.maximum(_seg_sum(kkr * kkr, seg), 1e-24))
    tw = jnp.tanh(_dot(xw, w1_ref[...])).astype(BF16)
    al = _dot(xa, a1_ref[...]).astype(BF16)
    k_keep = k * (1.0 - row_of(12))
    k_mix = k * row_of(12)
    k_ds = []
    for d in range(2):
        z = row_of(7 + d) + _dot(tw, w2_ref[d])
        lw_o[d][0] = -DECAY_SCALE * jax.nn.sigmoid(z)
        a = jax.nn.sigmoid(row_of(9 + d) + _dot(al, a2_ref[d]))
        k_d = k_keep + k_mix * a
        k_o[d][0] = k_d.astype(BF16)
        b_o[d][0] = (kk * a).astype(BF16)
        k_ds.append(k_d)
    bon_o[0] = (_seg_sum(r * (k_ds[0] + k_ds[1]) * row_of(13), seg) * v).astype(BF16)
    r_o[0] = r.astype(BF16)
    na_o[0] = (-kk).astype(BF16)
    v_o[0] = v.astype(BF16)
    g_o[0] = g.astype(BF16)


def rwkv_pre(x, v_first, norm_g, mu, w_r, w_k, w_v, w0, w1, w2, a0, a1, a2, g1, g2, k_k, k_a, r_k,
             vres, bm=Tiles.rwkv_pre_rows):
    bsz, s, d = x.shape
    bm = min(bm, s)
    nt = s // bm
    has_vres = vres is not None
    zero = jnp.zeros((d,), F32)
    v0 = vres[0] if has_vres else zero
    vec = jnp.stack([norm_g, *mu, w0[0], w0[1], a0[0], a0[1], k_k, k_a, r_k.reshape(d), v0, zero])
    bf = lambda t: t.astype(BF16)
    cat2 = lambda t: jnp.concatenate([t[0], t[1]], axis=1)
    lo = w1.shape[2]
    second = lambda t: jnp.stack([jnp.pad(t[0], ((0, lo), (0, 0))), jnp.pad(t[1], ((lo, 0), (0, 0)))])
    ins = [x, x, x, vec, bf(w_r), bf(w_k), bf(w_v), bf(cat2(w1)), bf(cat2(a1)), bf(_pad_to(g1, 1, LANES)),
           bf(second(w2)), bf(second(a2)), bf(_pad_to(g2, 0, LANES))]
    tile = pl.BlockSpec((1, bm, d), lambda b, i: (b, i, 0))
    hb = bm // SUBLANES
    in_specs = [tile,
                pl.BlockSpec((1, SUBLANES, d), lambda b, i: (b, jnp.maximum(i * hb - 1, 0), 0)),
                pl.BlockSpec((1, SUBLANES, d), lambda b, i: (b, jnp.minimum((i + 1) * hb, s // SUBLANES - 1), 0))]
    in_specs += [_const_spec(t.shape) for t in ins[3:]]
    if has_vres:
        extra = [bf(_pad_to(vres[1], 1, LANES)), bf(_pad_to(vres[2], 0, LANES))]
        ins += extra + [v_first]
        in_specs += [_const_spec(t.shape) for t in extra] + [tile]
    n_bf, n_f32 = 5, 2
    out_shape = ([jax.ShapeDtypeStruct((bsz, s, d), BF16)] * n_bf + [jax.ShapeDtypeStruct((bsz, s, d), F32)] * n_f32
                 + [jax.ShapeDtypeStruct((bsz, s, d), BF16)] * 4)
    if not has_vres:
        out_shape.append(jax.ShapeDtypeStruct((bsz, s, d), F32))
    outs = pl.pallas_call(
        functools.partial(_rwkv_pre_kernel, has_vres=has_vres, seg=RWKV_HEAD),
        out_shape=tuple(out_shape),
        grid=(bsz, nt),
        in_specs=in_specs,
        out_specs=tuple([tile] * len(out_shape)),
        compiler_params=_params("parallel", "parallel"),
        name="rwkv_pre",
    )(*ins)
    r, na, v, g, bonus, lw0, lw1, k0, k1, b0, b1 = outs[:11]
    v_first = v_first if has_vres else outs[11]
    return r, na, v, g, bonus, (lw0, lw1), (k0, k1), (b0, b1), v_first


def _wkv_kernel(r_ref, lw_ref, k_ref, v_ref, a_ref, b_ref, y_ref, h_scr, *, reverse, nh, hd):
    c = pl.program_id(1)

    @pl.when(c == 0)
    def _():
        h_scr[...] = jnp.zeros_like(h_scr)

    nb, L, _ = lw_ref.shape
    row = lax.broadcasted_iota(jnp.int32, (L, L), 0)
    col = lax.broadcasted_iota(jnp.int32, (L, L), 1)
    incl = (col >= row) if reverse else (col <= row)
    tri = jnp.where(incl, 1.0, 0.0).astype(BF16)
    eye = jnp.where(row == col, 1.0, 0.0).astype(F32)
    last = 0 if reverse else L - 1

    def scaled(bi):
        lw = lw_ref[bi]
        lw_hi, lw_lo = _split_bf16(lw)
        cs = (jnp.dot(tri, lw_hi, preferred_element_type=F32)
              + jnp.dot(tri, lw_lo, preferred_element_type=F32))
        ctot = cs[last:last + 1, :]
        r, k, v, a, b = [t[bi].astype(F32) for t in (r_ref, k_ref, v_ref, a_ref, b_ref)]
        einv = jnp.exp(-cs)
        etot = jnp.exp(ctot - cs)
        return dict(rt=r * jnp.exp(cs), at=a * jnp.exp(cs - lw), bt=b * einv, kt=k * einv,
                    bb=b * etot, kb=k * etot, v=v, wtot=jnp.exp(ctot))

    sc_in = [scaled(bi) for bi in range(nb)]

    row2 = lax.broadcasted_iota(jnp.int32, (2 * L, 2 * L), 0)
    col2 = lax.broadcasted_iota(jnp.int32, (2 * L, 2 * L), 1)
    rt2, cs2 = row2 & (L - 1), col2 & (L - 1)
    before = (cs2 > rt2) if reverse else (cs2 < rt2)
    score_mask = before | ((row2 >= L) & (cs2 == rt2))
    right_half = lax.broadcasted_iota(jnp.int32, (L, 2 * L), 1) >= L
    diag_blk = (row ^ col) < 2
    level_masks = []
    size = 2
    while size < L:
        x = row ^ col
        level_masks.append((x >= size) & (x < 2 * size))
        size *= 2

    T = 2 * hd
    lane_lo = lax.broadcasted_iota(jnp.int32, (1, T), 1) < hd
    rowp = lax.broadcasted_iota(jnp.int32, (T, T), 0)
    colp = lax.broadcasted_iota(jnp.int32, (T, T), 1)
    diag_blocks = (rowp < hd) == (colp < hd)
    eye_t = jnp.where(rowp == colp, 1.0, 0.0).astype(F32)
    tiles = [(bi, p) for bi in range(nb) for p in range(nh // 2)]
    units = range(len(tiles))
    heads = [(u, q) for u in units for q in range(2)]
    tile = lambda name, u: sc_in[tiles[u][0]][name][:, tiles[u][1] * T:(tiles[u][1] + 1) * T]
    own = lambda q: lane_lo if q == 0 else jnp.logical_not(lane_lo)

    a2 = [jnp.concatenate([tile("at", u), tile("rt", u)], 0) for u in units]
    b2 = [jnp.concatenate([tile("bt", u), tile("kt", u)], 0).astype(BF16) for u in units]
    v_sw = [pltpu.roll(tile("v", u), hd, 1) for u in units]
    vv_sw = [jnp.concatenate([t_.astype(BF16)] * 2, 0) for t_ in v_sw]
    bbkb_t = [jnp.concatenate([tile("bb", u), tile("kb", u)], 0).T.astype(BF16) for u in units]

    sc = [jnp.where(score_mask, _dot_nt(jnp.where(own(q), a2[u], 0.0), b2[u]), 0.0) for u, q in heads]
    top = [s_[:L] for s_ in sc]
    bot = [s_[L:] for s_ in sc]
    n_ab = [t_[:, :L] for t_ in top]
    t = [eye + jnp.where(diag_blk, n_, 0.0) for n_ in n_ab]
    for lm in level_masks:
        w = [_dot(jnp.where(lm, n_, 0.0), t_) for n_, t_ in zip(n_ab, t)]
        t = [t_ + _dot(t_, w_) for t_, w_ in zip(t, w)]
    x = [_dot(jnp.where(right_half, top[i], 0.0), vv_sw[u]) for i, (u, q) in enumerate(heads)]
    tz = [_dot(t[i], jnp.where(own(q), tile("at", u), x[i])) for i, (u, q) in enumerate(heads)]
    rhs = [jnp.concatenate([tz[i].astype(BF16), jnp.where(own(q), 0.0, v_sw[u]).astype(BF16)], 0)
           for i, (u, q) in enumerate(heads)]
    qp = [_dot(jnp.concatenate([bot[i].astype(BF16), bbkb_t[u][q * hd:(q + 1) * hd]], 0), rhs[i])
          for i, (u, q) in enumerate(heads)]
    for u in units:
        bi, p = tiles[u]
        q0, q1 = qp[2 * u], qp[2 * u + 1]
        rh = tile("rt", u) + jnp.where(lane_lo, q0[:L], q1[:L])
        yh = pltpu.roll(jnp.where(lane_lo, q1[:L], q0[:L]), hd, 1)
        prow = jnp.concatenate([q0[L:], q1[L:]], 0)
        m = jnp.where(diag_blocks, prow, 0.0) + eye_t * tile("wtot", u)
        g = pltpu.roll(jnp.where(diag_blocks, 0.0, prow), hd, 1)
        fin = _dot(jnp.concatenate([rh, m], 0), h_scr[bi, p])
        y_ref[bi, :, p * T:(p + 1) * T] = (fin[:L] + yh).astype(y_ref.dtype)
        h_scr[bi, p] = fin[L:] + g


def wkv7(r, lw, k, v, a, b, reverse, nb=Tiles.wkv_batch_rows):
    bsz, s, d = r.shape
    hd = RWKV_HEAD
    nh = d // hd
    L = min(WKV_CHUNK, s)
    nb = min(nb, bsz)
    assert L == hd and 2 * hd == LANES and nh % 2 == 0 and s % L == 0 and bsz % nb == 0
    nc = s // L
    if reverse:
        idx = lambda bi, ci: (bi, nc - 1 - ci, 0)
    else:
        idx = lambda bi, ci: (bi, ci, 0)
    spec = pl.BlockSpec((nb, L, d), idx)
    return pl.pallas_call(
        functools.partial(_wkv_kernel, reverse=reverse, nh=nh, hd=hd),
        out_shape=jax.ShapeDtypeStruct((bsz, s, d), BF16),
        grid=(bsz // nb, nc),
        in_specs=[spec] * 6,
        out_specs=spec,
        scratch_shapes=[pltpu.VMEM((nb, nh // 2, 2 * hd, 2 * hd), F32)],
        compiler_params=_params("parallel", "arbitrary"),
        name="wkv7_rev" if reverse else "wkv7_fwd",
    )(r, lw, k, v, a, b)


def _rwkv_post_kernel(yf_ref, yb_ref, bon_ref, g_ref, x_ref, vec_ref, wo_ref, o_ref, *, seg):
    y = yf_ref[...].astype(F32) + yb_ref[...].astype(F32)
    yc = y - _seg_sum(y, seg) * (1.0 / seg)
    var = _seg_sum(yc * yc, seg) * (1.0 / seg)
    vec = vec_ref[...]
    yn = yc * lax.rsqrt(var + GN_EPS) * vec[0:1] + vec[1:2] + bon_ref[...].astype(F32)
    o_ref[...] = x_ref[...] + _dot(yn * g_ref[...].astype(F32), wo_ref[...])


def rwkv_post(yf, yb, bonus, g, x, gn_g, gn_b, w_o, bm=Tiles.row_tile):
    m, d = x.shape
    bm = min(bm, m)
    vec = _pad_to(jnp.stack([gn_g, gn_b]), 0, SUBLANES)
    tile = pl.BlockSpec((bm, d), lambda i: (i, 0))
    return pl.pallas_call(
        functools.partial(_rwkv_post_kernel, seg=RWKV_HEAD),
        out_shape=jax.ShapeDtypeStruct((m, d), F32),
        grid=(m // bm,),
        in_specs=[tile] * 5 + [_const_spec(vec.shape), _const_spec(w_o.shape)],
        out_specs=tile,
        compiler_params=_params("parallel"),
        name="rwkv_post",
    )(yf, yb, bonus, g, x, vec, w_o.astype(BF16))


def _head_norm_rope(y2, ctab, stab, o_ref, nh):
    inv_dim = 1.0 / (QK_NOPE + QK_ROPE)
    for h in range(nh):
        sl = slice(h * HEAD_PAD, (h + 1) * HEAD_PAD)
        rh = slice((nh + h) * HEAD_PAD, (nh + h + 1) * HEAD_PAD)
        yh = y2[:, sl]
        ms = jnp.sum(yh * yh, axis=-1, keepdims=True) * inv_dim
        o_ref[0, :, sl] = ((yh * ctab + y2[:, rh] * stab) * lax.rsqrt(ms + NORM_EPS)).astype(o_ref.dtype)


def _mla_pre_kernel(x_ref, gx_ref, gc_ref, hg_ref, cf_ref, sf_ref, win_ref, wq_ref, wk_ref, wvt_ref,
                    q_ref, k_ref, vt_ref, *, nh):
    gc = gc_ref[...]
    h = _rms(x_ref[0], gx_ref[...])
    c = _dot(h, win_ref[...])
    c_q = _rms(c[:, :Q_LORA], gc[0:1])
    c_kv = _rms(c[:, Q_LORA:Q_LORA + KV_LORA], gc[1:2, :KV_LORA]).astype(BF16)
    cf, sf = cf_ref[...], sf_ref[...]
    hg = hg_ref[...]
    _head_norm_rope(_dot(c_q, wq_ref[...]), cf * hg[0:1], sf * hg[1:2], q_ref, nh)
    k_in = jnp.concatenate([c_kv, c[:, Q_LORA + KV_LORA:].astype(BF16)], axis=1)
    _head_norm_rope(_dot(k_in, wk_ref[...]), cf * hg[2:3], sf * hg[3:4], k_ref, nh)
    vt_ref[0] = _dot_nt(wvt_ref[...], c_kv).astype(vt_ref.dtype)


def _pad_heads(w, nh, width, take):
    k = w.shape[0]
    wh = w.reshape(k, nh, width)[:, :, take]
    wh = jnp.pad(wh, ((0, 0), (0, 0), (0, HEAD_PAD - wh.shape[-1])))
    return wh.reshape(k, nh * HEAD_PAD)


def _with_rotate_half(w_pad, nh):
    k = w_pad.shape[0]
    half = QK_ROPE // 2
    w3 = w_pad.reshape(k, nh, HEAD_PAD)
    x1 = w3[:, :, QK_NOPE:QK_NOPE + half]
    x2 = w3[:, :, QK_NOPE + half:QK_NOPE + QK_ROPE]
    z = lambda n: jnp.zeros((k, nh, n), w_pad.dtype)
    rh = jnp.concatenate([z(QK_NOPE), -x2, x1, z(HEAD_PAD - QK_NOPE - QK_ROPE)], axis=-1)
    return jnp.concatenate([w_pad, rh.reshape(k, nh * HEAD_PAD)], axis=1)


def _gain_rows(g):
    half = QK_ROPE // 2
    tail = jnp.zeros((HEAD_PAD - QK_NOPE - QK_ROPE,), F32)
    straight = jnp.concatenate([g, tail])
    swapped = jnp.concatenate([jnp.zeros((QK_NOPE,), F32), g[QK_NOPE + half:], g[QK_NOPE:QK_NOPE + half], tail])
    return [straight, swapped]


def mla_pre(x, tables, norm_g, w_in, q_norm_g, kv_norm_g, w_uq, w_ukv, q_head_g, k_head_g,
            bm=Tiles.mla_pre_rows):
    bsz, s, d = x.shape
    bm = min(bm, s)
    qk = QK_NOPE + QK_ROPE
    nh = w_uq.shape[1] // qk
    assert Q_LORA % LANES == 0 and KV_LORA % LANES == 0 and KV_LORA <= Q_LORA
    gx = norm_g.reshape(1, d)
    gc = _pad_to(jnp.stack([q_norm_g, jnp.pad(kv_norm_g, (0, Q_LORA - KV_LORA))]), 0, SUBLANES)
    hg = _pad_to(jnp.stack(_gain_rows(q_head_g * (qk ** -0.5 * LOG2_E)) + _gain_rows(k_head_g)), 0, SUBLANES)
    win = _pad_to(w_in, 1, LANES)
    rope_w = win.shape[1] - Q_LORA - KV_LORA
    wq = _with_rotate_half(_pad_heads(w_uq, nh, qk, slice(0, qk)), nh)
    wk = _pad_heads(w_ukv, nh, QK_NOPE + V_HEAD, slice(0, QK_NOPE))
    place = jnp.zeros((rope_w, nh, HEAD_PAD), F32)
    place = place.at[jnp.arange(QK_ROPE), :, QK_NOPE + jnp.arange(QK_ROPE)].set(1.0)
    wk = _with_rotate_half(jnp.concatenate([wk, place.reshape(rope_w, nh * HEAD_PAD)], 0), nh)
    wvt = w_ukv.reshape(KV_LORA, nh, QK_NOPE + V_HEAD)[:, :, QK_NOPE:].reshape(KV_LORA, nh * V_HEAD).T
    weights = [t.astype(BF16) for t in (win, wq, wk, wvt)]
    nt = s // bm
    tab = pl.BlockSpec((bm, HEAD_PAD), lambda b, i: (b * nt + i, 0))
    row = lambda n: pl.BlockSpec((1, bm, n), lambda b, i: (b, i, 0))
    return pl.pallas_call(
        functools.partial(_mla_pre_kernel, nh=nh),
        out_shape=(jax.ShapeDtypeStruct((bsz, s, nh * HEAD_PAD), BF16),
                   jax.ShapeDtypeStruct((bsz, s, nh * HEAD_PAD), BF16),
                   jax.ShapeDtypeStruct((bsz, nh * V_HEAD, s), BF16)),
        grid=(bsz, nt),
        in_specs=[row(d), _const_spec(gx.shape), _const_spec(gc.shape), _const_spec(hg.shape), tab, tab]
                 + [_const_spec(t.shape) for t in weights],
        out_specs=(row(nh * HEAD_PAD), row(nh * HEAD_PAD),
                   pl.BlockSpec((1, nh * V_HEAD, bm), lambda b, i: (b, 0, i))),
        compiler_params=_params("parallel", "parallel"),
        name="mla_pre",
    )(x, gx, gc, hg, *tables, *weights)


def _attn_kernel(q_ref, k_ref, vt_ref, o_ref, *, kc):
    s = k_ref.shape[1]
    n_heads = vt_ref.shape[1] // V_HEAD
    heads = range(n_heads)
    qs = [q_ref[0, :, j * HEAD_PAD:(j + 1) * HEAD_PAD] for j in heads]

    def scores(c):
        return [_dot_nt(k_ref[0, c * kc:(c + 1) * kc, j * HEAD_PAD:(j + 1) * HEAD_PAD], qs[j]) for j in heads]

    ones = jnp.ones((ONES_ROWS, kc), BF16)
    nxt = scores(0)
    m, acc = [None] * n_heads, [None] * n_heads
    for c in range(s // kc):
        cur = nxt
        if (c + 1) * kc < s:
            nxt = scores(c + 1)
        for j in heads:
            vt = jnp.concatenate([vt_ref[0, j * V_HEAD:(j + 1) * V_HEAD, c * kc:(c + 1) * kc], ones], axis=0)
            mc = jnp.max(cur[j], axis=0, keepdims=True)
            if c == 0:
                m[j] = mc
                acc[j] = jnp.dot(vt, jnp.exp2((cur[j] - mc).astype(BF16)), preferred_element_type=F32)
            else:
                m_new = jnp.maximum(m[j], mc)
                pt = jnp.exp2((cur[j] - m_new).astype(BF16))
                acc[j] = jnp.exp2(m[j] - m_new) * acc[j] + jnp.dot(vt, pt, preferred_element_type=F32)
                m[j] = m_new
    out = jnp.concatenate([acc[j][:V_HEAD] / acc[j][V_HEAD:V_HEAD + 1] for j in heads], axis=0)
    o_ref[0] = out.T.astype(o_ref.dtype)


def attention(q, k, vt, tq=Tiles.attn_queries, kc=Tiles.attn_key_chunk, hp=Tiles.attn_heads):
    bsz, s, _ = q.shape
    nh = vt.shape[1] // V_HEAD
    tq = min(tq, s)
    kc = min(kc, s)
    hp = min(hp, nh)
    assert nh % hp == 0 and (hp * V_HEAD) % LANES == 0
    return pl.pallas_call(
        functools.partial(_attn_kernel, kc=kc),
        out_shape=jax.ShapeDtypeStruct((bsz, s, nh * V_HEAD), BF16),
        grid=(bsz, nh // hp, s // tq),
        in_specs=[pl.BlockSpec((1, tq, hp * HEAD_PAD), lambda b, h, i: (b, i, h)),
                  pl.BlockSpec((1, s, hp * HEAD_PAD), lambda b, h, i: (b, 0, h)),
                  pl.BlockSpec((1, hp * V_HEAD, s), lambda b, h, i: (b, h, 0))],
        out_specs=pl.BlockSpec((1, tq, hp * V_HEAD), lambda b, h, i: (b, i, h)),
        compiler_params=_params("parallel", "parallel", "parallel"),
        name="mla_attention",
    )(q, k, vt)


def _proj_res_kernel(a_ref, w_ref, x_ref, o_ref):
    o_ref[...] = x_ref[...] + _dot(a_ref[...], w_ref[...])


def proj_residual(a, w, x, bm=Tiles.row_tile):
    m, k = a.shape
    n = w.shape[1]
    bm = min(bm, m)
    return pl.pallas_call(
        _proj_res_kernel,
        out_shape=jax.ShapeDtypeStruct((m, n), F32),
        grid=(m // bm,),
        in_specs=[pl.BlockSpec((bm, k), lambda i: (i, 0)), _const_spec(w.shape),
                  pl.BlockSpec((bm, n), lambda i: (i, 0))],
        out_specs=pl.BlockSpec((bm, n), lambda i: (i, 0)),
        compiler_params=_params("parallel"),
        name="proj_residual",
    )(a, w.astype(BF16), x)


def _affinity_kernel(x_ref, g_ref, wr_ref, gate_ref, hn_ref):
    h = _rms(x_ref[0], g_ref[...])
    hn_ref[0] = h.astype(hn_ref.dtype)
    h_hi, h_lo = _split_bf16(h)
    w_hi, w_lo = _split_bf16(wr_ref[...])
    logits = (lax.dot_general(w_hi, h_hi, NT_DIMS, preferred_element_type=F32)
              + lax.dot_general(w_hi, h_lo, NT_DIMS, preferred_element_type=F32)
              + lax.dot_general(w_lo, h_hi, NT_DIMS, preferred_element_type=F32))
    ex = jnp.exp(logits - jnp.max(logits, axis=0, keepdims=True))
    gate_ref[0] = ex / jnp.sum(ex, axis=0, keepdims=True)


def _select_kernel(aff_ref, pos_ref, *, cap):
    aff = aff_ref[...]
    ne, s = aff.shape
    bits = pltpu.bitcast(aff, jnp.int32)

    def count(mask):
        return jnp.sum(jnp.where(mask, 1.0, 0.0), axis=1, keepdims=True)

    def search(nbits, accept):
        value = jnp.zeros((ne, 1), jnp.int32)
        shift = nbits
        while shift > 0:
            width = 2 if shift % 2 == 0 else 1
            shift -= width
            digit = jnp.zeros((ne, 1), jnp.int32)
            for j in range(1, 1 << width):
                digit = digit + jnp.where(accept(value + (j << shift)), 1, 0)
            value = value + jnp.left_shift(digit, shift)
        return value

    thr = search(31, lambda cand: count(bits >= cand) >= cap)
    gt = bits > thr
    eq = bits == thr
    need = cap - count(gt)
    idx = lax.broadcasted_iota(jnp.int32, (ne, s), 1)
    cut = search(s.bit_length(), lambda cand: count(eq & (idx < cand)) <= need)
    sel = gt | (eq & (idx < cut))

    blk = LANES if s % LANES == 0 else s
    ri = lax.broadcasted_iota(jnp.int32, (blk, blk), 0)
    ci = lax.broadcasted_iota(jnp.int32, (blk, blk), 1)
    upper = jnp.where(ri <= ci, 1.0, 0.0).astype(BF16)
    off = jnp.zeros((ne, 1), F32)
    for j in range(s // blk):
        sl = slice(j * blk, (j + 1) * blk)
        sel_j = sel[:, sl]
        inc = jnp.dot(jnp.where(sel_j, 1.0, 0.0).astype(BF16), upper, preferred_element_type=F32)
        pos_ref[:, sl] = jnp.where(sel_j, (off + inc - 1.0).astype(jnp.int32), -1)
        off = off + inc[:, blk - 1:blk]


def route(x, norm_g, w_router, cap):
    bsz, s, d = x.shape
    ne = w_router.shape[1]
    out = pl.BlockSpec((1, ne, s), lambda b: (b, 0, 0))
    tok = pl.BlockSpec((1, s, d), lambda b: (b, 0, 0))
    gate, hn = pl.pallas_call(
        _affinity_kernel,
        out_shape=(jax.ShapeDtypeStruct((bsz, ne, s), F32), jax.ShapeDtypeStruct((bsz, s, d), BF16)),
        grid=(bsz,),
        in_specs=[tok, _const_spec((1, d)), _const_spec((ne, d))],
        out_specs=(out, tok),
        compiler_params=_params("parallel"),
        name="ec_affinity",
    )(x, norm_g.reshape(1, d), w_router.T)
    pos = pl.pallas_call(
        functools.partial(_select_kernel, cap=cap),
        out_shape=jax.ShapeDtypeStruct((bsz * ne, s), jnp.int32),
        grid=(1,),
        in_specs=[_const_spec((bsz * ne, s))],
        out_specs=_const_spec((bsz * ne, s)),
        compiler_params=_params("arbitrary"),
        name="ec_select",
    )(gate.reshape(bsz * ne, s))
    return pos.reshape(bsz, ne, s), gate, hn


def _selection(pos_row, cap):
    return pos_row == lax.broadcasted_iota(jnp.int32, (cap, pos_row.shape[1]), 0)


def _expert_ffn_kernel(h_ref, pos_ref, gate_ref, wg_ref, wu_ref, wd_ref, ye_ref, wg_s, wu_s, wd_s, *, cap):
    @pl.when(pl.program_id(1) == 0)
    def _():
        wg_s[...] = wg_ref[0, 0].astype(BF16)
        wu_s[...] = wu_ref[0, 0].astype(BF16)
        wd_s[...] = wd_ref[0, 0].astype(BF16)

    rows = range(h_ref.shape[0])
    onehot = [_selection(pos_ref[i, 0], cap) for i in rows]
    sel = [jnp.where(o, 1.0, 0.0).astype(BF16) for o in onehot]
    gcol = [jnp.sum(jnp.where(onehot[i], gate_ref[i, 0], 0.0), axis=1, keepdims=True) for i in rows]
    xe = [jnp.dot(sel[i], h_ref[i], preferred_element_type=F32).astype(BF16) for i in rows]
    hg = [jnp.dot(x_, wg_s[...], preferred_element_type=F32) for x_ in xe]
    hu = [jnp.dot(x_, wu_s[...], preferred_element_type=F32) for x_ in xe]
    hid = [(g_ * jax.nn.sigmoid(g_) * u_).astype(BF16) for g_, u_ in zip(hg, hu)]
    for i in rows:
        ye_ref[i, 0] = (jnp.dot(hid[i], wd_s[...], preferred_element_type=F32) * gcol[i]).astype(ye_ref.dtype)


def _expert_scatter_kernel(x_ref, pos_ref, ye_ref, o_ref, *, cap, eg):
    @pl.when(pl.program_id(1) == 0)
    def _():
        o_ref[...] = x_ref[...]

    sel = jnp.concatenate(
        [jnp.where(_selection(pos_ref[0, g], cap), 1.0, 0.0).astype(BF16) for g in range(eg)], axis=0)
    ye = ye_ref[0].reshape(eg * cap, ye_ref.shape[-1])
    o_ref[0] += lax.dot_general(sel, ye, TN_DIMS, preferred_element_type=F32)


def expert_choice_ffn(x, norm_g, w_router, w_gate, w_up, w_down, layer, eg=Tiles.scatter_experts):
    bsz, s, d = x.shape
    _, ne, _, ff = w_gate.shape
    cap = EC_CAPACITY * s // ne
    eg = min(eg, ne)
    pos, gate, hn = route(x, norm_g, w_router, cap)
    pos4 = pos.reshape(bsz, ne, 1, s)
    gate4 = gate.reshape(bsz, ne, 1, s)
    nb = min(Tiles.expert_batch_rows, bsz)
    assert bsz % nb == 0
    row = pl.BlockSpec((nb, 1, 1, s), lambda e, b: (b, e, 0, 0))
    wspec = lambda shape: pl.BlockSpec((1, 1) + shape, lambda e, b: (layer, e, 0, 0))
    ye = pl.pallas_call(
        functools.partial(_expert_ffn_kernel, cap=cap),
        out_shape=jax.ShapeDtypeStruct((bsz, ne, cap, d), BF16),
        grid=(ne, bsz // nb),
        in_specs=[pl.BlockSpec((nb, s, d), lambda e, b: (b, 0, 0)), row, row,
                  wspec((d, ff)), wspec((d, ff)), wspec((ff, d))],
        out_specs=pl.BlockSpec((nb, 1, cap, d), lambda e, b: (b, e, 0, 0)),
        scratch_shapes=[pltpu.VMEM((d, ff), BF16), pltpu.VMEM((d, ff), BF16), pltpu.VMEM((ff, d), BF16)],
        compiler_params=_params("arbitrary", "arbitrary"),
        name="ec_expert_ffn",
    )(hn, pos4, gate4, w_gate, w_up, w_down)
    tok = pl.BlockSpec((1, s, d), lambda b, j: (b, 0, 0))
    return pl.pallas_call(
        functools.partial(_expert_scatter_kernel, cap=cap, eg=eg),
        out_shape=jax.ShapeDtypeStruct((bsz, s, d), F32),
        grid=(bsz, ne // eg),
        in_specs=[tok, pl.BlockSpec((1, eg, 1, s), lambda b, j: (b, j, 0, 0)),
                  pl.BlockSpec((1, eg, cap, d), lambda b, j: (b, j, 0, 0))],
        out_specs=tok,
        compiler_params=_params("parallel", "arbitrary"),
        name="ec_expert_scatter",
    )(x, pos4, ye)


def rwkv7_layer(x, v_first, norm_g, mu, w_r, w_k, w_v, w_o, w0, w1, w2, a0, a1, a2,
                g1, g2, k_k, k_a, r_k, gn_g, gn_b, vres):
    bsz, s, d = x.shape
    r, na, v, g, bonus, lw, k_d, b_d, v_first = rwkv_pre(
        x, v_first, norm_g, mu, w_r, w_k, w_v, w0, w1, w2, a0, a1, a2, g1, g2, k_k, k_a, r_k, vres)
    yf = wkv7(r, lw[0], k_d[0], v, na, b_d[0], reverse=False)
    yb = wkv7(r, lw[1], k_d[1], v, na, b_d[1], reverse=True)
    flat = lambda t: t.reshape(bsz * s, d)
    out = rwkv_post(flat(yf), flat(yb), flat(bonus), flat(g), flat(x), gn_g, gn_b, w_o)
    return out.reshape(bsz, s, d), v_first


def rope_tables(positions):
    inv_freq = ROPE_THETA ** (-jnp.arange(0, QK_ROPE, 2, dtype=F32) / QK_ROPE)
    ang = positions.astype(F32)[..., None] * inv_freq
    cos, sin = jnp.cos(ang), jnp.sin(ang)
    shape = cos.shape[:-1]
    z = lambda n: jnp.zeros(shape + (n,), F32)
    tail = HEAD_PAD - QK_NOPE - QK_ROPE
    cf = jnp.concatenate([jnp.ones(shape + (QK_NOPE,), F32), cos, cos, z(tail)], -1)
    sf = jnp.concatenate([z(QK_NOPE), sin, sin, z(tail)], -1)
    return cf.reshape(-1, HEAD_PAD), sf.reshape(-1, HEAD_PAD)


def mla_layer(x, tables, norm_g, w_in, q_norm_g, kv_norm_g, w_uq, w_ukv, q_head_g, k_head_g, w_o):
    bsz, s, d = x.shape
    q, k, vt = mla_pre(x, tables, norm_g, w_in, q_norm_g, kv_norm_g, w_uq, w_ukv, q_head_g, k_head_g)
    o = attention(q, k, vt)
    return proj_residual(o.reshape(bsz * s, -1), w_o, x.reshape(bsz * s, d)).reshape(bsz, s, d)


def kernel(x, positions, norm_mix_g, norm_ffn_g,
           rw_mu, rw_wr, rw_wk, rw_wv, rw_wo, rw_w0, rw_w1, rw_w2,
           rw_a0, rw_a1, rw_a2, rw_g1, rw_g2, rw_kk, rw_ka, rw_rk,
           rw_gn_g, rw_gn_b, rw_v0, rw_v1, rw_v2,
           mla_w_in, mla_q_norm_g, mla_kv_norm_g, mla_w_uq, mla_w_ukv,
           mla_q_head_g, mla_k_head_g, mla_w_o,
           moe_router, moe_w_gate, moe_w_up, moe_w_down):
    depth = norm_mix_g.shape[0]
    n_mixers = 2
    tables = rope_tables(positions)
    v_first = None
    for i in range(depth):
        j = i // n_mixers
        if i % n_mixers == 0:
            vres = None if j == 0 else (rw_v0[j - 1], rw_v1[j - 1], rw_v2[j - 1])
            x, v_first = rwkv7_layer(
                x, v_first, norm_mix_g[i], rw_mu[j], rw_wr[j], rw_wk[j], rw_wv[j], rw_wo[j],
                rw_w0[j], rw_w1[j], rw_w2[j], rw_a0[j], rw_a1[j], rw_a2[j],
                rw_g1[j], rw_g2[j], rw_kk[j], rw_ka[j], rw_rk[j],
                rw_gn_g[j], rw_gn_b[j], vres)
        else:
            x = mla_layer(x, tables, norm_mix_g[i], mla_w_in[j], mla_q_norm_g[j], mla_kv_norm_g[j],
                          mla_w_uq[j], mla_w_ukv[j], mla_q_head_g[j], mla_k_head_g[j], mla_w_o[j])
        x = expert_choice_ffn(x, norm_ffn_g[i], moe_router[i], moe_w_gate, moe_w_up, moe_w_down, layer=i)
    return x
```

```python
import functools

import jax
import jax.numpy as jnp
from jax import lax
from jax.experimental import pallas as pl
from jax.experimental.pallas import tpu as pltpu

F32 = jnp.float32
BF16 = jnp.bfloat16

NORM_EPS = 1e-6
GN_EPS = 64e-5
RWKV_HEAD = 64
QK_NOPE = 64
QK_ROPE = 32
V_HEAD = 64
Q_LORA = 384
KV_LORA = 256
ROPE_THETA = 10000.0
EC_CAPACITY = 2
LANES = 128
SUBLANES = 8
HEAD_PAD = LANES
ONES_ROWS = 16
LOG2_E = 1.4426950408889634
DECAY_SCALE = 0.6065306597126334
WKV_CHUNK = 64
VMEM_LIMIT = 56 * 1024 * 1024


class Tiles:
    rwkv_pre_rows = 256
    mla_pre_rows = 512
    row_tile = 1024
    attn_queries = 2048
    attn_key_chunk = 128
    attn_heads = 4
    wkv_batch_rows = 2
    scatter_experts = 4
    expert_batch_rows = 2

NT_DIMS = (((1,), (1,)), ((), ()))
TN_DIMS = (((0,), (0,)), ((), ()))


def _dot(a, b):
    return jnp.dot(a.astype(BF16), b.astype(BF16), preferred_element_type=F32)


def _dot_nt(a, b):
    return lax.dot_general(a.astype(BF16), b.astype(BF16), NT_DIMS, preferred_element_type=F32)


def _split_bf16(x):
    hi = x.astype(BF16)
    lo = (x - hi.astype(F32)).astype(BF16)
    return hi, lo


def _rms(x, gain):
    return x * lax.rsqrt(jnp.mean(x * x, axis=-1, keepdims=True) + NORM_EPS) * gain


def _seg_sum(t, seg):
    width = min(LANES, t.shape[1])
    ri = lax.broadcasted_iota(jnp.int32, (width, width), 0)
    ci = lax.broadcasted_iota(jnp.int32, (width, width), 1)
    ones = jnp.where((ri // seg) == (ci // seg), 1.0, 0.0).astype(BF16)
    parts = [_dot(t[:, j:j + width], ones) for j in range(0, t.shape[1], width)]
    return parts[0] if len(parts) == 1 else jnp.concatenate(parts, axis=1)


def _pad_to(w, axis, mult):
    pad = -w.shape[axis] % mult
    if pad == 0:
        return w
    widths = [(0, 0)] * w.ndim
    widths[axis] = (0, pad)
    return jnp.pad(w, widths)


def _const_spec(shape):
    return pl.BlockSpec(shape, lambda *_: (0,) * len(shape))


def _params(*sem):
    return pltpu.CompilerParams(dimension_semantics=sem, vmem_limit_bytes=VMEM_LIMIT)


def _rwkv_pre_kernel(*refs, has_vres, seg):
    it = iter(refs)
    (x_ref, xp_ref, xn_ref, vec_ref, wr_ref, wk_ref, wv_ref,
     w1_ref, a1_ref, g1_ref, w2_ref, a2_ref, g2_ref) = [next(it) for _ in range(13)]
    if has_vres:
        v1_ref, v2_ref, vf_ref = next(it), next(it), next(it)
    r_o, na_o, v_o, g_o, bon_o = [next(it) for _ in range(5)]
    lw_o = [next(it), next(it)]
    k_o = [next(it), next(it)]
    b_o = [next(it), next(it)]
    if not has_vres:
        vf_o = next(it)

    i = pl.program_id(1)
    vec = vec_ref[...]
    row_of = lambda j: vec[j:j + 1]
    gain = row_of(0)
    x = x_ref[0]
    bm = x.shape[0]
    h = _rms(x, gain)
    hp = jnp.where(i == 0, 0.0, _rms(xp_ref[0][SUBLANES - 1:SUBLANES], gain))
    hn = jnp.where(i == pl.num_programs(1) - 1, 0.0, _rms(xn_ref[0][0:1], gain))
    row = lax.broadcasted_iota(jnp.int32, (bm, 1), 0)
    h_prev = jnp.where(row == 0, hp, pltpu.roll(h, 1, 0))
    h_next = jnp.where(row == bm - 1, hn, pltpu.roll(h, bm - 1, 0))
    xx = 0.5 * (h_prev + h_next) - h
    xr, xw, xk, xv, xa, xg = [(h + xx * row_of(1 + j)).astype(BF16) for j in range(6)]

    r = _dot(xr, wr_ref[...])
    k = _dot(xk, wk_ref[...])
    v = _dot(xv, wv_ref[...])
    if has_vres:
        vz = row_of(14) + _dot(_dot(xv, v1_ref[...]), v2_ref[...])
        v = v + (vf_ref[0] - v) * jax.nn.sigmoid(vz)
    else:
        vf_o[0] = v
    g = _dot(jax.nn.sigmoid(_dot(xg, g1_ref[...])), g2_ref[...])

    kkr = k * row_of(11)
    kk = kkr * lax.rsqrt(jnp.maximum(_seg_sum(kkr * kkr, seg), 1e-24))
    tw = jnp.tanh(_dot(xw, w1_ref[...])).astype(BF16)
    al = _dot(xa, a1_ref[...]).astype(BF16)
    k_keep = k * (1.0 - row_of(12))
    k_mix = k * row_of(12)
    k_ds = []
    for d in range(2):
        z = row_of(7 + d) + _dot(tw, w2_ref[d])
        lw_o[d][0] = -DECAY_SCALE * jax.nn.sigmoid(z)
        a = jax.nn.sigmoid(row_of(9 + d) + _dot(al, a2_ref[d]))
        k_d = k_keep + k_mix * a
        k_o[d][0] = k_d.astype(BF16)
        b_o[d][0] = (kk * a).astype(BF16)
        k_ds.append(k_d)
    bon_o[0] = (_seg_sum(r * (k_ds[0] + k_ds[1]) * row_of(13), seg) * v).astype(BF16)
    r_o[0] = r.astype(BF16)
    na_o[0] = (-kk).astype(BF16)
    v_o[0] = v.astype(BF16)
    g_o[0] = g.astype(BF16)


def rwkv_pre(x, v_first, norm_g, mu, w_r, w_k, w_v, w0, w1, w2, a0, a1, a2, g1, g2, k_k, k_a, r_k,
             vres, bm=Tiles.rwkv_pre_rows):
    bsz, s, d = x.shape
    bm = min(bm, s)
    nt = s // bm
    has_vres = vres is not None
    zero = jnp.zeros((d,), F32)
    v0 = vres[0] if has_vres else zero
    vec = jnp.stack([norm_g, *mu, w0[0], w0[1], a0[0], a0[1], k_k, k_a, r_k.reshape(d), v0, zero])
    bf = lambda t: t.astype(BF16)
    cat2 = lambda t: jnp.concatenate([t[0], t[1]], axis=1)
    lo = w1.shape[2]
    second = lambda t: jnp.stack([jnp.pad(t[0], ((0, lo), (0, 0))), jnp.pad(t[1], ((lo, 0), (0, 0)))])
    ins = [x, x, x, vec, bf(w_r), bf(w_k), bf(w_v), bf(cat2(w1)), bf(cat2(a1)), bf(_pad_to(g1, 1, LANES)),
           bf(second(w2)), bf(second(a2)), bf(_pad_to(g2, 0, LANES))]
    tile = pl.BlockSpec((1, bm, d), lambda b, i: (b, i, 0))
    hb = bm // SUBLANES
    in_specs = [tile,
                pl.BlockSpec((1, SUBLANES, d), lambda b, i: (b, jnp.maximum(i * hb - 1, 0), 0)),
                pl.BlockSpec((1, SUBLANES, d), lambda b, i: (b, jnp.minimum((i + 1) * hb, s // SUBLANES - 1), 0))]
    in_specs += [_const_spec(t.shape) for t in ins[3:]]
    if has_vres:
        extra = [bf(_pad_to(vres[1], 1, LANES)), bf(_pad_to(vres[2], 0, LANES))]
        ins += extra + [v_first]
        in_specs += [_const_spec(t.shape) for t in extra] + [tile]
    n_bf, n_f32 = 5, 2
    out_shape = ([jax.ShapeDtypeStruct((bsz, s, d), BF16)] * n_bf + [jax.ShapeDtypeStruct((bsz, s, d), F32)] * n_f32
                 + [jax.ShapeDtypeStruct((bsz, s, d), BF16)] * 4)
    if not has_vres:
        out_shape.append(jax.ShapeDtypeStruct((bsz, s, d), F32))
    outs = pl.pallas_call(
        functools.partial(_rwkv_pre_kernel, has_vres=has_vres, seg=RWKV_HEAD),
        out_shape=tuple(out_shape),
        grid=(bsz, nt),
        in_specs=in_specs,
        out_specs=tuple([tile] * len(out_shape)),
        compiler_params=_params("parallel", "parallel"),
        name="rwkv_pre",
    )(*ins)
    r, na, v, g, bonus, lw0, lw1, k0, k1, b0, b1 = outs[:11]
    v_first = v_first if has_vres else outs[11]
    return r, na, v, g, bonus, (lw0, lw1), (k0, k1), (b0, b1), v_first


def _wkv_kernel(r_ref, lw_ref, k_ref, v_ref, a_ref, b_ref, y_ref, h_scr, *, reverse, nh, hd):
    c = pl.program_id(1)

    @pl.when(c == 0)
    def _():
        h_scr[...] = jnp.zeros_like(h_scr)

    nb, L, _ = lw_ref.shape
    row = lax.broadcasted_iota(jnp.int32, (L, L), 0)
    col = lax.broadcasted_iota(jnp.int32, (L, L), 1)
    incl = (col >= row) if reverse else (col <= row)
    tri = jnp.where(incl, 1.0, 0.0).astype(BF16)
    eye = jnp.where(row == col, 1.0, 0.0).astype(F32)
    last = 0 if reverse else L - 1

    def scaled(bi):
        lw = lw_ref[bi]
        lw_hi, lw_lo = _split_bf16(lw)
        cs = (jnp.dot(tri, lw_hi, preferred_element_type=F32)
              + jnp.dot(tri, lw_lo, preferred_element_type=F32))
        ctot = cs[last:last + 1, :]
        r, k, v, a, b = [t[bi].astype(F32) for t in (r_ref, k_ref, v_ref, a_ref, b_ref)]
        einv = jnp.exp(-cs)
        etot = jnp.exp(ctot - cs)
        return dict(rt=r * jnp.exp(cs), at=a * jnp.exp(cs - lw), bt=b * einv, kt=k * einv,
                    bb=b * etot, kb=k * etot, v=v, wtot=jnp.exp(ctot))

    sc_in = [scaled(bi) for bi in range(nb)]

    row2 = lax.broadcasted_iota(jnp.int32, (2 * L, 2 * L), 0)
    col2 = lax.broadcasted_iota(jnp.int32, (2 * L, 2 * L), 1)
    rt2, cs2 = row2 & (L - 1), col2 & (L - 1)
    before = (cs2 > rt2) if reverse else (cs2 < rt2)
    score_mask = before | ((row2 >= L) & (cs2 == rt2))
    right_half = lax.broadcasted_iota(jnp.int32, (L, 2 * L), 1) >= L
    diag_blk = (row ^ col) < 2
    level_masks = []
    size = 2
    while size < L:
        x = row ^ col
        level_masks.append((x >= size) & (x < 2 * size))
        size *= 2

    T = 2 * hd
    lane_lo = lax.broadcasted_iota(jnp.int32, (1, T), 1) < hd
    rowp = lax.broadcasted_iota(jnp.int32, (T, T), 0)
    colp = lax.broadcasted_iota(jnp.int32, (T, T), 1)
    diag_blocks = (rowp < hd) == (colp < hd)
    eye_t = jnp.where(rowp == colp, 1.0, 0.0).astype(F32)
    tiles = [(bi, p) for bi in range(nb) for p in range(nh // 2)]
    units = range(len(tiles))
    heads = [(u, q) for u in units for q in range(2)]
    tile = lambda name, u: sc_in[tiles[u][0]][name][:, tiles[u][1] * T:(tiles[u][1] + 1) * T]
    own = lambda q: lane_lo if q == 0 else jnp.logical_not(lane_lo)

    a2 = [jnp.concatenate([tile("at", u), tile("rt", u)], 0) for u in units]
    b2 = [jnp.concatenate([tile("bt", u), tile("kt", u)], 0).astype(BF16) for u in units]
    v_sw = [pltpu.roll(tile("v", u), hd, 1) for u in units]
    vv_sw = [jnp.concatenate([t_.astype(BF16)] * 2, 0) for t_ in v_sw]
    bbkb_t = [jnp.concatenate([tile("bb", u), tile("kb", u)], 0).T.astype(BF16) for u in units]

    sc = [jnp.where(score_mask, _dot_nt(jnp.where(own(q), a2[u], 0.0), b2[u]), 0.0) for u, q in heads]
    top = [s_[:L] for s_ in sc]
    bot = [s_[L:] for s_ in sc]
    n_ab = [t_[:, :L] for t_ in top]
    t = [eye + jnp.where(diag_blk, n_, 0.0) for n_ in n_ab]
    for lm in level_masks:
        w = [_dot(jnp.where(lm, n_, 0.0), t_) for n_, t_ in zip(n_ab, t)]
        t = [t_ + _dot(t_, w_) for t_, w_ in zip(t, w)]
    x = [_dot(jnp.where(right_half, top[i], 0.0), vv_sw[u]) for i, (u, q) in enumerate(heads)]
    tz = [_dot(t[i], jnp.where(own(q), tile("at", u), x[i])) for i, (u, q) in enumerate(heads)]
    rhs = [jnp.concatenate([tz[i].astype(BF16), jnp.where(own(q), 0.0, v_sw[u]).astype(BF16)], 0)
           for i, (u, q) in enumerate(heads)]
    qp = [_dot(jnp.concatenate([bot[i].astype(BF16), bbkb_t[u][q * hd:(q + 1) * hd]], 0), rhs[i])
          for i, (u, q) in enumerate(heads)]
    for u in units:
        bi, p = tiles[u]
        q0, q1 = qp[2 * u], qp[2 * u + 1]
        rh = tile("rt", u) + jnp.where(lane_lo, q0[:L], q1[:L])
        yh = pltpu.roll(jnp.where(lane_lo, q1[:L], q0[:L]), hd, 1)
        prow = jnp.concatenate([q0[L:], q1[L:]], 0)
        m = jnp.where(diag_blocks, prow, 0.0) + eye_t * tile("wtot", u)
        g = pltpu.roll(jnp.where(diag_blocks, 0.0, prow), hd, 1)
        fin = _dot(jnp.concatenate([rh, m], 0), h_scr[bi, p])
        y_ref[bi, :, p * T:(p + 1) * T] = (fin[:L] + yh).astype(y_ref.dtype)
        h_scr[bi, p] = fin[L:] + g


def wkv7(r, lw, k, v, a, b, reverse, nb=Tiles.wkv_batch_rows):
    bsz, s, d = r.shape
    hd = RWKV_HEAD
    nh = d // hd
    L = min(WKV_CHUNK, s)
    nb = min(nb, bsz)
    assert L == hd and 2 * hd == LANES and nh % 2 == 0 and s % L == 0 and bsz % nb == 0
    nc = s // L
    if reverse:
        idx = lambda bi, ci: (bi, nc - 1 - ci, 0)
    else:
        idx = lambda bi, ci: (bi, ci, 0)
    spec = pl.BlockSpec((nb, L, d), idx)
    return pl.pallas_call(
        functools.partial(_wkv_kernel, reverse=reverse, nh=nh, hd=hd),
        out_shape=jax.ShapeDtypeStruct((bsz, s, d), BF16),
        grid=(bsz // nb, nc),
        in_specs=[spec] * 6,
        out_specs=spec,
        scratch_shapes=[pltpu.VMEM((nb, nh // 2, 2 * hd, 2 * hd), F32)],
        compiler_params=_params("parallel", "arbitrary"),
        name="wkv7_rev" if reverse else "wkv7_fwd",
    )(r, lw, k, v, a, b)


def _rwkv_post_kernel(yf_ref, yb_ref, bon_ref, g_ref, x_ref, vec_ref, wo_ref, o_ref, *, seg):
    y = yf_ref[...].astype(F32) + yb_ref[...].astype(F32)
    yc = y - _seg_sum(y, seg) * (1.0 / seg)
    var = _seg_sum(yc * yc, seg) * (1.0 / seg)
    vec = vec_ref[...]
    yn = yc * lax.rsqrt(var + GN_EPS) * vec[0:1] + vec[1:2] + bon_ref[...].astype(F32)
    o_ref[...] = x_ref[...] + _dot(yn * g_ref[...].astype(F32), wo_ref[...])


def rwkv_post(yf, yb, bonus, g, x, gn_g, gn_b, w_o, bm=Tiles.row_tile):
    m, d = x.shape
    bm = min(bm, m)
    vec = _pad_to(jnp.stack([gn_g, gn_b]), 0, SUBLANES)
    tile = pl.BlockSpec((bm, d), lambda i: (i, 0))
    return pl.pallas_call(
        functools.partial(_rwkv_post_kernel, seg=RWKV_HEAD),
        out_shape=jax.ShapeDtypeStruct((m, d), F32),
        grid=(m // bm,),
        in_specs=[tile] * 5 + [_const_spec(vec.shape), _const_spec(w_o.shape)],
        out_specs=tile,
        compiler_params=_params("parallel"),
        name="rwkv_post",
    )(yf, yb, bonus, g, x, vec, w_o.astype(BF16))


def _head_norm_rope(y2, ctab, stab, o_ref, nh):
    inv_dim = 1.0 / (QK_NOPE + QK_ROPE)
    for h in range(nh):
        sl = slice(h * HEAD_PAD, (h + 1) * HEAD_PAD)
        rh = slice((nh + h) * HEAD_PAD, (nh + h + 1) * HEAD_PAD)
        yh = y2[:, sl]
        ms = jnp.sum(yh * yh, axis=-1, keepdims=True) * inv_dim
        o_ref[0, :, sl] = ((yh * ctab + y2[:, rh] * stab) * lax.rsqrt(ms + NORM_EPS)).astype(o_ref.dtype)


def _mla_pre_kernel(x_ref, gx_ref, gc_ref, hg_ref, cf_ref, sf_ref, win_ref, wq_ref, wk_ref, wvt_ref,
                    q_ref, k_ref, vt_ref, *, nh):
    gc = gc_ref[...]
    h = _rms(x_ref[0], gx_ref[...])
    c = _dot(h, win_ref[...])
    c_q = _rms(c[:, :Q_LORA], gc[0:1])
    c_kv = _rms(c[:, Q_LORA:Q_LORA + KV_LORA], gc[1:2, :KV_LORA]).astype(BF16)
    cf, sf = cf_ref[...], sf_ref[...]
    hg = hg_ref[...]
    _head_norm_rope(_dot(c_q, wq_ref[...]), cf * hg[0:1], sf * hg[1:2], q_ref, nh)
    k_in = jnp.concatenate([c_kv, c[:, Q_LORA + KV_LORA:].astype(BF16)], axis=1)
    _head_norm_rope(_dot(k_in, wk_ref[...]), cf * hg[2:3], sf * hg[3:4], k_ref, nh)
    vt_ref[0] = _dot_nt(wvt_ref[...], c_kv).astype(vt_ref.dtype)


def _pad_heads(w, nh, width, take):
    k = w.shape[0]
    wh = w.reshape(k, nh, width)[:, :, take]
    wh = jnp.pad(wh, ((0, 0), (0, 0), (0, HEAD_PAD - wh.shape[-1])))
    return wh.reshape(k, nh * HEAD_PAD)


def _with_rotate_half(w_pad, nh):
    k = w_pad.shape[0]
    half = QK_ROPE // 2
    w3 = w_pad.reshape(k, nh, HEAD_PAD)
    x1 = w3[:, :, QK_NOPE:QK_NOPE + half]
    x2 = w3[:, :, QK_NOPE + half:QK_NOPE + QK_ROPE]
    z = lambda n: jnp.zeros((k, nh, n), w_pad.dtype)
    rh = jnp.concatenate([z(QK_NOPE), -x2, x1, z(HEAD_PAD - QK_NOPE - QK_ROPE)], axis=-1)
    return jnp.concatenate([w_pad, rh.reshape(k, nh * HEAD_PAD)], axis=1)


def _gain_rows(g):
    half = QK_ROPE // 2
    tail = jnp.zeros((HEAD_PAD - QK_NOPE - QK_ROPE,), F32)
    straight = jnp.concatenate([g, tail])
    swapped = jnp.concatenate([jnp.zeros((QK_NOPE,), F32), g[QK_NOPE + half:], g[QK_NOPE:QK_NOPE + half], tail])
    return [straight, swapped]


def mla_pre(x, tables, norm_g, w_in, q_norm_g, kv_norm_g, w_uq, w_ukv, q_head_g, k_head_g,
            bm=Tiles.mla_pre_rows):
    bsz, s, d = x.shape
    bm = min(bm, s)
    qk = QK_NOPE + QK_ROPE
    nh = w_uq.shape[1] // qk
    assert Q_LORA % LANES == 0 and KV_LORA % LANES == 0 and KV_LORA <= Q_LORA
    gx = norm_g.reshape(1, d)
    gc = _pad_to(jnp.stack([q_norm_g, jnp.pad(kv_norm_g, (0, Q_LORA - KV_LORA))]), 0, SUBLANES)
    hg = _pad_to(jnp.stack(_gain_rows(q_head_g * (qk ** -0.5 * LOG2_E)) + _gain_rows(k_head_g)), 0, SUBLANES)
    win = _pad_to(w_in, 1, LANES)
    rope_w = win.shape[1] - Q_LORA - KV_LORA
    wq = _with_rotate_half(_pad_heads(w_uq, nh, qk, slice(0, qk)), nh)
    wk = _pad_heads(w_ukv, nh, QK_NOPE + V_HEAD, slice(0, QK_NOPE))
    place = jnp.zeros((rope_w, nh, HEAD_PAD), F32)
    place = place.at[jnp.arange(QK_ROPE), :, QK_NOPE + jnp.arange(QK_ROPE)].set(1.0)
    wk = _with_rotate_half(jnp.concatenate([wk, place.reshape(rope_w, nh * HEAD_PAD)], 0), nh)
    wvt = w_ukv.reshape(KV_LORA, nh, QK_NOPE + V_HEAD)[:, :, QK_NOPE:].reshape(KV_LORA, nh * V_HEAD).T
    weights = [t.astype(BF16) for t in (win, wq, wk, wvt)]
    nt = s // bm
    tab = pl.BlockSpec((bm, HEAD_PAD), lambda b, i: (b * nt + i, 0))
    row = lambda n: pl.BlockSpec((1, bm, n), lambda b, i: (b, i, 0))
    return pl.pallas_call(
        functools.partial(_mla_pre_kernel, nh=nh),
        out_shape=(jax.ShapeDtypeStruct((bsz, s, nh * HEAD_PAD), BF16),
                   jax.ShapeDtypeStruct((bsz, s, nh * HEAD_PAD), BF16),
                   jax.ShapeDtypeStruct((bsz, nh * V_HEAD, s), BF16)),
        grid=(bsz, nt),
        in_specs=[row(d), _const_spec(gx.shape), _const_spec(gc.shape), _const_spec(hg.shape), tab, tab]
                 + [_const_spec(t.shape) for t in weights],
        out_specs=(row(nh * HEAD_PAD), row(nh * HEAD_PAD),
                   pl.BlockSpec((1, nh * V_HEAD, bm), lambda b, i: (b, 0, i))),
        compiler_params=_params("parallel", "parallel"),
        name="mla_pre",
    )(x, gx, gc, hg, *tables, *weights)


def _attn_kernel(q_ref, k_ref, vt_ref, o_ref, *, kc):
    s = k_ref.shape[1]
    n_heads = vt_ref.shape[1] // V_HEAD
    heads = range(n_heads)
    qs = [q_ref[0, :, j * HEAD_PAD:(j + 1) * HEAD_PAD] for j in heads]

    def scores(c):
        return [_dot_nt(k_ref[0, c * kc:(c + 1) * kc, j * HEAD_PAD:(j + 1) * HEAD_PAD], qs[j]) for j in heads]

    ones = jnp.ones((ONES_ROWS, kc), BF16)
    nxt = scores(0)
    m, acc = [None] * n_heads, [None] * n_heads
    for c in range(s // kc):
        cur = nxt
        if (c + 1) * kc < s:
            nxt = scores(c + 1)
        for j in heads:
            vt = jnp.concatenate([vt_ref[0, j * V_HEAD:(j + 1) * V_HEAD, c * kc:(c + 1) * kc], ones], axis=0)
            mc = jnp.max(cur[j], axis=0, keepdims=True)
            if c == 0:
                m[j] = mc
                acc[j] = jnp.dot(vt, jnp.exp2((cur[j] - mc).astype(BF16)), preferred_element_type=F32)
            else:
                m_new = jnp.maximum(m[j], mc)
                pt = jnp.exp2((cur[j] - m_new).astype(BF16))
                acc[j] = jnp.exp2(m[j] - m_new) * acc[j] + jnp.dot(vt, pt, preferred_element_type=F32)
                m[j] = m_new
    out = jnp.concatenate([acc[j][:V_HEAD] / acc[j][V_HEAD:V_HEAD + 1] for j in heads], axis=0)
    o_ref[0] = out.T.astype(o_ref.dtype)


def attention(q, k, vt, tq=Tiles.attn_queries, kc=Tiles.attn_key_chunk, hp=Tiles.attn_heads):
    bsz, s, _ = q.shape
    nh = vt.shape[1] // V_HEAD
    tq = min(tq, s)
    kc = min(kc, s)
    hp = min(hp, nh)
    assert nh % hp == 0 and (hp * V_HEAD) % LANES == 0
    return pl.pallas_call(
        functools.partial(_attn_kernel, kc=kc),
        out_shape=jax.ShapeDtypeStruct((bsz, s, nh * V_HEAD), BF16),
        grid=(bsz, nh // hp, s // tq),
        in_specs=[pl.BlockSpec((1, tq, hp * HEAD_PAD), lambda b, h, i: (b, i, h)),
                  pl.BlockSpec((1, s, hp * HEAD_PAD), lambda b, h, i: (b, 0, h)),
                  pl.BlockSpec((1, hp * V_HEAD, s), lambda b, h, i: (b, h, 0))],
        out_specs=pl.BlockSpec((1, tq, hp * V_HEAD), lambda b, h, i: (b, i, h)),
        compiler_params=_params("parallel", "parallel", "parallel"),
        name="mla_attention",
    )(q, k, vt)


def _proj_res_kernel(a_ref, w_ref, x_ref, o_ref):
    o_ref[...] = x_ref[...] + _dot(a_ref[...], w_ref[...])


def proj_residual(a, w, x, bm=Tiles.row_tile):
    m, k = a.shape
    n = w.shape[1]
    bm = min(bm, m)
    return pl.pallas_call(
        _proj_res_kernel,
        out_shape=jax.ShapeDtypeStruct((m, n), F32),
        grid=(m // bm,),
        in_specs=[pl.BlockSpec((bm, k), lambda i: (i, 0)), _const_spec(w.shape),
                  pl.BlockSpec((bm, n), lambda i: (i, 0))],
        out_specs=pl.BlockSpec((bm, n), lambda i: (i, 0)),
        compiler_params=_params("parallel"),
        name="proj_residual",
    )(a, w.astype(BF16), x)


def _affinity_kernel(x_ref, g_ref, wr_ref, gate_ref, hn_ref):
    h = _rms(x_ref[0], g_ref[...])
    hn_ref[0] = h.astype(hn_ref.dtype)
    h_hi, h_lo = _split_bf16(h)
    w_hi, w_lo = _split_bf16(wr_ref[...])
    logits = (lax.dot_general(w_hi, h_hi, NT_DIMS, preferred_element_type=F32)
              + lax.dot_general(w_hi, h_lo, NT_DIMS, preferred_element_type=F32)
              + lax.dot_general(w_lo, h_hi, NT_DIMS, preferred_element_type=F32))
    ex = jnp.exp(logits - jnp.max(logits, axis=0, keepdims=True))
    gate_ref[0] = ex / jnp.sum(ex, axis=0, keepdims=True)


def _select_kernel(aff_ref, pos_ref, *, cap):
    aff = aff_ref[...]
    ne, s = aff.shape
    bits = pltpu.bitcast(aff, jnp.int32)

    def count(mask):
        return jnp.sum(jnp.where(mask, 1.0, 0.0), axis=1, keepdims=True)

    def search(nbits, accept):
        value = jnp.zeros((ne, 1), jnp.int32)
        shift = nbits
        while shift > 0:
            width = 2 if shift % 2 == 0 else 1
            shift -= width
            digit = jnp.zeros((ne, 1), jnp.int32)
            for j in range(1, 1 << width):
                digit = digit + jnp.where(accept(value + (j << shift)), 1, 0)
            value = value + jnp.left_shift(digit, shift)
        return value

    thr = search(31, lambda cand: count(bits >= cand) >= cap)
    gt = bits > thr
    eq = bits == thr
    need = cap - count(gt)
    idx = lax.broadcasted_iota(jnp.int32, (ne, s), 1)
    cut = search(s.bit_length(), lambda cand: count(eq & (idx < cand)) <= need)
    sel = gt | (eq & (idx < cut))

    blk = LANES if s % LANES == 0 else s
    ri = lax.broadcasted_iota(jnp.int32, (blk, blk), 0)
    ci = lax.broadcasted_iota(jnp.int32, (blk, blk), 1)
    upper = jnp.where(ri <= ci, 1.0, 0.0).astype(BF16)
    off = jnp.zeros((ne, 1), F32)
    for j in range(s // blk):
        sl = slice(j * blk, (j + 1) * blk)
        sel_j = sel[:, sl]
        inc = jnp.dot(jnp.where(sel_j, 1.0, 0.0).astype(BF16), upper, preferred_element_type=F32)
        pos_ref[:, sl] = jnp.where(sel_j, (off + inc - 1.0).astype(jnp.int32), -1)
        off = off + inc[:, blk - 1:blk]


def route(x, norm_g, w_router, cap):
    bsz, s, d = x.shape
    ne = w_router.shape[1]
    out = pl.BlockSpec((1, ne, s), lambda b: (b, 0, 0))
    tok = pl.BlockSpec((1, s, d), lambda b: (b, 0, 0))
    gate, hn = pl.pallas_call(
        _affinity_kernel,
        out_shape=(jax.ShapeDtypeStruct((bsz, ne, s), F32), jax.ShapeDtypeStruct((bsz, s, d), BF16)),
        grid=(bsz,),
        in_specs=[tok, _const_spec((1, d)), _const_spec((ne, d))],
        out_specs=(out, tok),
        compiler_params=_params("parallel"),
        name="ec_affinity",
    )(x, norm_g.reshape(1, d), w_router.T)
    pos = pl.pallas_call(
        functools.partial(_select_kernel, cap=cap),
        out_shape=jax.ShapeDtypeStruct((bsz * ne, s), jnp.int32),
        grid=(1,),
        in_specs=[_const_spec((bsz * ne, s))],
        out_specs=_const_spec((bsz * ne, s)),
        compiler_params=_params("arbitrary"),
        name="ec_select",
    )(gate.reshape(bsz * ne, s))
    return pos.reshape(bsz, ne, s), gate, hn


def _selection(pos_row, cap):
    return pos_row == lax.broadcasted_iota(jnp.int32, (cap, pos_row.shape[1]), 0)


def _expert_ffn_kernel(h_ref, pos_ref, gate_ref, wg_ref, wu_ref, wd_ref, ye_ref, wg_s, wu_s, wd_s, *, cap):
    @pl.when(pl.program_id(1) == 0)
    def _():
        wg_s[...] = wg_ref[0, 0].astype(BF16)
        wu_s[...] = wu_ref[0, 0].astype(BF16)
        wd_s[...] = wd_ref[0, 0].astype(BF16)

    rows = range(h_ref.shape[0])
    onehot = [_selection(pos_ref[i, 0], cap) for i in rows]
    sel = [jnp.where(o, 1.0, 0.0).astype(BF16) for o in onehot]
    gcol = [jnp.sum(jnp.where(onehot[i], gate_ref[i, 0], 0.0), axis=1, keepdims=True) for i in rows]
    xe = [jnp.dot(sel[i], h_ref[i], preferred_element_type=F32).astype(BF16) for i in rows]
    hg = [jnp.dot(x_, wg_s[...], preferred_element_type=F32) for x_ in xe]
    hu = [jnp.dot(x_, wu_s[...], preferred_element_type=F32) for x_ in xe]
    hid = [(g_ * jax.nn.sigmoid(g_) * u_).astype(BF16) for g_, u_ in zip(hg, hu)]
    for i in rows:
        ye_ref[i, 0] = (jnp.dot(hid[i], wd_s[...], preferred_element_type=F32) * gcol[i]).astype(ye_ref.dtype)


def _expert_scatter_kernel(x_ref, pos_ref, ye_ref, o_ref, *, cap, eg):
    @pl.when(pl.program_id(1) == 0)
    def _():
        o_ref[...] = x_ref[...]

    sel = jnp.concatenate(
        [jnp.where(_selection(pos_ref[0, g], cap), 1.0, 0.0).astype(BF16) for g in range(eg)], axis=0)
    ye = ye_ref[0].reshape(eg * cap, ye_ref.shape[-1])
    o_ref[0] += lax.dot_general(sel, ye, TN_DIMS, preferred_element_type=F32)


def expert_choice_ffn(x, norm_g, w_router, w_gate, w_up, w_down, layer, eg=Tiles.scatter_experts):
    bsz, s, d = x.shape
    _, ne, _, ff = w_gate.shape
    cap = EC_CAPACITY * s // ne
    eg = min(eg, ne)
    pos, gate, hn = route(x, norm_g, w_router, cap)
    pos4 = pos.reshape(bsz, ne, 1, s)
    gate4 = gate.reshape(bsz, ne, 1, s)
    nb = min(Tiles.expert_batch_rows, bsz)
    assert bsz % nb == 0
    row = pl.BlockSpec((nb, 1, 1, s), lambda e, b: (b, e, 0, 0))
    wspec = lambda shape: pl.BlockSpec((1, 1) + shape, lambda e, b: (layer, e, 0, 0))
    ye = pl.pallas_call(
        functools.partial(_expert_ffn_kernel, cap=cap),
        out_shape=jax.ShapeDtypeStruct((bsz, ne, cap, d), BF16),
        grid=(ne, bsz // nb),
        in_specs=[pl.BlockSpec((nb, s, d), lambda e, b: (b, 0, 0)), row, row,
                  wspec((d, ff)), wspec((d, ff)), wspec((ff, d))],
        out_specs=pl.BlockSpec((nb, 1, cap, d), lambda e, b: (b, e, 0, 0)),
        scratch_shapes=[pltpu.VMEM((d, ff), BF16), pltpu.VMEM((d, ff), BF16), pltpu.VMEM((ff, d), BF16)],
        compiler_params=_params("arbitrary", "arbitrary"),
        name="ec_expert_ffn",
    )(hn, pos4, gate4, w_gate, w_up, w_down)
    tok = pl.BlockSpec((1, s, d), lambda b, j: (b, 0, 0))
    return pl.pallas_call(
        functools.partial(_expert_scatter_kernel, cap=cap, eg=eg),
        out_shape=jax.ShapeDtypeStruct((bsz, s, d), F32),
        grid=(bsz, ne // eg),
        in_specs=[tok, pl.BlockSpec((1, eg, 1, s), lambda b, j: (b, j, 0, 0)),
                  pl.BlockSpec((1, eg, cap, d), lambda b, j: (b, j, 0, 0))],
        out_specs=tok,
        compiler_params=_params("parallel", "arbitrary"),
        name="ec_expert_scatter",
    )(x, pos4, ye)


def rwkv7_layer(x, v_first, norm_g, mu, w_r, w_k, w_v, w_o, w0, w1, w2, a0, a1, a2,
                g1, g2, k_k, k_a, r_k, gn_g, gn_b, vres):
    bsz, s, d = x.shape
    r, na, v, g, bonus, lw, k_d, b_d, v_first = rwkv_pre(
        x, v_first, norm_g, mu, w_r, w_k, w_v, w0, w1, w2, a0, a1, a2, g1, g2, k_k, k_a, r_k, vres)
    yf = wkv7(r, lw[0], k_d[0], v, na, b_d[0], reverse=False)
    yb = wkv7(r, lw[1], k_d[1], v, na, b_d[1], reverse=True)
    flat = lambda t: t.reshape(bsz * s, d)
    out = rwkv_post(flat(yf), flat(yb), flat(bonus), flat(g), flat(x), gn_g, gn_b, w_o)
    return out.reshape(bsz, s, d), v_first


def rope_tables(positions):
    inv_freq = ROPE_THETA ** (-jnp.arange(0, QK_ROPE, 2, dtype=F32) / QK_ROPE)
    ang = positions.astype(F32)[..., None] * inv_freq
    cos, sin = jnp.cos(ang), jnp.sin(ang)
    shape = cos.shape[:-1]
    z = lambda n: jnp.zeros(shape + (n,), F32)
    tail = HEAD_PAD - QK_NOPE - QK_ROPE
    cf = jnp.concatenate([jnp.ones(shape + (QK_NOPE,), F32), cos, cos, z(tail)], -1)
    sf = jnp.concatenate([z(QK_NOPE), sin, sin, z(tail)], -1)
    return cf.reshape(-1, HEAD_PAD), sf.reshape(-1, HEAD_PAD)


def mla_layer(x, tables, norm_g, w_in, q_norm_g, kv_norm_g, w_uq, w_ukv, q_head_g, k_head_g, w_o):
    bsz, s, d = x.shape
    q, k, vt = mla_pre(x, tables, norm_g, w_in, q_norm_g, kv_norm_g, w_uq, w_ukv, q_head_g, k_head_g)
    o = attention(q, k, vt)
    return proj_residual(o.reshape(bsz * s, -1), w_o, x.reshape(bsz * s, d)).reshape(bsz, s, d)


def kernel(x, positions, norm_mix_g, norm_ffn_g,
           rw_mu, rw_wr, rw_wk, rw_wv, rw_wo, rw_w0, rw_w1, rw_w2,
           rw_a0, rw_a1, rw_a2, rw_g1, rw_g2, rw_kk, rw_ka, rw_rk,
           rw_gn_g, rw_gn_b, rw_v0, rw_v1, rw_v2,
           mla_w_in, mla_q_norm_g, mla_kv_norm_g, mla_w_uq, mla_w_ukv,
           mla_q_head_g, mla_k_head_g, mla_w_o,
           moe_router, moe_w_gate, moe_w_up, moe_w_down):
    depth = norm_mix_g.shape[0]
    n_mixers = 2
    tables = rope_tables(positions)
    v_first = None
    for i in range(depth):
        j = i // n_mixers
        if i % n_mixers == 0:
            vres = None if j == 0 else (rw_v0[j - 1], rw_v1[j - 1], rw_v2[j - 1])
            x, v_first = rwkv7_layer(
                x, v_first, norm_mix_g[i], rw_mu[j], rw_wr[j], rw_wk[j], rw_wv[j], rw_wo[j],
                rw_w0[j], rw_w1[j], rw_w2[j], rw_a0[j], rw_a1[j], rw_a2[j],
                rw_g1[j], rw_g2[j], rw_kk[j], rw_ka[j], rw_rk[j],
                rw_gn_g[j], rw_gn_b[j], vres)
        else:
            x = mla_layer(x, tables, norm_mix_g[i], mla_w_in[j], mla_q_norm_g[j], mla_kv_norm_g[j],
                          mla_w_uq[j], mla_w_ukv[j], mla_q_head_g[j], mla_k_head_g[j], mla_w_o[j])
        x = expert_choice_ffn(x, norm_ffn_g[i], moe_router[i], moe_w_gate, moe_w_up, moe_w_down, layer=i)
    return x
```

```python
import functools

import jax
import jax.numpy as jnp
from jax import lax
from jax.experimental import pallas as pl
from jax.experimental.pallas import tpu as pltpu

F32 = jnp.float32
BF16 = jnp.bfloat16

NORM_EPS = 1e-6
GN_EPS = 64e-5
RWKV_HEAD = 64
QK_NOPE = 64
QK_ROPE = 32
V_HEAD = 64
Q_LORA = 384
KV_LORA = 256
ROPE_THETA = 10000.0
EC_CAPACITY = 2
LANES = 128
SUBLANES = 8
HEAD_PAD = LANES
ONES_ROWS = 16
LOG2_E = 1.4426950408889634
DECAY_SCALE = 0.6065306597126334
WKV_CHUNK = 64
VMEM_LIMIT = 56 * 1024 * 1024


class Tiles:
    rwkv_pre_rows = 256
    mla_pre_rows = 512
    row_tile = 1024
    attn_queries = 2048
    attn_key_chunk = 128
    attn_heads = 4
    wkv_batch_rows = 4
    scatter_experts = 4
    expert_batch_rows = 2

NT_DIMS = (((1,), (1,)), ((), ()))
TN_DIMS = (((0,), (0,)), ((), ()))


def _dot(a, b):
    return jnp.dot(a.astype(BF16), b.astype(BF16), preferred_element_type=F32)


def _dot_nt(a, b):
    return lax.dot_general(a.astype(BF16), b.astype(BF16), NT_DIMS, preferred_element_type=F32)


def _split_bf16(x):
    hi = x.astype(BF16)
    lo = (x - hi.astype(F32)).astype(BF16)
    return hi, lo


def _rms(x, gain):
    return x * lax.rsqrt(jnp.mean(x * x, axis=-1, keepdims=True) + NORM_EPS) * gain


def _seg_sum(t, seg):
    width = min(LANES, t.shape[1])
    ri = lax.broadcasted_iota(jnp.int32, (width, width), 0)
    ci = lax.broadcasted_iota(jnp.int32, (width, width), 1)
    ones = jnp.where((ri // seg) == (ci // seg), 1.0, 0.0).astype(BF16)
    parts = [_dot(t[:, j:j + width], ones) for j in range(0, t.shape[1], width)]
    return parts[0] if len(parts) == 1 else jnp.concatenate(parts, axis=1)


def _pad_to(w, axis, mult):
    pad = -w.shape[axis] % mult
    if pad == 0:
        return w
    widths = [(0, 0)] * w.ndim
    widths[axis] = (0, pad)
    return jnp.pad(w, widths)


def _const_spec(shape):
    return pl.BlockSpec(shape, lambda *_: (0,) * len(shape))


def _params(*sem):
    return pltpu.CompilerParams(dimension_semantics=sem, vmem_limit_bytes=VMEM_LIMIT)


def _rwkv_pre_kernel(*refs, has_vres, seg):
    it = iter(refs)
    (x_ref, xp_ref, xn_ref, vec_ref, wr_ref, wk_ref, wv_ref,
     w1_ref, a1_ref, g1_ref, w2_ref, a2_ref, g2_ref) = [next(it) for _ in range(13)]
    if has_vres:
        v1_ref, v2_ref, vf_ref = next(it), next(it), next(it)
    r_o, na_o, v_o, g_o, bon_o = [next(it) for _ in range(5)]
    lw_o = [next(it), next(it)]
    k_o = [next(it), next(it)]
    b_o = [next(it), next(it)]
    if not has_vres:
        vf_o = next(it)

    i = pl.program_id(1)
    vec = vec_ref[...]
    row_of = lambda j: vec[j:j + 1]
    gain = row_of(0)
    x = x_ref[0]
    bm = x.shape[0]
    h = _rms(x, gain)
    hp = jnp.where(i == 0, 0.0, _rms(xp_ref[0][SUBLANES - 1:SUBLANES], gain))
    hn = jnp.where(i == pl.num_programs(1) - 1, 0.0, _rms(xn_ref[0][0:1], gain))
    row = lax.broadcasted_iota(jnp.int32, (bm, 1), 0)
    h_prev = jnp.where(row == 0, hp, pltpu.roll(h, 1, 0))
    h_next = jnp.where(row == bm - 1, hn, pltpu.roll(h, bm - 1, 0))
    xx = 0.5 * (h_prev + h_next) - h
    xr, xw, xk, xv, xa, xg = [(h + xx * row_of(1 + j)).astype(BF16) for j in range(6)]

    r = _dot(xr, wr_ref[...])
    k = _dot(xk, wk_ref[...])
    v = _dot(xv, wv_ref[...])
    if has_vres:
        vz = row_of(14) + _dot(_dot(xv, v1_ref[...]), v2_ref[...])
        v = v + (vf_ref[0] - v) * jax.nn.sigmoid(vz)
    else:
        vf_o[0] = v
    g = _dot(jax.nn.sigmoid(_dot(xg, g1_ref[...])), g2_ref[...])

    kkr = k * row_of(11)
    kk = kkr * lax.rsqrt(jnp.maximum(_seg_sum(kkr * kkr, seg), 1e-24))
    tw = jnp.tanh(_dot(xw, w1_ref[...])).astype(BF16)
    al = _dot(xa, a1_ref[...]).astype(BF16)
    k_keep = k * (1.0 - row_of(12))
    k_mix = k * row_of(12)
    k_ds = []
    for d in range(2):
        z = row_of(7 + d) + _dot(tw, w2_ref[d])
        lw_o[d][0] = -DECAY_SCALE * jax.nn.sigmoid(z)
        a = jax.nn.sigmoid(row_of(9 + d) + _dot(al, a2_ref[d]))
        k_d = k_keep + k_mix * a
        k_o[d][0] = k_d.astype(BF16)
        b_o[d][0] = (kk * a).astype(BF16)
        k_ds.append(k_d)
    bon_o[0] = (_seg_sum(r * (k_ds[0] + k_ds[1]) * row_of(13), seg) * v).astype(BF16)
    r_o[0] = r.astype(BF16)
    na_o[0] = (-kk).astype(BF16)
    v_o[0] = v.astype(BF16)
    g_o[0] = g.astype(BF16)


def rwkv_pre(x, v_first, norm_g, mu, w_r, w_k, w_v, w0, w1, w2, a0, a1, a2, g1, g2, k_k, k_a, r_k,
             vres, bm=Tiles.rwkv_pre_rows):
    bsz, s, d = x.shape
    bm = min(bm, s)
    nt = s // bm
    has_vres = vres is not None
    zero = jnp.zeros((d,), F32)
    v0 = vres[0] if has_vres else zero
    vec = jnp.stack([norm_g, *mu, w0[0], w0[1], a0[0], a0[1], k_k, k_a, r_k.reshape(d), v0, zero])
    bf = lambda t: t.astype(BF16)
    cat2 = lambda t: jnp.concatenate([t[0], t[1]], axis=1)
    lo = w1.shape[2]
    second = lambda t: jnp.stack([jnp.pad(t[0], ((0, lo), (0, 0))), jnp.pad(t[1], ((lo, 0), (0, 0)))])
    ins = [x, x, x, vec, bf(w_r), bf(w_k), bf(w_v), bf(cat2(w1)), bf(cat2(a1)), bf(_pad_to(g1, 1, LANES)),
           bf(second(w2)), bf(second(a2)), bf(_pad_to(g2, 0, LANES))]
    tile = pl.BlockSpec((1, bm, d), lambda b, i: (b, i, 0))
    hb = bm // SUBLANES
    in_specs = [tile,
                pl.BlockSpec((1, SUBLANES, d), lambda b, i: (b, jnp.maximum(i * hb - 1, 0), 0)),
                pl.BlockSpec((1, SUBLANES, d), lambda b, i: (b, jnp.minimum((i + 1) * hb, s // SUBLANES - 1), 0))]
    in_specs += [_const_spec(t.shape) for t in ins[3:]]
    if has_vres:
        extra = [bf(_pad_to(vres[1], 1, LANES)), bf(_pad_to(vres[2], 0, LANES))]
        ins += extra + [v_first]
        in_specs += [_const_spec(t.shape) for t in extra] + [tile]
    n_bf, n_f32 = 5, 2
    out_shape = ([jax.ShapeDtypeStruct((bsz, s, d), BF16)] * n_bf + [jax.ShapeDtypeStruct((bsz, s, d), F32)] * n_f32
                 + [jax.ShapeDtypeStruct((bsz, s, d), BF16)] * 4)
    if not has_vres:
        out_shape.append(jax.ShapeDtypeStruct((bsz, s, d), F32))
    outs = pl.pallas_call(
        functools.partial(_rwkv_pre_kernel, has_vres=has_vres, seg=RWKV_HEAD),
        out_shape=tuple(out_shape),
        grid=(bsz, nt),
        in_specs=in_specs,
        out_specs=tuple([tile] * len(out_shape)),
        compiler_params=_params("parallel", "parallel"),
        name="rwkv_pre",
    )(*ins)
    r, na, v, g, bonus, lw0, lw1, k0, k1, b0, b1 = outs[:11]
    v_first = v_first if has_vres else outs[11]
    return r, na, v, g, bonus, (lw0, lw1), (k0, k1), (b0, b1), v_first


def _wkv_kernel(r_ref, lw_ref, k_ref, v_ref, a_ref, b_ref, y_ref, h_scr, *, reverse, nh, hd):
    c = pl.program_id(1)

    @pl.when(c == 0)
    def _():
        h_scr[...] = jnp.zeros_like(h_scr)

    nb, L, _ = lw_ref.shape
    row = lax.broadcasted_iota(jnp.int32, (L, L), 0)
    col = lax.broadcasted_iota(jnp.int32, (L, L), 1)
    incl = (col >= row) if reverse else (col <= row)
    tri = jnp.where(incl, 1.0, 0.0).astype(BF16)
    eye = jnp.where(row == col, 1.0, 0.0).astype(F32)
    last = 0 if reverse else L - 1

    def scaled(bi):
        lw = lw_ref[bi]
        lw_hi, lw_lo = _split_bf16(lw)
        cs = (jnp.dot(tri, lw_hi, preferred_element_type=F32)
              + jnp.dot(tri, lw_lo, preferred_element_type=F32))
        ctot = cs[last:last + 1, :]
        r, k, v, a, b = [t[bi].astype(F32) for t in (r_ref, k_ref, v_ref, a_ref, b_ref)]
        einv = jnp.exp(-cs)
        etot = jnp.exp(ctot - cs)
        return dict(rt=r * jnp.exp(cs), at=a * jnp.exp(cs - lw), bt=b * einv, kt=k * einv,
                    bb=b * etot, kb=k * etot, v=v, wtot=jnp.exp(ctot))

    sc_in = [scaled(bi) for bi in range(nb)]

    row2 = lax.broadcasted_iota(jnp.int32, (2 * L, 2 * L), 0)
    col2 = lax.broadcasted_iota(jnp.int32, (2 * L, 2 * L), 1)
    rt2, cs2 = row2 & (L - 1), col2 & (L - 1)
    before = (cs2 > rt2) if reverse else (cs2 < rt2)
    score_mask = before | ((row2 >= L) & (cs2 == rt2))
    right_half = lax.broadcasted_iota(jnp.int32, (L, 2 * L), 1) >= L
    diag_blk = (row ^ col) < 2
    level_masks = []
    size = 2
    while size < L:
        x = row ^ col
        level_masks.append((x >= size) & (x < 2 * size))
        size *= 2

    T = 2 * hd
    lane_lo = lax.broadcasted_iota(jnp.int32, (1, T), 1) < hd
    rowp = lax.broadcasted_iota(jnp.int32, (T, T), 0)
    colp = lax.broadcasted_iota(jnp.int32, (T, T), 1)
    diag_blocks = (rowp < hd) == (colp < hd)
    eye_t = jnp.where(rowp == colp, 1.0, 0.0).astype(F32)
    tiles = [(bi, p) for bi in range(nb) for p in range(nh // 2)]
    units = range(len(tiles))
    heads = [(u, q) for u in units for q in range(2)]
    tile = lambda name, u: sc_in[tiles[u][0]][name][:, tiles[u][1] * T:(tiles[u][1] + 1) * T]
    own = lambda q: lane_lo if q == 0 else jnp.logical_not(lane_lo)

    a2 = [jnp.concatenate([tile("at", u), tile("rt", u)], 0) for u in units]
    b2 = [jnp.concatenate([tile("bt", u), tile("kt", u)], 0).astype(BF16) for u in units]
    v_sw = [pltpu.roll(tile("v", u), hd, 1) for u in units]
    vv_sw = [jnp.concatenate([t_.astype(BF16)] * 2, 0) for t_ in v_sw]
    bbkb_t = [jnp.concatenate([tile("bb", u), tile("kb", u)], 0).T.astype(BF16) for u in units]

    sc = [jnp.where(score_mask, _dot_nt(jnp.where(own(q), a2[u], 0.0), b2[u]), 0.0) for u, q in heads]
    top = [s_[:L] for s_ in sc]
    bot = [s_[L:] for s_ in sc]
    n_ab = [t_[:, :L] for t_ in top]
    t = [eye + jnp.where(diag_blk, n_, 0.0) for n_ in n_ab]
    for lm in level_masks:
        w = [_dot(jnp.where(lm, n_, 0.0), t_) for n_, t_ in zip(n_ab, t)]
        t = [t_ + _dot(t_, w_) for t_, w_ in zip(t, w)]
    x = [_dot(jnp.where(right_half, top[i], 0.0), vv_sw[u]) for i, (u, q) in enumerate(heads)]
    tz = [_dot(t[i], jnp.where(own(q), tile("at", u), x[i])) for i, (u, q) in enumerate(heads)]
    rhs = [jnp.concatenate([tz[i].astype(BF16), jnp.where(own(q), 0.0, v_sw[u]).astype(BF16)], 0)
           for i, (u, q) in enumerate(heads)]
    qp = [_dot(jnp.concatenate([bot[i].astype(BF16), bbkb_t[u][q * hd:(q + 1) * hd]], 0), rhs[i])
          for i, (u, q) in enumerate(heads)]
    for u in units:
        bi, p = tiles[u]
        q0, q1 = qp[2 * u], qp[2 * u + 1]
        rh = tile("rt", u) + jnp.where(lane_lo, q0[:L], q1[:L])
        yh = pltpu.roll(jnp.where(lane_lo, q1[:L], q0[:L]), hd, 1)
        prow = jnp.concatenate([q0[L:], q1[L:]], 0)
        m = jnp.where(diag_blocks, prow, 0.0) + eye_t * tile("wtot", u)
        g = pltpu.roll(jnp.where(diag_blocks, 0.0, prow), hd, 1)
        fin = _dot(jnp.concatenate([rh, m], 0), h_scr[bi, p])
        y_ref[bi, :, p * T:(p + 1) * T] = (fin[:L] + yh).astype(y_ref.dtype)
        h_scr[bi, p] = fin[L:] + g


def wkv7(r, lw, k, v, a, b, reverse, nb=Tiles.wkv_batch_rows):
    bsz, s, d = r.shape
    hd = RWKV_HEAD
    nh = d // hd
    L = min(WKV_CHUNK, s)
    nb = min(nb, bsz)
    assert L == hd and 2 * hd == LANES and nh % 2 == 0 and s % L == 0 and bsz % nb == 0
    nc = s // L
    if reverse:
        idx = lambda bi, ci: (bi, nc - 1 - ci, 0)
    else:
        idx = lambda bi, ci: (bi, ci, 0)
    spec = pl.BlockSpec((nb, L, d), idx)
    return pl.pallas_call(
        functools.partial(_wkv_kernel, reverse=reverse, nh=nh, hd=hd),
        out_shape=jax.ShapeDtypeStruct((bsz, s, d), BF16),
        grid=(bsz // nb, nc),
        in_specs=[spec] * 6,
        out_specs=spec,
        scratch_shapes=[pltpu.VMEM((nb, nh // 2, 2 * hd, 2 * hd), F32)],
        compiler_params=_params("parallel", "arbitrary"),
        name="wkv7_rev" if reverse else "wkv7_fwd",
    )(r, lw, k, v, a, b)


def _rwkv_post_kernel(yf_ref, yb_ref, bon_ref, g_ref, x_ref, vec_ref, wo_ref, o_ref, *, seg):
    y = yf_ref[...].astype(F32) + yb_ref[...].astype(F32)
    yc = y - _seg_sum(y, seg) * (1.0 / seg)
    var = _seg_sum(yc * yc, seg) * (1.0 / seg)
    vec = vec_ref[...]
    yn = yc * lax.rsqrt(var + GN_EPS) * vec[0:1] + vec[1:2] + bon_ref[...].astype(F32)
    o_ref[...] = x_ref[...] + _dot(yn * g_ref[...].astype(F32), wo_ref[...])


def rwkv_post(yf, yb, bonus, g, x, gn_g, gn_b, w_o, bm=Tiles.row_tile):
    m, d = x.shape
    bm = min(bm, m)
    vec = _pad_to(jnp.stack([gn_g, gn_b]), 0, SUBLANES)
    tile = pl.BlockSpec((bm, d), lambda i: (i, 0))
    return pl.pallas_call(
        functools.partial(_rwkv_post_kernel, seg=RWKV_HEAD),
        out_shape=jax.ShapeDtypeStruct((m, d), F32),
        grid=(m // bm,),
        in_specs=[tile] * 5 + [_const_spec(vec.shape), _const_spec(w_o.shape)],
        out_specs=tile,
        compiler_params=_params("parallel"),
        name="rwkv_post",
    )(yf, yb, bonus, g, x, vec, w_o.astype(BF16))


def _head_norm_rope(y2, ctab, stab, o_ref, nh):
    inv_dim = 1.0 / (QK_NOPE + QK_ROPE)
    for h in range(nh):
        sl = slice(h * HEAD_PAD, (h + 1) * HEAD_PAD)
        rh = slice((nh + h) * HEAD_PAD, (nh + h + 1) * HEAD_PAD)
        yh = y2[:, sl]
        ms = jnp.sum(yh * yh, axis=-1, keepdims=True) * inv_dim
        o_ref[0, :, sl] = ((yh * ctab + y2[:, rh] * stab) * lax.rsqrt(ms + NORM_EPS)).astype(o_ref.dtype)


def _mla_pre_kernel(x_ref, gx_ref, gc_ref, hg_ref, cf_ref, sf_ref, win_ref, wq_ref, wk_ref, wvt_ref,
                    q_ref, k_ref, vt_ref, *, nh):
    gc = gc_ref[...]
    h = _rms(x_ref[0], gx_ref[...])
    c = _dot(h, win_ref[...])
    c_q = _rms(c[:, :Q_LORA], gc[0:1])
    c_kv = _rms(c[:, Q_LORA:Q_LORA + KV_LORA], gc[1:2, :KV_LORA]).astype(BF16)
    cf, sf = cf_ref[...], sf_ref[...]
    hg = hg_ref[...]
    _head_norm_rope(_dot(c_q, wq_ref[...]), cf * hg[0:1], sf * hg[1:2], q_ref, nh)
    k_in = jnp.concatenate([c_kv, c[:, Q_LORA + KV_LORA:].astype(BF16)], axis=1)
    _head_norm_rope(_dot(k_in, wk_ref[...]), cf * hg[2:3], sf * hg[3:4], k_ref, nh)
    vt_ref[0] = _dot_nt(wvt_ref[...], c_kv).astype(vt_ref.dtype)


def _pad_heads(w, nh, width, take):
    k = w.shape[0]
    wh = w.reshape(k, nh, width)[:, :, take]
    wh = jnp.pad(wh, ((0, 0), (0, 0), (0, HEAD_PAD - wh.shape[-1])))
    return wh.reshape(k, nh * HEAD_PAD)


def _with_rotate_half(w_pad, nh):
    k = w_pad.shape[0]
    half = QK_ROPE // 2
    w3 = w_pad.reshape(k, nh, HEAD_PAD)
    x1 = w3[:, :, QK_NOPE:QK_NOPE + half]
    x2 = w3[:, :, QK_NOPE + half:QK_NOPE + QK_ROPE]
    z = lambda n: jnp.zeros((k, nh, n), w_pad.dtype)
    rh = jnp.concatenate([z(QK_NOPE), -x2, x1, z(HEAD_PAD - QK_NOPE - QK_ROPE)], axis=-1)
    return jnp.concatenate([w_pad, rh.reshape(k, nh * HEAD_PAD)], axis=1)


def _gain_rows(g):
    half = QK_ROPE // 2
    tail = jnp.zeros((HEAD_PAD - QK_NOPE - QK_ROPE,), F32)
    straight = jnp.concatenate([g, tail])
    swapped = jnp.concatenate([jnp.zeros((QK_NOPE,), F32), g[QK_NOPE + half:], g[QK_NOPE:QK_NOPE + half], tail])
    return [straight, swapped]


def mla_pre(x, tables, norm_g, w_in, q_norm_g, kv_norm_g, w_uq, w_ukv, q_head_g, k_head_g,
            bm=Tiles.mla_pre_rows):
    bsz, s, d = x.shape
    bm = min(bm, s)
    qk = QK_NOPE + QK_ROPE
    nh = w_uq.shape[1] // qk
    assert Q_LORA % LANES == 0 and KV_LORA % LANES == 0 and KV_LORA <= Q_LORA
    gx = norm_g.reshape(1, d)
    gc = _pad_to(jnp.stack([q_norm_g, jnp.pad(kv_norm_g, (0, Q_LORA - KV_LORA))]), 0, SUBLANES)
    hg = _pad_to(jnp.stack(_gain_rows(q_head_g * (qk ** -0.5 * LOG2_E)) + _gain_rows(k_head_g)), 0, SUBLANES)
    win = _pad_to(w_in, 1, LANES)
    rope_w = win.shape[1] - Q_LORA - KV_LORA
    wq = _with_rotate_half(_pad_heads(w_uq, nh, qk, slice(0, qk)), nh)
    wk = _pad_heads(w_ukv, nh, QK_NOPE + V_HEAD, slice(0, QK_NOPE))
    place = jnp.zeros((rope_w, nh, HEAD_PAD), F32)
    place = place.at[jnp.arange(QK_ROPE), :, QK_NOPE + jnp.arange(QK_ROPE)].set(1.0)
    wk = _with_rotate_half(jnp.concatenate([wk, place.reshape(rope_w, nh * HEAD_PAD)], 0), nh)
    wvt = w_ukv.reshape(KV_LORA, nh, QK_NOPE + V_HEAD)[:, :, QK_NOPE:].reshape(KV_LORA, nh * V_HEAD).T
    weights = [t.astype(BF16) for t in (win, wq, wk, wvt)]
    nt = s // bm
    tab = pl.BlockSpec((bm, HEAD_PAD), lambda b, i: (b * nt + i, 0))
    row = lambda n: pl.BlockSpec((1, bm, n), lambda b, i: (b, i, 0))
    return pl.pallas_call(
        functools.partial(_mla_pre_kernel, nh=nh),
        out_shape=(jax.ShapeDtypeStruct((bsz, s, nh * HEAD_PAD), BF16),
                   jax.ShapeDtypeStruct((bsz, s, nh * HEAD_PAD), BF16),
                   jax.ShapeDtypeStruct((bsz, nh * V_HEAD, s), BF16)),
        grid=(bsz, nt),
        in_specs=[row(d), _const_spec(gx.shape), _const_spec(gc.shape), _const_spec(hg.shape), tab, tab]
                 + [_const_spec(t.shape) for t in weights],
        out_specs=(row(nh * HEAD_PAD), row(nh * HEAD_PAD),
                   pl.BlockSpec((1, nh * V_HEAD, bm), lambda b, i: (b, 0, i))),
        compiler_params=_params("parallel", "parallel"),
        name="mla_pre",
    )(x, gx, gc, hg, *tables, *weights)


def _attn_kernel(q_ref, k_ref, vt_ref, o_ref, *, kc):
    s = k_ref.shape[1]
    n_heads = vt_ref.shape[1] // V_HEAD
    heads = range(n_heads)
    qs = [q_ref[0, :, j * HEAD_PAD:(j + 1) * HEAD_PAD] for j in heads]

    def scores(c):
        return [_dot_nt(k_ref[0, c * kc:(c + 1) * kc, j * HEAD_PAD:(j + 1) * HEAD_PAD], qs[j]) for j in heads]

    ones = jnp.ones((ONES_ROWS, kc), BF16)
    nxt = scores(0)
    m, acc = [None] * n_heads, [None] * n_heads
    for c in range(s // kc):
        cur = nxt
        if (c + 1) * kc < s:
            nxt = scores(c + 1)
        for j in heads:
            vt = jnp.concatenate([vt_ref[0, j * V_HEAD:(j + 1) * V_HEAD, c * kc:(c + 1) * kc], ones], axis=0)
            mc = jnp.max(cur[j], axis=0, keepdims=True)
            if c == 0:
                m[j] = mc
                acc[j] = jnp.dot(vt, jnp.exp2((cur[j] - mc).astype(BF16)), preferred_element_type=F32)
            else:
                m_new = jnp.maximum(m[j], mc)
                pt = jnp.exp2((cur[j] - m_new).astype(BF16))
                acc[j] = jnp.exp2(m[j] - m_new) * acc[j] + jnp.dot(vt, pt, preferred_element_type=F32)
                m[j] = m_new
    out = jnp.concatenate([acc[j][:V_HEAD] / acc[j][V_HEAD:V_HEAD + 1] for j in heads], axis=0)
    o_ref[0] = out.T.astype(o_ref.dtype)


def attention(q, k, vt, tq=Tiles.attn_queries, kc=Tiles.attn_key_chunk, hp=Tiles.attn_heads):
    bsz, s, _ = q.shape
    nh = vt.shape[1] // V_HEAD
    tq = min(tq, s)
    kc = min(kc, s)
    hp = min(hp, nh)
    assert nh % hp == 0 and (hp * V_HEAD) % LANES == 0
    return pl.pallas_call(
        functools.partial(_attn_kernel, kc=kc),
        out_shape=jax.ShapeDtypeStruct((bsz, s, nh * V_HEAD), BF16),
        grid=(bsz, nh // hp, s // tq),
        in_specs=[pl.BlockSpec((1, tq, hp * HEAD_PAD), lambda b, h, i: (b, i, h)),
                  pl.BlockSpec((1, s, hp * HEAD_PAD), lambda b, h, i: (b, 0, h)),
                  pl.BlockSpec((1, hp * V_HEAD, s), lambda b, h, i: (b, h, 0))],
        out_specs=pl.BlockSpec((1, tq, hp * V_HEAD), lambda b, h, i: (b, i, h)),
        compiler_params=_params("parallel", "parallel", "parallel"),
        name="mla_attention",
    )(q, k, vt)


def _proj_res_kernel(a_ref, w_ref, x_ref, o_ref):
    o_ref[...] = x_ref[...] + _dot(a_ref[...], w_ref[...])


def proj_residual(a, w, x, bm=Tiles.row_tile):
    m, k = a.shape
    n = w.shape[1]
    bm = min(bm, m)
    return pl.pallas_call(
        _proj_res_kernel,
        out_shape=jax.ShapeDtypeStruct((m, n), F32),
        grid=(m // bm,),
        in_specs=[pl.BlockSpec((bm, k), lambda i: (i, 0)), _const_spec(w.shape),
                  pl.BlockSpec((bm, n), lambda i: (i, 0))],
        out_specs=pl.BlockSpec((bm, n), lambda i: (i, 0)),
        compiler_params=_params("parallel"),
        name="proj_residual",
    )(a, w.astype(BF16), x)


def _affinity_kernel(x_ref, g_ref, wr_ref, gate_ref, hn_ref):
    h = _rms(x_ref[0], g_ref[...])
    hn_ref[0] = h.astype(hn_ref.dtype)
    h_hi, h_lo = _split_bf16(h)
    w_hi, w_lo = _split_bf16(wr_ref[...])
    logits = (lax.dot_general(w_hi, h_hi, NT_DIMS, preferred_element_type=F32)
              + lax.dot_general(w_hi, h_lo, NT_DIMS, preferred_element_type=F32)
              + lax.dot_general(w_lo, h_hi, NT_DIMS, preferred_element_type=F32))
    ex = jnp.exp(logits - jnp.max(logits, axis=0, keepdims=True))
    gate_ref[0] = ex / jnp.sum(ex, axis=0, keepdims=True)


def _select_kernel(aff_ref, pos_ref, *, cap):
    aff = aff_ref[...]
    ne, s = aff.shape
    bits = pltpu.bitcast(aff, jnp.int32)

    def count(mask):
        return jnp.sum(jnp.where(mask, 1.0, 0.0), axis=1, keepdims=True)

    def search(nbits, accept):
        value = jnp.zeros((ne, 1), jnp.int32)
        shift = nbits
        while shift > 0:
            width = 2 if shift % 2 == 0 else 1
            shift -= width
            digit = jnp.zeros((ne, 1), jnp.int32)
            for j in range(1, 1 << width):
                digit = digit + jnp.where(accept(value + (j << shift)), 1, 0)
            value = value + jnp.left_shift(digit, shift)
        return value

    thr = search(31, lambda cand: count(bits >= cand) >= cap)
    gt = bits > thr
    eq = bits == thr
    need = cap - count(gt)
    idx = lax.broadcasted_iota(jnp.int32, (ne, s), 1)
    cut = search(s.bit_length(), lambda cand: count(eq & (idx < cand)) <= need)
    sel = gt | (eq & (idx < cut))

    blk = LANES if s % LANES == 0 else s
    ri = lax.broadcasted_iota(jnp.int32, (blk, blk), 0)
    ci = lax.broadcasted_iota(jnp.int32, (blk, blk), 1)
    upper = jnp.where(ri <= ci, 1.0, 0.0).astype(BF16)
    off = jnp.zeros((ne, 1), F32)
    for j in range(s // blk):
        sl = slice(j * blk, (j + 1) * blk)
        sel_j = sel[:, sl]
        inc = jnp.dot(jnp.where(sel_j, 1.0, 0.0).astype(BF16), upper, preferred_element_type=F32)
        pos_ref[:, sl] = jnp.where(sel_j, (off + inc - 1.0).astype(jnp.int32), -1)
        off = off + inc[:, blk - 1:blk]


def route(x, norm_g, w_router, cap):
    bsz, s, d = x.shape
    ne = w_router.shape[1]
    out = pl.BlockSpec((1, ne, s), lambda b: (b, 0, 0))
    tok = pl.BlockSpec((1, s, d), lambda b: (b, 0, 0))
    gate, hn = pl.pallas_call(
        _affinity_kernel,
        out_shape=(jax.ShapeDtypeStruct((bsz, ne, s), F32), jax.ShapeDtypeStruct((bsz, s, d), BF16)),
        grid=(bsz,),
        in_specs=[tok, _const_spec((1, d)), _const_spec((ne, d))],
        out_specs=(out, tok),
        compiler_params=_params("parallel"),
        name="ec_affinity",
    )(x, norm_g.reshape(1, d), w_router.T)
    pos = pl.pallas_call(
        functools.partial(_select_kernel, cap=cap),
        out_shape=jax.ShapeDtypeStruct((bsz * ne, s), jnp.int32),
        grid=(1,),
        in_specs=[_const_spec((bsz * ne, s))],
        out_specs=_const_spec((bsz * ne, s)),
        compiler_params=_params("arbitrary"),
        name="ec_select",
    )(gate.reshape(bsz * ne, s))
    return pos.reshape(bsz, ne, s), gate, hn


def _selection(pos_row, cap):
    return pos_row == lax.broadcasted_iota(jnp.int32, (cap, pos_row.shape[1]), 0)


def _expert_ffn_kernel(h_ref, pos_ref, gate_ref, wg_ref, wu_ref, wd_ref, ye_ref, wg_s, wu_s, wd_s, *, cap):
    @pl.when(pl.program_id(1) == 0)
    def _():
        wg_s[...] = wg_ref[0, 0].astype(BF16)
        wu_s[...] = wu_ref[0, 0].astype(BF16)
        wd_s[...] = wd_ref[0, 0].astype(BF16)

    rows = range(h_ref.shape[0])
    onehot = [_selection(pos_ref[i, 0], cap) for i in rows]
    sel = [jnp.where(o, 1.0, 0.0).astype(BF16) for o in onehot]
    gcol = [jnp.sum(jnp.where(onehot[i], gate_ref[i, 0], 0.0), axis=1, keepdims=True) for i in rows]
    xe = [jnp.dot(sel[i], h_ref[i], preferred_element_type=F32).astype(BF16) for i in rows]
    hg = [jnp.dot(x_, wg_s[...], preferred_element_type=F32) for x_ in xe]
    hu = [jnp.dot(x_, wu_s[...], preferred_element_type=F32) for x_ in xe]
    hid = [(g_ * jax.nn.sigmoid(g_) * u_).astype(BF16) for g_, u_ in zip(hg, hu)]
    for i in rows:
        ye_ref[i, 0] = (jnp.dot(hid[i], wd_s[...], preferred_element_type=F32) * gcol[i]).astype(ye_ref.dtype)


def _expert_scatter_kernel(x_ref, pos_ref, ye_ref, o_ref, *, cap, eg):
    @pl.when(pl.program_id(1) == 0)
    def _():
        o_ref[...] = x_ref[...]

    sel = jnp.concatenate(
        [jnp.where(_selection(pos_ref[0, g], cap), 1.0, 0.0).astype(BF16) for g in range(eg)], axis=0)
    ye = ye_ref[0].reshape(eg * cap, ye_ref.shape[-1])
    o_ref[0] += lax.dot_general(sel, ye, TN_DIMS, preferred_element_type=F32)


def expert_choice_ffn(x, norm_g, w_router, w_gate, w_up, w_down, layer, eg=Tiles.scatter_experts):
    bsz, s, d = x.shape
    _, ne, _, ff = w_gate.shape
    cap = EC_CAPACITY * s // ne
    eg = min(eg, ne)
    pos, gate, hn = route(x, norm_g, w_router, cap)
    pos4 = pos.reshape(bsz, ne, 1, s)
    gate4 = gate.reshape(bsz, ne, 1, s)
    nb = min(Tiles.expert_batch_rows, bsz)
    assert bsz % nb == 0
    row = pl.BlockSpec((nb, 1, 1, s), lambda e, b: (b, e, 0, 0))
    wspec = lambda shape: pl.BlockSpec((1, 1) + shape, lambda e, b: (layer, e, 0, 0))
    ye = pl.pallas_call(
        functools.partial(_expert_ffn_kernel, cap=cap),
        out_shape=jax.ShapeDtypeStruct((bsz, ne, cap, d), BF16),
        grid=(ne, bsz // nb),
        in_specs=[pl.BlockSpec((nb, s, d), lambda e, b: (b, 0, 0)), row, row,
                  wspec((d, ff)), wspec((d, ff)), wspec((ff, d))],
        out_specs=pl.BlockSpec((nb, 1, cap, d), lambda e, b: (b, e, 0, 0)),
        scratch_shapes=[pltpu.VMEM((d, ff), BF16), pltpu.VMEM((d, ff), BF16), pltpu.VMEM((ff, d), BF16)],
        compiler_params=_params("arbitrary", "arbitrary"),
        name="ec_expert_ffn",
    )(hn, pos4, gate4, w_gate, w_up, w_down)
    tok = pl.BlockSpec((1, s, d), lambda b, j: (b, 0, 0))
    return pl.pallas_call(
        functools.partial(_expert_scatter_kernel, cap=cap, eg=eg),
        out_shape=jax.ShapeDtypeStruct((bsz, s, d), F32),
        grid=(bsz, ne // eg),
        in_specs=[tok, pl.BlockSpec((1, eg, 1, s), lambda b, j: (b, j, 0, 0)),
                  pl.BlockSpec((1, eg, cap, d), lambda b, j: (b, j, 0, 0))],
        out_specs=tok,
        compiler_params=_params("parallel", "arbitrary"),
        name="ec_expert_scatter",
    )(x, pos4, ye)


def rwkv7_layer(x, v_first, norm_g, mu, w_r, w_k, w_v, w_o, w0, w1, w2, a0, a1, a2,
                g1, g2, k_k, k_a, r_k, gn_g, gn_b, vres):
    bsz, s, d = x.shape
    r, na, v, g, bonus, lw, k_d, b_d, v_first = rwkv_pre(
        x, v_first, norm_g, mu, w_r, w_k, w_v, w0, w1, w2, a0, a1, a2, g1, g2, k_k, k_a, r_k, vres)
    yf = wkv7(r, lw[0], k_d[0], v, na, b_d[0], reverse=False)
    yb = wkv7(r, lw[1], k_d[1], v, na, b_d[1], reverse=True)
    flat = lambda t: t.reshape(bsz * s, d)
    out = rwkv_post(flat(yf), flat(yb), flat(bonus), flat(g), flat(x), gn_g, gn_b, w_o)
    return out.reshape(bsz, s, d), v_first


def rope_tables(positions):
    inv_freq = ROPE_THETA ** (-jnp.arange(0, QK_ROPE, 2, dtype=F32) / QK_ROPE)
    ang = positions.astype(F32)[..., None] * inv_freq
    cos, sin = jnp.cos(ang), jnp.sin(ang)
    shape = cos.shape[:-1]
    z = lambda n: jnp.zeros(shape + (n,), F32)
    tail = HEAD_PAD - QK_NOPE - QK_ROPE
    cf = jnp.concatenate([jnp.ones(shape + (QK_NOPE,), F32), cos, cos, z(tail)], -1)
    sf = jnp.concatenate([z(QK_NOPE), sin, sin, z(tail)], -1)
    return cf.reshape(-1, HEAD_PAD), sf.reshape(-1, HEAD_PAD)


def mla_layer(x, tables, norm_g, w_in, q_norm_g, kv_norm_g, w_uq, w_ukv, q_head_g, k_head_g, w_o):
    bsz, s, d = x.shape
    q, k, vt = mla_pre(x, tables, norm_g, w_in, q_norm_g, kv_norm_g, w_uq, w_ukv, q_head_g, k_head_g)
    o = attention(q, k, vt)
    return proj_residual(o.reshape(bsz * s, -1), w_o, x.reshape(bsz * s, d)).reshape(bsz, s, d)


def kernel(x, positions, norm_mix_g, norm_ffn_g,
           rw_mu, rw_wr, rw_wk, rw_wv, rw_wo, rw_w0, rw_w1, rw_w2,
           rw_a0, rw_a1, rw_a2, rw_g1, rw_g2, rw_kk, rw_ka, rw_rk,
           rw_gn_g, rw_gn_b, rw_v0, rw_v1, rw_v2,
           mla_w_in, mla_q_norm_g, mla_kv_norm_g, mla_w_uq, mla_w_ukv,
           mla_q_head_g, mla_k_head_g, mla_w_o,
           moe_router, moe_w_gate, moe_w_up, moe_w_down):
    depth = norm_mix_g.shape[0]
    n_mixers = 2
    tables = rope_tables(positions)
    v_first = None
    for i in range(depth):
        j = i // n_mixers
        if i % n_mixers == 0:
            vres = None if j == 0 else (rw_v0[j - 1], rw_v1[j - 1], rw_v2[j - 1])
            x, v_first = rwkv7_layer(
                x, v_first, norm_mix_g[i], rw_mu[j], rw_wr[j], rw_wk[j], rw_wv[j], rw_wo[j],
                rw_w0[j], rw_w1[j], rw_w2[j], rw_a0[j], rw_a1[j], rw_a2[j],
                rw_g1[j], rw_g2[j], rw_kk[j], rw_ka[j], rw_rk[j],
                rw_gn_g[j], rw_gn_b[j], vres)
        else:
            x = mla_layer(x, tables, norm_mix_g[i], mla_w_in[j], mla_q_norm_g[j], mla_kv_norm_g[j],
                          mla_w_uq[j], mla_w_ukv[j], mla_q_head_g[j], mla_k_head_g[j], mla_w_o[j])
        x = expert_choice_ffn(x, norm_ffn_g[i], moe_router[i], moe_w_gate, moe_w_up, moe_w_down, layer=i)
    return x
```

```python
import functools

import jax
import jax.numpy as jnp
from jax import lax
from jax.experimental import pallas as pl
from jax.experimental.pallas import tpu as pltpu

F32 = jnp.float32
BF16 = jnp.bfloat16

NORM_EPS = 1e-6
GN_EPS = 64e-5
RWKV_HEAD = 64
QK_NOPE = 64
QK_ROPE = 32
V_HEAD = 64
Q_LORA = 384
KV_LORA = 256
ROPE_THETA = 10000.0
EC_CAPACITY = 2
LANES = 128
SUBLANES = 8
HEAD_PAD = LANES
ONES_ROWS = 16
LOG2_E = 1.4426950408889634
DECAY_SCALE = 0.6065306597126334
WKV_CHUNK = 64
VMEM_LIMIT = 56 * 1024 * 1024


class Tiles:
    rwkv_pre_rows = 256
    mla_pre_rows = 512
    row_tile = 1024
    attn_queries = 2048
    attn_key_chunk = 128
    attn_heads = 4
    wkv_batch_rows = 2
    scatter_experts = 8
    expert_batch_rows = 2

NT_DIMS = (((1,), (1,)), ((), ()))
TN_DIMS = (((0,), (0,)), ((), ()))


def _dot(a, b):
    return jnp.dot(a.astype(BF16), b.astype(BF16), preferred_element_type=F32)


def _dot_nt(a, b):
    return lax.dot_general(a.astype(BF16), b.astype(BF16), NT_DIMS, preferred_element_type=F32)


def _split_bf16(x):
    hi = x.astype(BF16)
    lo = (x - hi.astype(F32)).astype(BF16)
    return hi, lo


def _rms(x, gain):
    return x * lax.rsqrt(jnp.mean(x * x, axis=-1, keepdims=True) + NORM_EPS) * gain


def _seg_sum(t, seg):
    width = min(LANES, t.shape[1])
    ri = lax.broadcasted_iota(jnp.int32, (width, width), 0)
    ci = lax.broadcasted_iota(jnp.int32, (width, width), 1)
    ones = jnp.where((ri // seg) == (ci // seg), 1.0, 0.0).astype(BF16)
    parts = [_dot(t[:, j:j + width], ones) for j in range(0, t.shape[1], width)]
    return parts[0] if len(parts) == 1 else jnp.concatenate(parts, axis=1)


def _pad_to(w, axis, mult):
    pad = -w.shape[axis] % mult
    if pad == 0:
        return w
    widths = [(0, 0)] * w.ndim
    widths[axis] = (0, pad)
    return jnp.pad(w, widths)


def _const_spec(shape):
    return pl.BlockSpec(shape, lambda *_: (0,) * len(shape))


def _params(*sem):
    return pltpu.CompilerParams(dimension_semantics=sem, vmem_limit_bytes=VMEM_LIMIT)


def _rwkv_pre_kernel(*refs, has_vres, seg):
    it = iter(refs)
    (x_ref, xp_ref, xn_ref, vec_ref, wr_ref, wk_ref, wv_ref,
     w1_ref, a1_ref, g1_ref, w2_ref, a2_ref, g2_ref) = [next(it) for _ in range(13)]
    if has_vres:
        v1_ref, v2_ref, vf_ref = next(it), next(it), next(it)
    r_o, na_o, v_o, g_o, bon_o = [next(it) for _ in range(5)]
    lw_o = [next(it), next(it)]
    k_o = [next(it), next(it)]
    b_o = [next(it), next(it)]
    if not has_vres:
        vf_o = next(it)

    i = pl.program_id(1)
    vec = vec_ref[...]
    row_of = lambda j: vec[j:j + 1]
    gain = row_of(0)
    x = x_ref[0]
    bm = x.shape[0]
    h = _rms(x, gain)
    hp = jnp.where(i == 0, 0.0, _rms(xp_ref[0][SUBLANES - 1:SUBLANES], gain))
    hn = jnp.where(i == pl.num_programs(1) - 1, 0.0, _rms(xn_ref[0][0:1], gain))
    row = lax.broadcasted_iota(jnp.int32, (bm, 1), 0)
    h_prev = jnp.where(row == 0, hp, pltpu.roll(h, 1, 0))
    h_next = jnp.where(row == bm - 1, hn, pltpu.roll(h, bm - 1, 0))
    xx = 0.5 * (h_prev + h_next) - h
    xr, xw, xk, xv, xa, xg = [(h + xx * row_of(1 + j)).astype(BF16) for j in range(6)]

    r = _dot(xr, wr_ref[...])
    k = _dot(xk, wk_ref[...])
    v = _dot(xv, wv_ref[...])
    if has_vres:
        vz = row_of(14) + _dot(_dot(xv, v1_ref[...]), v2_ref[...])
        v = v + (vf_ref[0] - v) * jax.nn.sigmoid(vz)
    else:
        vf_o[0] = v
    g = _dot(jax.nn.sigmoid(_dot(xg, g1_ref[...])), g2_ref[...])

    kkr = k * row_of(11)
    kk = kkr * lax.rsqrt(jnp.maximum(_seg_sum(kkr * kkr, seg), 1e-24))
    tw = jnp.tanh(_dot(xw, w1_ref[...])).astype(BF16)
    al = _dot(xa, a1_ref[...]).astype(BF16)
    k_keep = k * (1.0 - row_of(12))
    k_mix = k * row_of(12)
    k_ds = []
    for d in range(2):
        z = row_of(7 + d) + _dot(tw, w2_ref[d])
        lw_o[d][0] = -DECAY_SCALE * jax.nn.sigmoid(z)
        a = jax.nn.sigmoid(row_of(9 + d) + _dot(al, a2_ref[d]))
        k_d = k_keep + k_mix * a
        k_o[d][0] = k_d.astype(BF16)
        b_o[d][0] = (kk * a).astype(BF16)
        k_ds.append(k_d)
    bon_o[0] = (_seg_sum(r * (k_ds[0] + k_ds[1]) * row_of(13), seg) * v).astype(BF16)
    r_o[0] = r.astype(BF16)
    na_o[0] = (-kk).astype(BF16)
    v_o[0] = v.astype(BF16)
    g_o[0] = g.astype(BF16)


def rwkv_pre(x, v_first, norm_g, mu, w_r, w_k, w_v, w0, w1, w2, a0, a1, a2, g1, g2, k_k, k_a, r_k,
             vres, bm=Tiles.rwkv_pre_rows):
    bsz, s, d = x.shape
    bm = min(bm, s)
    nt = s // bm
    has_vres = vres is not None
    zero = jnp.zeros((d,), F32)
    v0 = vres[0] if has_vres else zero
    vec = jnp.stack([norm_g, *mu, w0[0], w0[1], a0[0], a0[1], k_k, k_a, r_k.reshape(d), v0, zero])
    bf = lambda t: t.astype(BF16)
    cat2 = lambda t: jnp.concatenate([t[0], t[1]], axis=1)
    lo = w1.shape[2]
    second = lambda t: jnp.stack([jnp.pad(t[0], ((0, lo), (0, 0))), jnp.pad(t[1], ((lo, 0), (0, 0)))])
    ins = [x, x, x, vec, bf(w_r), bf(w_k), bf(w_v), bf(cat2(w1)), bf(cat2(a1)), bf(_pad_to(g1, 1, LANES)),
           bf(second(w2)), bf(second(a2)), bf(_pad_to(g2, 0, LANES))]
    tile = pl.BlockSpec((1, bm, d), lambda b, i: (b, i, 0))
    hb = bm // SUBLANES
    in_specs = [tile,
                pl.BlockSpec((1, SUBLANES, d), lambda b, i: (b, jnp.maximum(i * hb - 1, 0), 0)),
                pl.BlockSpec((1, SUBLANES, d), lambda b, i: (b, jnp.minimum((i + 1) * hb, s // SUBLANES - 1), 0))]
    in_specs += [_const_spec(t.shape) for t in ins[3:]]
    if has_vres:
        extra = [bf(_pad_to(vres[1], 1, LANES)), bf(_pad_to(vres[2], 0, LANES))]
        ins += extra + [v_first]
        in_specs += [_const_spec(t.shape) for t in extra] + [tile]
    n_bf, n_f32 = 5, 2
    out_shape = ([jax.ShapeDtypeStruct((bsz, s, d), BF16)] * n_bf + [jax.ShapeDtypeStruct((bsz, s, d), F32)] * n_f32
                 + [jax.ShapeDtypeStruct((bsz, s, d), BF16)] * 4)
    if not has_vres:
        out_shape.append(jax.ShapeDtypeStruct((bsz, s, d), F32))
    outs = pl.pallas_call(
        functools.partial(_rwkv_pre_kernel, has_vres=has_vres, seg=RWKV_HEAD),
        out_shape=tuple(out_shape),
        grid=(bsz, nt),
        in_specs=in_specs,
        out_specs=tuple([tile] * len(out_shape)),
        compiler_params=_params("parallel", "parallel"),
        name="rwkv_pre",
    )(*ins)
    r, na, v, g, bonus, lw0, lw1, k0, k1, b0, b1 = outs[:11]
    v_first = v_first if has_vres else outs[11]
    return r, na, v, g, bonus, (lw0, lw1), (k0, k1), (b0, b1), v_first


def _wkv_kernel(r_ref, lw_ref, k_ref, v_ref, a_ref, b_ref, y_ref, h_scr, *, reverse, nh, hd):
    c = pl.program_id(1)

    @pl.when(c == 0)
    def _():
        h_scr[...] = jnp.zeros_like(h_scr)

    nb, L, _ = lw_ref.shape
    row = lax.broadcasted_iota(jnp.int32, (L, L), 0)
    col = lax.broadcasted_iota(jnp.int32, (L, L), 1)
    incl = (col >= row) if reverse else (col <= row)
    tri = jnp.where(incl, 1.0, 0.0).astype(BF16)
    eye = jnp.where(row == col, 1.0, 0.0).astype(F32)
    last = 0 if reverse else L - 1

    def scaled(bi):
        lw = lw_ref[bi]
        lw_hi, lw_lo = _split_bf16(lw)
        cs = (jnp.dot(tri, lw_hi, preferred_element_type=F32)
              + jnp.dot(tri, lw_lo, preferred_element_type=F32))
        ctot = cs[last:last + 1, :]
        r, k, v, a, b = [t[bi].astype(F32) for t in (r_ref, k_ref, v_ref, a_ref, b_ref)]
        einv = jnp.exp(-cs)
        etot = jnp.exp(ctot - cs)
        return dict(rt=r * jnp.exp(cs), at=a * jnp.exp(cs - lw), bt=b * einv, kt=k * einv,
                    bb=b * etot, kb=k * etot, v=v, wtot=jnp.exp(ctot))

    sc_in = [scaled(bi) for bi in range(nb)]

    row2 = lax.broadcasted_iota(jnp.int32, (2 * L, 2 * L), 0)
    col2 = lax.broadcasted_iota(jnp.int32, (2 * L, 2 * L), 1)
    rt2, cs2 = row2 & (L - 1), col2 & (L - 1)
    before = (cs2 > rt2) if reverse else (cs2 < rt2)
    score_mask = before | ((row2 >= L) & (cs2 == rt2))
    right_half = lax.broadcasted_iota(jnp.int32, (L, 2 * L), 1) >= L
    diag_blk = (row ^ col) < 2
    level_masks = []
    size = 2
    while size < L:
        x = row ^ col
        level_masks.append((x >= size) & (x < 2 * size))
        size *= 2

    T = 2 * hd
    lane_lo = lax.broadcasted_iota(jnp.int32, (1, T), 1) < hd
    rowp = lax.broadcasted_iota(jnp.int32, (T, T), 0)
    colp = lax.broadcasted_iota(jnp.int32, (T, T), 1)
    diag_blocks = (rowp < hd) == (colp < hd)
    eye_t = jnp.where(rowp == colp, 1.0, 0.0).astype(F32)
    tiles = [(bi, p) for bi in range(nb) for p in range(nh // 2)]
    units = range(len(tiles))
    heads = [(u, q) for u in units for q in range(2)]
    tile = lambda name, u: sc_in[tiles[u][0]][name][:, tiles[u][1] * T:(tiles[u][1] + 1) * T]
    own = lambda q: lane_lo if q == 0 else jnp.logical_not(lane_lo)

    a2 = [jnp.concatenate([tile("at", u), tile("rt", u)], 0) for u in units]
    b2 = [jnp.concatenate([tile("bt", u), tile("kt", u)], 0).astype(BF16) for u in units]
    v_sw = [pltpu.roll(tile("v", u), hd, 1) for u in units]
    vv_sw = [jnp.concatenate([t_.astype(BF16)] * 2, 0) for t_ in v_sw]
    bbkb_t = [jnp.concatenate([tile("bb", u), tile("kb", u)], 0).T.astype(BF16) for u in units]

    sc = [jnp.where(score_mask, _dot_nt(jnp.where(own(q), a2[u], 0.0), b2[u]), 0.0) for u, q in heads]
    top = [s_[:L] for s_ in sc]
    bot = [s_[L:] for s_ in sc]
    n_ab = [t_[:, :L] for t_ in top]
    t = [eye + jnp.where(diag_blk, n_, 0.0) for n_ in n_ab]
    for lm in level_masks:
        w = [_dot(jnp.where(lm, n_, 0.0), t_) for n_, t_ in zip(n_ab, t)]
        t = [t_ + _dot(t_, w_) for t_, w_ in zip(t, w)]
    x = [_dot(jnp.where(right_half, top[i], 0.0), vv_sw[u]) for i, (u, q) in enumerate(heads)]
    tz = [_dot(t[i], jnp.where(own(q), tile("at", u), x[i])) for i, (u, q) in enumerate(heads)]
    rhs = [jnp.concatenate([tz[i].astype(BF16), jnp.where(own(q), 0.0, v_sw[u]).astype(BF16)], 0)
           for i, (u, q) in enumerate(heads)]
    qp = [_dot(jnp.concatenate([bot[i].astype(BF16), bbkb_t[u][q * hd:(q + 1) * hd]], 0), rhs[i])
          for i, (u, q) in enumerate(heads)]
    for u in units:
        bi, p = tiles[u]
        q0, q1 = qp[2 * u], qp[2 * u + 1]
        rh = tile("rt", u) + jnp.where(lane_lo, q0[:L], q1[:L])
        yh = pltpu.roll(jnp.where(lane_lo, q1[:L], q0[:L]), hd, 1)
        prow = jnp.concatenate([q0[L:], q1[L:]], 0)
        m = jnp.where(diag_blocks, prow, 0.0) + eye_t * tile("wtot", u)
        g = pltpu.roll(jnp.where(diag_blocks, 0.0, prow), hd, 1)
        fin = _dot(jnp.concatenate([rh, m], 0), h_scr[bi, p])
        y_ref[bi, :, p * T:(p + 1) * T] = (fin[:L] + yh).astype(y_ref.dtype)
        h_scr[bi, p] = fin[L:] + g


def wkv7(r, lw, k, v, a, b, reverse, nb=Tiles.wkv_batch_rows):
    bsz, s, d = r.shape
    hd = RWKV_HEAD
    nh = d // hd
    L = min(WKV_CHUNK, s)
    nb = min(nb, bsz)
    assert L == hd and 2 * hd == LANES and nh % 2 == 0 and s % L == 0 and bsz % nb == 0
    nc = s // L
    if reverse:
        idx = lambda bi, ci: (bi, nc - 1 - ci, 0)
    else:
        idx = lambda bi, ci: (bi, ci, 0)
    spec = pl.BlockSpec((nb, L, d), idx)
    return pl.pallas_call(
        functools.partial(_wkv_kernel, reverse=reverse, nh=nh, hd=hd),
        out_shape=jax.ShapeDtypeStruct((bsz, s, d), BF16),
        grid=(bsz // nb, nc),
        in_specs=[spec] * 6,
        out_specs=spec,
        scratch_shapes=[pltpu.VMEM((nb, nh // 2, 2 * hd, 2 * hd), F32)],
        compiler_params=_params("parallel", "arbitrary"),
        name="wkv7_rev" if reverse else "wkv7_fwd",
    )(r, lw, k, v, a, b)


def _rwkv_post_kernel(yf_ref, yb_ref, bon_ref, g_ref, x_ref, vec_ref, wo_ref, o_ref, *, seg):
    y = yf_ref[...].astype(F32) + yb_ref[...].astype(F32)
    yc = y - _seg_sum(y, seg) * (1.0 / seg)
    var = _seg_sum(yc * yc, seg) * (1.0 / seg)
    vec = vec_ref[...]
    yn = yc * lax.rsqrt(var + GN_EPS) * vec[0:1] + vec[1:2] + bon_ref[...].astype(F32)
    o_ref[...] = x_ref[...] + _dot(yn * g_ref[...].astype(F32), wo_ref[...])


def rwkv_post(yf, yb, bonus, g, x, gn_g, gn_b, w_o, bm=Tiles.row_tile):
    m, d = x.shape
    bm = min(bm, m)
    vec = _pad_to(jnp.stack([gn_g, gn_b]), 0, SUBLANES)
    tile = pl.BlockSpec((bm, d), lambda i: (i, 0))
    return pl.pallas_call(
        functools.partial(_rwkv_post_kernel, seg=RWKV_HEAD),
        out_shape=jax.ShapeDtypeStruct((m, d), F32),
        grid=(m // bm,),
        in_specs=[tile] * 5 + [_const_spec(vec.shape), _const_spec(w_o.shape)],
        out_specs=tile,
        compiler_params=_params("parallel"),
        name="rwkv_post",
    )(yf, yb, bonus, g, x, vec, w_o.astype(BF16))


def _head_norm_rope(y2, ctab, stab, o_ref, nh):
    inv_dim = 1.0 / (QK_NOPE + QK_ROPE)
    for h in range(nh):
        sl = slice(h * HEAD_PAD, (h + 1) * HEAD_PAD)
        rh = slice((nh + h) * HEAD_PAD, (nh + h + 1) * HEAD_PAD)
        yh = y2[:, sl]
        ms = jnp.sum(yh * yh, axis=-1, keepdims=True) * inv_dim
        o_ref[0, :, sl] = ((yh * ctab + y2[:, rh] * stab) * lax.rsqrt(ms + NORM_EPS)).astype(o_ref.dtype)


def _mla_pre_kernel(x_ref, gx_ref, gc_ref, hg_ref, cf_ref, sf_ref, win_ref, wq_ref, wk_ref, wvt_ref,
                    q_ref, k_ref, vt_ref, *, nh):
    gc = gc_ref[...]
    h = _rms(x_ref[0], gx_ref[...])
    c = _dot(h, win_ref[...])
    c_q = _rms(c[:, :Q_LORA], gc[0:1])
    c_kv = _rms(c[:, Q_LORA:Q_LORA + KV_LORA], gc[1:2, :KV_LORA]).astype(BF16)
    cf, sf = cf_ref[...], sf_ref[...]
    hg = hg_ref[...]
    _head_norm_rope(_dot(c_q, wq_ref[...]), cf * hg[0:1], sf * hg[1:2], q_ref, nh)
    k_in = jnp.concatenate([c_kv, c[:, Q_LORA + KV_LORA:].astype(BF16)], axis=1)
    _head_norm_rope(_dot(k_in, wk_ref[...]), cf * hg[2:3], sf * hg[3:4], k_ref, nh)
    vt_ref[0] = _dot_nt(wvt_ref[...], c_kv).astype(vt_ref.dtype)


def _pad_heads(w, nh, width, take):
    k = w.shape[0]
    wh = w.reshape(k, nh, width)[:, :, take]
    wh = jnp.pad(wh, ((0, 0), (0, 0), (0, HEAD_PAD - wh.shape[-1])))
    return wh.reshape(k, nh * HEAD_PAD)


def _with_rotate_half(w_pad, nh):
    k = w_pad.shape[0]
    half = QK_ROPE // 2
    w3 = w_pad.reshape(k, nh, HEAD_PAD)
    x1 = w3[:, :, QK_NOPE:QK_NOPE + half]
    x2 = w3[:, :, QK_NOPE + half:QK_NOPE + QK_ROPE]
    z = lambda n: jnp.zeros((k, nh, n), w_pad.dtype)
    rh = jnp.concatenate([z(QK_NOPE), -x2, x1, z(HEAD_PAD - QK_NOPE - QK_ROPE)], axis=-1)
    return jnp.concatenate([w_pad, rh.reshape(k, nh * HEAD_PAD)], axis=1)


def _gain_rows(g):
    half = QK_ROPE // 2
    tail = jnp.zeros((HEAD_PAD - QK_NOPE - QK_ROPE,), F32)
    straight = jnp.concatenate([g, tail])
    swapped = jnp.concatenate([jnp.zeros((QK_NOPE,), F32), g[QK_NOPE + half:], g[QK_NOPE:QK_NOPE + half], tail])
    return [straight, swapped]


def mla_pre(x, tables, norm_g, w_in, q_norm_g, kv_norm_g, w_uq, w_ukv, q_head_g, k_head_g,
            bm=Tiles.mla_pre_rows):
    bsz, s, d = x.shape
    bm = min(bm, s)
    qk = QK_NOPE + QK_ROPE
    nh = w_uq.shape[1] // qk
    assert Q_LORA % LANES == 0 and KV_LORA % LANES == 0 and KV_LORA <= Q_LORA
    gx = norm_g.reshape(1, d)
    gc = _pad_to(jnp.stack([q_norm_g, jnp.pad(kv_norm_g, (0, Q_LORA - KV_LORA))]), 0, SUBLANES)
    hg = _pad_to(jnp.stack(_gain_rows(q_head_g * (qk ** -0.5 * LOG2_E)) + _gain_rows(k_head_g)), 0, SUBLANES)
    win = _pad_to(w_in, 1, LANES)
    rope_w = win.shape[1] - Q_LORA - KV_LORA
    wq = _with_rotate_half(_pad_heads(w_uq, nh, qk, slice(0, qk)), nh)
    wk = _pad_heads(w_ukv, nh, QK_NOPE + V_HEAD, slice(0, QK_NOPE))
    place = jnp.zeros((rope_w, nh, HEAD_PAD), F32)
    place = place.at[jnp.arange(QK_ROPE), :, QK_NOPE + jnp.arange(QK_ROPE)].set(1.0)
    wk = _with_rotate_half(jnp.concatenate([wk, place.reshape(rope_w, nh * HEAD_PAD)], 0), nh)
    wvt = w_ukv.reshape(KV_LORA, nh, QK_NOPE + V_HEAD)[:, :, QK_NOPE:].reshape(KV_LORA, nh * V_HEAD).T
    weights = [t.astype(BF16) for t in (win, wq, wk, wvt)]
    nt = s // bm
    tab = pl.BlockSpec((bm, HEAD_PAD), lambda b, i: (b * nt + i, 0))
    row = lambda n: pl.BlockSpec((1, bm, n), lambda b, i: (b, i, 0))
    return pl.pallas_call(
        functools.partial(_mla_pre_kernel, nh=nh),
        out_shape=(jax.ShapeDtypeStruct((bsz, s, nh * HEAD_PAD), BF16),
                   jax.ShapeDtypeStruct((bsz, s, nh * HEAD_PAD), BF16),
                   jax.ShapeDtypeStruct((bsz, nh * V_HEAD, s), BF16)),
        grid=(bsz, nt),
        in_specs=[row(d), _const_spec(gx.shape), _const_spec(gc.shape), _const_spec(hg.shape), tab, tab]
                 + [_const_spec(t.shape) for t in weights],
        out_specs=(row(nh * HEAD_PAD), row(nh * HEAD_PAD),
                   pl.BlockSpec((1, nh * V_HEAD, bm), lambda b, i: (b, 0, i))),
        compiler_params=_params("parallel", "parallel"),
        name="mla_pre",
    )(x, gx, gc, hg, *tables, *weights)


def _attn_kernel(q_ref, k_ref, vt_ref, o_ref, *, kc):
    s = k_ref.shape[1]
    n_heads = vt_ref.shape[1] // V_HEAD
    heads = range(n_heads)
    qs = [q_ref[0, :, j * HEAD_PAD:(j + 1) * HEAD_PAD] for j in heads]

    def scores(c):
        return [_dot_nt(k_ref[0, c * kc:(c + 1) * kc, j * HEAD_PAD:(j + 1) * HEAD_PAD], qs[j]) for j in heads]

    ones = jnp.ones((ONES_ROWS, kc), BF16)
    nxt = scores(0)
    m, acc = [None] * n_heads, [None] * n_heads
    for c in range(s // kc):
        cur = nxt
        if (c + 1) * kc < s:
            nxt = scores(c + 1)
        for j in heads:
            vt = jnp.concatenate([vt_ref[0, j * V_HEAD:(j + 1) * V_HEAD, c * kc:(c + 1) * kc], ones], axis=0)
            mc = jnp.max(cur[j], axis=0, keepdims=True)
            if c == 0:
                m[j] = mc
                acc[j] = jnp.dot(vt, jnp.exp2((cur[j] - mc).astype(BF16)), preferred_element_type=F32)
            else:
                m_new = jnp.maximum(m[j], mc)
                pt = jnp.exp2((cur[j] - m_new).astype(BF16))
                acc[j] = jnp.exp2(m[j] - m_new) * acc[j] + jnp.dot(vt, pt, preferred_element_type=F32)
                m[j] = m_new
    out = jnp.concatenate([acc[j][:V_HEAD] / acc[j][V_HEAD:V_HEAD + 1] for j in heads], axis=0)
    o_ref[0] = out.T.astype(o_ref.dtype)


def attention(q, k, vt, tq=Tiles.attn_queries, kc=Tiles.attn_key_chunk, hp=Tiles.attn_heads):
    bsz, s, _ = q.shape
    nh = vt.shape[1] // V_HEAD
    tq = min(tq, s)
    kc = min(kc, s)
    hp = min(hp, nh)
    assert nh % hp == 0 and (hp * V_HEAD) % LANES == 0
    return pl.pallas_call(
        functools.partial(_attn_kernel, kc=kc),
        out_shape=jax.ShapeDtypeStruct((bsz, s, nh * V_HEAD), BF16),
        grid=(bsz, nh // hp, s // tq),
        in_specs=[pl.BlockSpec((1, tq, hp * HEAD_PAD), lambda b, h, i: (b, i, h)),
                  pl.BlockSpec((1, s, hp * HEAD_PAD), lambda b, h, i: (b, 0, h)),
                  pl.BlockSpec((1, hp * V_HEAD, s), lambda b, h, i: (b, h, 0))],
        out_specs=pl.BlockSpec((1, tq, hp * V_HEAD), lambda b, h, i: (b, i, h)),
        compiler_params=_params("parallel", "parallel", "parallel"),
        name="mla_attention",
    )(q, k, vt)


def _proj_res_kernel(a_ref, w_ref, x_ref, o_ref):
    o_ref[...] = x_ref[...] + _dot(a_ref[...], w_ref[...])


def proj_residual(a, w, x, bm=Tiles.row_tile):
    m, k = a.shape
    n = w.shape[1]
    bm = min(bm, m)
    return pl.pallas_call(
        _proj_res_kernel,
        out_shape=jax.ShapeDtypeStruct((m, n), F32),
        grid=(m // bm,),
        in_specs=[pl.BlockSpec((bm, k), lambda i: (i, 0)), _const_spec(w.shape),
                  pl.BlockSpec((bm, n), lambda i: (i, 0))],
        out_specs=pl.BlockSpec((bm, n), lambda i: (i, 0)),
        compiler_params=_params("parallel"),
        name="proj_residual",
    )(a, w.astype(BF16), x)


def _affinity_kernel(x_ref, g_ref, wr_ref, gate_ref, hn_ref):
    h = _rms(x_ref[0], g_ref[...])
    hn_ref[0] = h.astype(hn_ref.dtype)
    h_hi, h_lo = _split_bf16(h)
    w_hi, w_lo = _split_bf16(wr_ref[...])
    logits = (lax.dot_general(w_hi, h_hi, NT_DIMS, preferred_element_type=F32)
              + lax.dot_general(w_hi, h_lo, NT_DIMS, preferred_element_type=F32)
              + lax.dot_general(w_lo, h_hi, NT_DIMS, preferred_element_type=F32))
    ex = jnp.exp(logits - jnp.max(logits, axis=0, keepdims=True))
    gate_ref[0] = ex / jnp.sum(ex, axis=0, keepdims=True)


def _select_kernel(aff_ref, pos_ref, *, cap):
    aff = aff_ref[...]
    ne, s = aff.shape
    bits = pltpu.bitcast(aff, jnp.int32)

    def count(mask):
        return jnp.sum(jnp.where(mask, 1.0, 0.0), axis=1, keepdims=True)

    def search(nbits, accept):
        value = jnp.zeros((ne, 1), jnp.int32)
        shift = nbits
        while shift > 0:
            width = 2 if shift % 2 == 0 else 1
            shift -= width
            digit = jnp.zeros((ne, 1), jnp.int32)
            for j in range(1, 1 << width):
                digit = digit + jnp.where(accept(value + (j << shift)), 1, 0)
            value = value + jnp.left_shift(digit, shift)
        return value

    thr = search(31, lambda cand: count(bits >= cand) >= cap)
    gt = bits > thr
    eq = bits == thr
    need = cap - count(gt)
    idx = lax.broadcasted_iota(jnp.int32, (ne, s), 1)
    cut = search(s.bit_length(), lambda cand: count(eq & (idx < cand)) <= need)
    sel = gt | (eq & (idx < cut))

    blk = LANES if s % LANES == 0 else s
    ri = lax.broadcasted_iota(jnp.int32, (blk, blk), 0)
    ci = lax.broadcasted_iota(jnp.int32, (blk, blk), 1)
    upper = jnp.where(ri <= ci, 1.0, 0.0).astype(BF16)
    off = jnp.zeros((ne, 1), F32)
    for j in range(s // blk):
        sl = slice(j * blk, (j + 1) * blk)
        sel_j = sel[:, sl]
        inc = jnp.dot(jnp.where(sel_j, 1.0, 0.0).astype(BF16), upper, preferred_element_type=F32)
        pos_ref[:, sl] = jnp.where(sel_j, (off + inc - 1.0).astype(jnp.int32), -1)
        off = off + inc[:, blk - 1:blk]


def route(x, norm_g, w_router, cap):
    bsz, s, d = x.shape
    ne = w_router.shape[1]
    out = pl.BlockSpec((1, ne, s), lambda b: (b, 0, 0))
    tok = pl.BlockSpec((1, s, d), lambda b: (b, 0, 0))
    gate, hn = pl.pallas_call(
        _affinity_kernel,
        out_shape=(jax.ShapeDtypeStruct((bsz, ne, s), F32), jax.ShapeDtypeStruct((bsz, s, d), BF16)),
        grid=(bsz,),
        in_specs=[tok, _const_spec((1, d)), _const_spec((ne, d))],
        out_specs=(out, tok),
        compiler_params=_params("parallel"),
        name="ec_affinity",
    )(x, norm_g.reshape(1, d), w_router.T)
    pos = pl.pallas_call(
        functools.partial(_select_kernel, cap=cap),
        out_shape=jax.ShapeDtypeStruct((bsz * ne, s), jnp.int32),
        grid=(1,),
        in_specs=[_const_spec((bsz * ne, s))],
        out_specs=_const_spec((bsz * ne, s)),
        compiler_params=_params("arbitrary"),
        name="ec_select",
    )(gate.reshape(bsz * ne, s))
    return pos.reshape(bsz, ne, s), gate, hn


def _selection(pos_row, cap):
    return pos_row == lax.broadcasted_iota(jnp.int32, (cap, pos_row.shape[1]), 0)


def _expert_ffn_kernel(h_ref, pos_ref, gate_ref, wg_ref, wu_ref, wd_ref, ye_ref, wg_s, wu_s, wd_s, *, cap):
    @pl.when(pl.program_id(1) == 0)
    def _():
        wg_s[...] = wg_ref[0, 0].astype(BF16)
        wu_s[...] = wu_ref[0, 0].astype(BF16)
        wd_s[...] = wd_ref[0, 0].astype(BF16)

    rows = range(h_ref.shape[0])
    onehot = [_selection(pos_ref[i, 0], cap) for i in rows]
    sel = [jnp.where(o, 1.0, 0.0).astype(BF16) for o in onehot]
    gcol = [jnp.sum(jnp.where(onehot[i], gate_ref[i, 0], 0.0), axis=1, keepdims=True) for i in rows]
    xe = [jnp.dot(sel[i], h_ref[i], preferred_element_type=F32).astype(BF16) for i in rows]
    hg = [jnp.dot(x_, wg_s[...], preferred_element_type=F32) for x_ in xe]
    hu = [jnp.dot(x_, wu_s[...], preferred_element_type=F32) for x_ in xe]
    hid = [(g_ * jax.nn.sigmoid(g_) * u_).astype(BF16) for g_, u_ in zip(hg, hu)]
    for i in rows:
        ye_ref[i, 0] = (jnp.dot(hid[i], wd_s[...], preferred_element_type=F32) * gcol[i]).astype(ye_ref.dtype)


def _expert_scatter_kernel(x_ref, pos_ref, ye_ref, o_ref, *, cap, eg):
    @pl.when(pl.program_id(1) == 0)
    def _():
        o_ref[...] = x_ref[...]

    sel = jnp.concatenate(
        [jnp.where(_selection(pos_ref[0, g], cap), 1.0, 0.0).astype(BF16) for g in range(eg)], axis=0)
    ye = ye_ref[0].reshape(eg * cap, ye_ref.shape[-1])
    o_ref[0] += lax.dot_general(sel, ye, TN_DIMS, preferred_element_type=F32)


def expert_choice_ffn(x, norm_g, w_router, w_gate, w_up, w_down, layer, eg=Tiles.scatter_experts):
    bsz, s, d = x.shape
    _, ne, _, ff = w_gate.shape
    cap = EC_CAPACITY * s // ne
    eg = min(eg, ne)
    pos, gate, hn = route(x, norm_g, w_router, cap)
    pos4 = pos.reshape(bsz, ne, 1, s)
    gate4 = gate.reshape(bsz, ne, 1, s)
    nb = min(Tiles.expert_batch_rows, bsz)
    assert bsz % nb == 0
    row = pl.BlockSpec((nb, 1, 1, s), lambda e, b: (b, e, 0, 0))
    wspec = lambda shape: pl.BlockSpec((1, 1) + shape, lambda e, b: (layer, e, 0, 0))
    ye = pl.pallas_call(
        functools.partial(_expert_ffn_kernel, cap=cap),
        out_shape=jax.ShapeDtypeStruct((bsz, ne, cap, d), BF16),
        grid=(ne, bsz // nb),
        in_specs=[pl.BlockSpec((nb, s, d), lambda e, b: (b, 0, 0)), row, row,
                  wspec((d, ff)), wspec((d, ff)), wspec((ff, d))],
        out_specs=pl.BlockSpec((nb, 1, cap, d), lambda e, b: (b, e, 0, 0)),
        scratch_shapes=[pltpu.VMEM((d, ff), BF16), pltpu.VMEM((d, ff), BF16), pltpu.VMEM((ff, d), BF16)],
        compiler_params=_params("arbitrary", "arbitrary"),
        name="ec_expert_ffn",
    )(hn, pos4, gate4, w_gate, w_up, w_down)
    tok = pl.BlockSpec((1, s, d), lambda b, j: (b, 0, 0))
    return pl.pallas_call(
        functools.partial(_expert_scatter_kernel, cap=cap, eg=eg),
        out_shape=jax.ShapeDtypeStruct((bsz, s, d), F32),
        grid=(bsz, ne // eg),
        in_specs=[tok, pl.BlockSpec((1, eg, 1, s), lambda b, j: (b, j, 0, 0)),
                  pl.BlockSpec((1, eg, cap, d), lambda b, j: (b, j, 0, 0))],
        out_specs=tok,
        compiler_params=_params("parallel", "arbitrary"),
        name="ec_expert_scatter",
    )(x, pos4, ye)


def rwkv7_layer(x, v_first, norm_g, mu, w_r, w_k, w_v, w_o, w0, w1, w2, a0, a1, a2,
                g1, g2, k_k, k_a, r_k, gn_g, gn_b, vres):
    bsz, s, d = x.shape
    r, na, v, g, bonus, lw, k_d, b_d, v_first = rwkv_pre(
        x, v_first, norm_g, mu, w_r, w_k, w_v, w0, w1, w2, a0, a1, a2, g1, g2, k_k, k_a, r_k, vres)
    yf = wkv7(r, lw[0], k_d[0], v, na, b_d[0], reverse=False)
    yb = wkv7(r, lw[1], k_d[1], v, na, b_d[1], reverse=True)
    flat = lambda t: t.reshape(bsz * s, d)
    out = rwkv_post(flat(yf), flat(yb), flat(bonus), flat(g), flat(x), gn_g, gn_b, w_o)
    return out.reshape(bsz, s, d), v_first


def rope_tables(positions):
    inv_freq = ROPE_THETA ** (-jnp.arange(0, QK_ROPE, 2, dtype=F32) / QK_ROPE)
    ang = positions.astype(F32)[..., None] * inv_freq
    cos, sin = jnp.cos(ang), jnp.sin(ang)
    shape = cos.shape[:-1]
    z = lambda n: jnp.zeros(shape + (n,), F32)
    tail = HEAD_PAD - QK_NOPE - QK_ROPE
    cf = jnp.concatenate([jnp.ones(shape + (QK_NOPE,), F32), cos, cos, z(tail)], -1)
    sf = jnp.concatenate([z(QK_NOPE), sin, sin, z(tail)], -1)
    return cf.reshape(-1, HEAD_PAD), sf.reshape(-1, HEAD_PAD)


def mla_layer(x, tables, norm_g, w_in, q_norm_g, kv_norm_g, w_uq, w_ukv, q_head_g, k_head_g, w_o):
    bsz, s, d = x.shape
    q, k, vt = mla_pre(x, tables, norm_g, w_in, q_norm_g, kv_norm_g, w_uq, w_ukv, q_head_g, k_head_g)
    o = attention(q, k, vt)
    return proj_residual(o.reshape(bsz * s, -1), w_o, x.reshape(bsz * s, d)).reshape(bsz, s, d)


def kernel(x, positions, norm_mix_g, norm_ffn_g,
           rw_mu, rw_wr, rw_wk, rw_wv, rw_wo, rw_w0, rw_w1, rw_w2,
           rw_a0, rw_a1, rw_a2, rw_g1, rw_g2, rw_kk, rw_ka, rw_rk,
           rw_gn_g, rw_gn_b, rw_v0, rw_v1, rw_v2,
           mla_w_in, mla_q_norm_g, mla_kv_norm_g, mla_w_uq, mla_w_ukv,
           mla_q_head_g, mla_k_head_g, mla_w_o,
           moe_router, moe_w_gate, moe_w_up, moe_w_down):
    depth = norm_mix_g.shape[0]
    n_mixers = 2
    tables = rope_tables(positions)
    v_first = None
    for i in range(depth):
        j = i // n_mixers
        if i % n_mixers == 0:
            vres = None if j == 0 else (rw_v0[j - 1], rw_v1[j - 1], rw_v2[j - 1])
            x, v_first = rwkv7_layer(
                x, v_first, norm_mix_g[i], rw_mu[j], rw_wr[j], rw_wk[j], rw_wv[j], rw_wo[j],
                rw_w0[j], rw_w1[j], rw_w2[j], rw_a0[j], rw_a1[j], rw_a2[j],
                rw_g1[j], rw_g2[j], rw_kk[j], rw_ka[j], rw_rk[j],
                rw_gn_g[j], rw_gn_b[j], vres)
        else:
            x = mla_layer(x, tables, norm_mix_g[i], mla_w_in[j], mla_q_norm_g[j], mla_kv_norm_g[j],
                          mla_w_uq[j], mla_w_ukv[j], mla_q_head_g[j], mla_k_head_g[j], mla_w_o[j])
        x = expert_choice_ffn(x, norm_ffn_g[i], moe_router[i], moe_w_gate, moe_w_up, moe_w_down, layer=i)
    return x
```

```python
import functools

import jax
import jax.numpy as jnp
from jax import lax
from jax.experimental import pallas as pl
from jax.experimental.pallas import tpu as pltpu

F32 = jnp.float32
BF16 = jnp.bfloat16

NORM_EPS = 1e-6
GN_EPS = 64e-5
RWKV_HEAD = 64
QK_NOPE = 64
QK_ROPE = 32
V_HEAD = 64
Q_LORA = 384
KV_LORA = 256
ROPE_THETA = 10000.0
EC_CAPACITY = 2
LANES = 128
SUBLANES = 8
HEAD_PAD = LANES
ONES_ROWS = 16
LOG2_E = 1.4426950408889634
DECAY_SCALE = 0.6065306597126334
WKV_CHUNK = 64
VMEM_LIMIT = 56 * 1024 * 1024


class Tiles:
    rwkv_pre_rows = 256
    mla_pre_rows = 512
    row_tile = 1024
    attn_queries = 2048
    attn_key_chunk = 128
    attn_heads = 4
    wkv_batch_rows = 2
    scatter_experts = 8
    expert_batch_rows = 2

NT_DIMS = (((1,), (1,)), ((), ()))
TN_DIMS = (((0,), (0,)), ((), ()))


def _dot(a, b):
    return jnp.dot(a.astype(BF16), b.astype(BF16), preferred_element_type=F32)


def _dot_nt(a, b):
    return lax.dot_general(a.astype(BF16), b.astype(BF16), NT_DIMS, preferred_element_type=F32)


def _split_bf16(x):
    hi = x.astype(BF16)
    lo = (x - hi.astype(F32)).astype(BF16)
    return hi, lo


def _rms(x, gain):
    return x * lax.rsqrt(jnp.mean(x * x, axis=-1, keepdims=True) + NORM_EPS) * gain


def _seg_sum(t, seg):
    width = min(LANES, t.shape[1])
    ri = lax.broadcasted_iota(jnp.int32, (width, width), 0)
    ci = lax.broadcasted_iota(jnp.int32, (width, width), 1)
    ones = jnp.where((ri // seg) == (ci // seg), 1.0, 0.0).astype(BF16)
    parts = [_dot(t[:, j:j + width], ones) for j in range(0, t.shape[1], width)]
    return parts[0] if len(parts) == 1 else jnp.concatenate(parts, axis=1)


def _pad_to(w, axis, mult):
    pad = -w.shape[axis] % mult
    if pad == 0:
        return w
    widths = [(0, 0)] * w.ndim
    widths[axis] = (0, pad)
    return jnp.pad(w, widths)


def _const_spec(shape):
    return pl.BlockSpec(shape, lambda *_: (0,) * len(shape))


def _params(*sem):
    return pltpu.CompilerParams(dimension_semantics=sem, vmem_limit_bytes=VMEM_LIMIT)


def _rwkv_pre_kernel(*refs, has_vres, seg):
    it = iter(refs)
    (x_ref, xp_ref, xn_ref, vec_ref, wr_ref, wk_ref, wv_ref,
     w1_ref, a1_ref, g1_ref, w2_ref, a2_ref, g2_ref) = [next(it) for _ in range(13)]
    if has_vres:
        v1_ref, v2_ref, vf_ref = next(it), next(it), next(it)
    r_o, na_o, v_o, g_o, bon_o = [next(it) for _ in range(5)]
    lw_o = [next(it), next(it)]
    k_o = [next(it), next(it)]
    b_o = [next(it), next(it)]
    if not has_vres:
        vf_o = next(it)

    i = pl.program_id(1)
    vec = vec_ref[...]
    row_of = lambda j: vec[j:j + 1]
    gain = row_of(0)
    x = x_ref[0]
    bm = x.shape[0]
    h = _rms(x, gain)
    hp = jnp.where(i == 0, 0.0, _rms(xp_ref[0][SUBLANES - 1:SUBLANES], gain))
    hn = jnp.where(i == pl.num_programs(1) - 1, 0.0, _rms(xn_ref[0][0:1], gain))
    row = lax.broadcasted_iota(jnp.int32, (bm, 1), 0)
    h_prev = jnp.where(row == 0, hp, pltpu.roll(h, 1, 0))
    h_next = jnp.where(row == bm - 1, hn, pltpu.roll(h, bm - 1, 0))
    xx = 0.5 * (h_prev + h_next) - h
    xr, xw, xk, xv, xa, xg = [(h + xx * row_of(1 + j)).astype(BF16) for j in range(6)]

    r = _dot(xr, wr_ref[...])
    k = _dot(xk, wk_ref[...])
    v = _dot(xv, wv_ref[...])
    if has_vres:
        vz = row_of(14) + _dot(_dot(xv, v1_ref[...]), v2_ref[...])
        v = v + (vf_ref[0] - v) * jax.nn.sigmoid(vz)
    else:
        vf_o[0] = v
    g = _dot(jax.nn.sigmoid(_dot(xg, g1_ref[...])), g2_ref[...])

    kkr = k * row_of(11)
    kk = kkr * lax.rsqrt(jnp.maximum(_seg_sum(kkr * kkr, seg), 1e-24))
    tw = jnp.tanh(_dot(xw, w1_ref[...])).astype(BF16)
    al = _dot(xa, a1_ref[...]).astype(BF16)
    k_keep = k * (1.0 - row_of(12))
    k_mix = k * row_of(12)
    k_ds = []
    for d in range(2):
        z = row_of(7 + d) + _dot(tw, w2_ref[d])
        lw_o[d][0] = -DECAY_SCALE * jax.nn.sigmoid(z)
        a = jax.nn.sigmoid(row_of(9 + d) + _dot(al, a2_ref[d]))
        k_d = k_keep + k_mix * a
        k_o[d][0] = k_d.astype(BF16)
        b_o[d][0] = (kk * a).astype(BF16)
        k_ds.append(k_d)
    bon_o[0] = (_seg_sum(r * (k_ds[0] + k_ds[1]) * row_of(13), seg) * v).astype(BF16)
    r_o[0] = r.astype(BF16)
    na_o[0] = (-kk).astype(BF16)
    v_o[0] = v.astype(BF16)
    g_o[0] = g.astype(BF16)


def rwkv_pre(x, v_first, norm_g, mu, w_r, w_k, w_v, w0, w1, w2, a0, a1, a2, g1, g2, k_k, k_a, r_k,
             vres, bm=Tiles.rwkv_pre_rows):
    bsz, s, d = x.shape
    bm = min(bm, s)
    nt = s // bm
    has_vres = vres is not None
    zero = jnp.zeros((d,), F32)
    v0 = vres[0] if has_vres else zero
    vec = jnp.stack([norm_g, *mu, w0[0], w0[1], a0[0], a0[1], k_k, k_a, r_k.reshape(d), v0, zero])
    bf = lambda t: t.astype(BF16)
    cat2 = lambda t: jnp.concatenate([t[0], t[1]], axis=1)
    lo = w1.shape[2]
    second = lambda t: jnp.stack([jnp.pad(t[0], ((0, lo), (0, 0))), jnp.pad(t[1], ((lo, 0), (0, 0)))])
    ins = [x, x, x, vec, bf(w_r), bf(w_k), bf(w_v), bf(cat2(w1)), bf(cat2(a1)), bf(_pad_to(g1, 1, LANES)),
           bf(second(w2)), bf(second(a2)), bf(_pad_to(g2, 0, LANES))]
    tile = pl.BlockSpec((1, bm, d), lambda b, i: (b, i, 0))
    hb = bm // SUBLANES
    in_specs = [tile,
                pl.BlockSpec((1, SUBLANES, d), lambda b, i: (b, jnp.maximum(i * hb - 1, 0), 0)),
                pl.BlockSpec((1, SUBLANES, d), lambda b, i: (b, jnp.minimum((i + 1) * hb, s // SUBLANES - 1), 0))]
    in_specs += [_const_spec(t.shape) for t in ins[3:]]
    if has_vres:
        extra = [bf(_pad_to(vres[1], 1, LANES)), bf(_pad_to(vres[2], 0, LANES))]
        ins += extra + [v_first]
        in_specs += [_const_spec(t.shape) for t in extra] + [tile]
    n_bf, n_f32 = 5, 2
    out_shape = ([jax.ShapeDtypeStruct((bsz, s, d), BF16)] * n_bf + [jax.ShapeDtypeStruct((bsz, s, d), F32)] * n_f32
                 + [jax.ShapeDtypeStruct((bsz, s, d), BF16)] * 4)
    if not has_vres:
        out_shape.append(jax.ShapeDtypeStruct((bsz, s, d), F32))
    outs = pl.pallas_call(
        functools.partial(_rwkv_pre_kernel, has_vres=has_vres, seg=RWKV_HEAD),
        out_shape=tuple(out_shape),
        grid=(bsz, nt),
        in_specs=in_specs,
        out_specs=tuple([tile] * len(out_shape)),
        compiler_params=_params("parallel", "parallel"),
        name="rwkv_pre",
    )(*ins)
    r, na, v, g, bonus, lw0, lw1, k0, k1, b0, b1 = outs[:11]
    v_first = v_first if has_vres else outs[11]
    return r, na, v, g, bonus, (lw0, lw1), (k0, k1), (b0, b1), v_first


def _wkv_kernel(r_ref, lw_ref, k_ref, v_ref, a_ref, b_ref, y_ref, h_scr, *, reverse, nh, hd):
    c = pl.program_id(1)

    @pl.when(c == 0)
    def _():
        h_scr[...] = jnp.zeros_like(h_scr)

    nb, L, _ = lw_ref.shape
    row = lax.broadcasted_iota(jnp.int32, (L, L), 0)
    col = lax.broadcasted_iota(jnp.int32, (L, L), 1)
    incl = (col >= row) if reverse else (col <= row)
    tri = jnp.where(incl, 1.0, 0.0).astype(BF16)
    eye = jnp.where(row == col, 1.0, 0.0).astype(F32)
    last = 0 if reverse else L - 1

    def scaled(bi, p):
        lanes = slice(p * 2 * hd, (p + 1) * 2 * hd)
        lw = lw_ref[bi, :, lanes]
        lw_hi, lw_lo = _split_bf16(lw)
        cs = (jnp.dot(tri, lw_hi, preferred_element_type=F32)
              + jnp.dot(tri, lw_lo, preferred_element_type=F32))
        ctot = cs[last:last + 1, :]
        r, k, v, a, b = [t[bi, :, lanes].astype(F32) for t in (r_ref, k_ref, v_ref, a_ref, b_ref)]
        einv = jnp.exp(-cs)
        etot = jnp.exp(ctot - cs)
        return dict(rt=r * jnp.exp(cs), at=a * jnp.exp(cs - lw), bt=b * einv, kt=k * einv,
                    bb=b * etot, kb=k * etot, v=v, wtot=jnp.exp(ctot))

    sc_in = [scaled(bi, p) for bi in range(nb) for p in range(nh // 2)]

    row2 = lax.broadcasted_iota(jnp.int32, (2 * L, 2 * L), 0)
    col2 = lax.broadcasted_iota(jnp.int32, (2 * L, 2 * L), 1)
    rt2, cs2 = row2 & (L - 1), col2 & (L - 1)
    before = (cs2 > rt2) if reverse else (cs2 < rt2)
    score_mask = before | ((row2 >= L) & (cs2 == rt2))
    right_half = lax.broadcasted_iota(jnp.int32, (L, 2 * L), 1) >= L
    diag_blk = (row ^ col) < 2
    level_masks = []
    size = 2
    while size < L:
        x = row ^ col
        level_masks.append((x >= size) & (x < 2 * size))
        size *= 2

    T = 2 * hd
    lane_lo = lax.broadcasted_iota(jnp.int32, (1, T), 1) < hd
    rowp = lax.broadcasted_iota(jnp.int32, (T, T), 0)
    colp = lax.broadcasted_iota(jnp.int32, (T, T), 1)
    diag_blocks = (rowp < hd) == (colp < hd)
    eye_t = jnp.where(rowp == colp, 1.0, 0.0).astype(F32)
    tiles = [(bi, p) for bi in range(nb) for p in range(nh // 2)]
    units = range(len(tiles))
    heads = [(u, q) for u in units for q in range(2)]
    tile = lambda name, u: sc_in[u][name]
    own = lambda q: lane_lo if q == 0 else jnp.logical_not(lane_lo)

    a2 = [jnp.concatenate([tile("at", u), tile("rt", u)], 0) for u in units]
    b2 = [jnp.concatenate([tile("bt", u), tile("kt", u)], 0).astype(BF16) for u in units]
    v_sw = [pltpu.roll(tile("v", u), hd, 1) for u in units]
    vv_sw = [jnp.concatenate([t_.astype(BF16)] * 2, 0) for t_ in v_sw]
    bbkb_t = [jnp.concatenate([tile("bb", u), tile("kb", u)], 0).T.astype(BF16) for u in units]

    sc = [jnp.where(score_mask, _dot_nt(jnp.where(own(q), a2[u], 0.0), b2[u]), 0.0) for u, q in heads]
    top = [s_[:L] for s_ in sc]
    bot = [s_[L:] for s_ in sc]
    n_ab = [t_[:, :L] for t_ in top]
    t = [eye + jnp.where(diag_blk, n_, 0.0) for n_ in n_ab]
    for lm in level_masks:
        w = [_dot(jnp.where(lm, n_, 0.0), t_) for n_, t_ in zip(n_ab, t)]
        t = [t_ + _dot(t_, w_) for t_, w_ in zip(t, w)]
    x = [_dot(jnp.where(right_half, top[i], 0.0), vv_sw[u]) for i, (u, q) in enumerate(heads)]
    tz = [_dot(t[i], jnp.where(own(q), tile("at", u), x[i])) for i, (u, q) in enumerate(heads)]
    rhs = [jnp.concatenate([tz[i].astype(BF16), jnp.where(own(q), 0.0, v_sw[u]).astype(BF16)], 0)
           for i, (u, q) in enumerate(heads)]
    qp = [_dot(jnp.concatenate([bot[i].astype(BF16), bbkb_t[u][q * hd:(q + 1) * hd]], 0), rhs[i])
          for i, (u, q) in enumerate(heads)]
    for u in units:
        bi, p = tiles[u]
        q0, q1 = qp[2 * u], qp[2 * u + 1]
        rh = tile("rt", u) + jnp.where(lane_lo, q0[:L], q1[:L])
        yh = pltpu.roll(jnp.where(lane_lo, q1[:L], q0[:L]), hd, 1)
        prow = jnp.concatenate([q0[L:], q1[L:]], 0)
        m = jnp.where(diag_blocks, prow, 0.0) + eye_t * tile("wtot", u)
        g = pltpu.roll(jnp.where(diag_blocks, 0.0, prow), hd, 1)
        fin = _dot(jnp.concatenate([rh, m], 0), h_scr[bi, p])
        y_ref[bi, :, p * T:(p + 1) * T] = (fin[:L] + yh).astype(y_ref.dtype)
        h_scr[bi, p] = fin[L:] + g


def wkv7(r, lw, k, v, a, b, reverse, nb=Tiles.wkv_batch_rows):
    bsz, s, d = r.shape
    hd = RWKV_HEAD
    nh = d // hd
    L = min(WKV_CHUNK, s)
    nb = min(nb, bsz)
    assert L == hd and 2 * hd == LANES and nh % 2 == 0 and s % L == 0 and bsz % nb == 0
    nc = s // L
    if reverse:
        idx = lambda bi, ci: (bi, nc - 1 - ci, 0)
    else:
        idx = lambda bi, ci: (bi, ci, 0)
    spec = pl.BlockSpec((nb, L, d), idx)
    return pl.pallas_call(
        functools.partial(_wkv_kernel, reverse=reverse, nh=nh, hd=hd),
        out_shape=jax.ShapeDtypeStruct((bsz, s, d), BF16),
        grid=(bsz // nb, nc),
        in_specs=[spec] * 6,
        out_specs=spec,
        scratch_shapes=[pltpu.VMEM((nb, nh // 2, 2 * hd, 2 * hd), F32)],
        compiler_params=_params("parallel", "arbitrary"),
        name="wkv7_rev" if reverse else "wkv7_fwd",
    )(r, lw, k, v, a, b)


def _rwkv_post_kernel(yf_ref, yb_ref, bon_ref, g_ref, x_ref, vec_ref, wo_ref, o_ref, *, seg):
    y = yf_ref[...].astype(F32) + yb_ref[...].astype(F32)
    yc = y - _seg_sum(y, seg) * (1.0 / seg)
    var = _seg_sum(yc * yc, seg) * (1.0 / seg)
    vec = vec_ref[...]
    yn = yc * lax.rsqrt(var + GN_EPS) * vec[0:1] + vec[1:2] + bon_ref[...].astype(F32)
    o_ref[...] = x_ref[...] + _dot(yn * g_ref[...].astype(F32), wo_ref[...])


def rwkv_post(yf, yb, bonus, g, x, gn_g, gn_b, w_o, bm=Tiles.row_tile):
    m, d = x.shape
    bm = min(bm, m)
    vec = _pad_to(jnp.stack([gn_g, gn_b]), 0, SUBLANES)
    tile = pl.BlockSpec((bm, d), lambda i: (i, 0))
    return pl.pallas_call(
        functools.partial(_rwkv_post_kernel, seg=RWKV_HEAD),
        out_shape=jax.ShapeDtypeStruct((m, d), F32),
        grid=(m // bm,),
        in_specs=[tile] * 5 + [_const_spec(vec.shape), _const_spec(w_o.shape)],
        out_specs=tile,
        compiler_params=_params("parallel"),
        name="rwkv_post",
    )(yf, yb, bonus, g, x, vec, w_o.astype(BF16))


def _head_norm_rope(y2, ctab, stab, o_ref, nh):
    inv_dim = 1.0 / (QK_NOPE + QK_ROPE)
    for h in range(nh):
        sl = slice(h * HEAD_PAD, (h + 1) * HEAD_PAD)
        rh = slice((nh + h) * HEAD_PAD, (nh + h + 1) * HEAD_PAD)
        yh = y2[:, sl]
        ms = jnp.sum(yh * yh, axis=-1, keepdims=True) * inv_dim
        o_ref[0, :, sl] = ((yh * ctab + y2[:, rh] * stab) * lax.rsqrt(ms + NORM_EPS)).astype(o_ref.dtype)


def _mla_pre_kernel(x_ref, gx_ref, gc_ref, hg_ref, cf_ref, sf_ref, win_ref, wq_ref, wk_ref, wvt_ref,
                    q_ref, k_ref, vt_ref, *, nh):
    gc = gc_ref[...]
    h = _rms(x_ref[0], gx_ref[...])
    c = _dot(h, win_ref[...])
    c_q = _rms(c[:, :Q_LORA], gc[0:1])
    c_kv = _rms(c[:, Q_LORA:Q_LORA + KV_LORA], gc[1:2, :KV_LORA]).astype(BF16)
    cf, sf = cf_ref[...], sf_ref[...]
    hg = hg_ref[...]
    _head_norm_rope(_dot(c_q, wq_ref[...]), cf * hg[0:1], sf * hg[1:2], q_ref, nh)
    k_in = jnp.concatenate([c_kv, c[:, Q_LORA + KV_LORA:].astype(BF16)], axis=1)
    _head_norm_rope(_dot(k_in, wk_ref[...]), cf * hg[2:3], sf * hg[3:4], k_ref, nh)
    vt_ref[0] = _dot_nt(wvt_ref[...], c_kv).astype(vt_ref.dtype)


def _pad_heads(w, nh, width, take):
    k = w.shape[0]
    wh = w.reshape(k, nh, width)[:, :, take]
    wh = jnp.pad(wh, ((0, 0), (0, 0), (0, HEAD_PAD - wh.shape[-1])))
    return wh.reshape(k, nh * HEAD_PAD)


def _with_rotate_half(w_pad, nh):
    k = w_pad.shape[0]
    half = QK_ROPE // 2
    w3 = w_pad.reshape(k, nh, HEAD_PAD)
    x1 = w3[:, :, QK_NOPE:QK_NOPE + half]
    x2 = w3[:, :, QK_NOPE + half:QK_NOPE + QK_ROPE]
    z = lambda n: jnp.zeros((k, nh, n), w_pad.dtype)
    rh = jnp.concatenate([z(QK_NOPE), -x2, x1, z(HEAD_PAD - QK_NOPE - QK_ROPE)], axis=-1)
    return jnp.concatenate([w_pad, rh.reshape(k, nh * HEAD_PAD)], axis=1)


def _gain_rows(g):
    half = QK_ROPE // 2
    tail = jnp.zeros((HEAD_PAD - QK_NOPE - QK_ROPE,), F32)
    straight = jnp.concatenate([g, tail])
    swapped = jnp.concatenate([jnp.zeros((QK_NOPE,), F32), g[QK_NOPE + half:], g[QK_NOPE:QK_NOPE + half], tail])
    return [straight, swapped]


def mla_pre(x, tables, norm_g, w_in, q_norm_g, kv_norm_g, w_uq, w_ukv, q_head_g, k_head_g,
            bm=Tiles.mla_pre_rows):
    bsz, s, d = x.shape
    bm = min(bm, s)
    qk = QK_NOPE + QK_ROPE
    nh = w_uq.shape[1] // qk
    assert Q_LORA % LANES == 0 and KV_LORA % LANES == 0 and KV_LORA <= Q_LORA
    gx = norm_g.reshape(1, d)
    gc = _pad_to(jnp.stack([q_norm_g, jnp.pad(kv_norm_g, (0, Q_LORA - KV_LORA))]), 0, SUBLANES)
    hg = _pad_to(jnp.stack(_gain_rows(q_head_g * (qk ** -0.5 * LOG2_E)) + _gain_rows(k_head_g)), 0, SUBLANES)
    win = _pad_to(w_in, 1, LANES)
    rope_w = win.shape[1] - Q_LORA - KV_LORA
    wq = _with_rotate_half(_pad_heads(w_uq, nh, qk, slice(0, qk)), nh)
    wk = _pad_heads(w_ukv, nh, QK_NOPE + V_HEAD, slice(0, QK_NOPE))
    place = jnp.zeros((rope_w, nh, HEAD_PAD), F32)
    place = place.at[jnp.arange(QK_ROPE), :, QK_NOPE + jnp.arange(QK_ROPE)].set(1.0)
    wk = _with_rotate_half(jnp.concatenate([wk, place.reshape(rope_w, nh * HEAD_PAD)], 0), nh)
    wvt = w_ukv.reshape(KV_LORA, nh, QK_NOPE + V_HEAD)[:, :, QK_NOPE:].reshape(KV_LORA, nh * V_HEAD).T
    weights = [t.astype(BF16) for t in (win, wq, wk, wvt)]
    nt = s // bm
    tab = pl.BlockSpec((bm, HEAD_PAD), lambda b, i: (b * nt + i, 0))
    row = lambda n: pl.BlockSpec((1, bm, n), lambda b, i: (b, i, 0))
    return pl.pallas_call(
        functools.partial(_mla_pre_kernel, nh=nh),
        out_shape=(jax.ShapeDtypeStruct((bsz, s, nh * HEAD_PAD), BF16),
                   jax.ShapeDtypeStruct((bsz, s, nh * HEAD_PAD), BF16),
                   jax.ShapeDtypeStruct((bsz, nh * V_HEAD, s), BF16)),
        grid=(bsz, nt),
        in_specs=[row(d), _const_spec(gx.shape), _const_spec(gc.shape), _const_spec(hg.shape), tab, tab]
                 + [_const_spec(t.shape) for t in weights],
        out_specs=(row(nh * HEAD_PAD), row(nh * HEAD_PAD),
                   pl.BlockSpec((1, nh * V_HEAD, bm), lambda b, i: (b, 0, i))),
        compiler_params=_params("parallel", "parallel"),
        name="mla_pre",
    )(x, gx, gc, hg, *tables, *weights)


def _attn_kernel(q_ref, k_ref, vt_ref, o_ref, *, kc):
    s = k_ref.shape[1]
    n_heads = vt_ref.shape[1] // V_HEAD
    heads = range(n_heads)
    qs = [q_ref[0, :, j * HEAD_PAD:(j + 1) * HEAD_PAD] for j in heads]

    def scores(c):
        return [_dot_nt(k_ref[0, c * kc:(c + 1) * kc, j * HEAD_PAD:(j + 1) * HEAD_PAD], qs[j]) for j in heads]

    ones = jnp.ones((ONES_ROWS, kc), BF16)
    nxt = scores(0)
    m, acc = [None] * n_heads, [None] * n_heads
    for c in range(s // kc):
        cur = nxt
        if (c + 1) * kc < s:
            nxt = scores(c + 1)
        for j in heads:
            vt = jnp.concatenate([vt_ref[0, j * V_HEAD:(j + 1) * V_HEAD, c * kc:(c + 1) * kc], ones], axis=0)
            mc = jnp.max(cur[j], axis=0, keepdims=True)
            if c == 0:
                m[j] = mc
                acc[j] = jnp.dot(vt, jnp.exp2((cur[j] - mc).astype(BF16)), preferred_element_type=F32)
            else:
                m_new = jnp.maximum(m[j], mc)
                pt = jnp.exp2((cur[j] - m_new).astype(BF16))
                acc[j] = jnp.exp2(m[j] - m_new) * acc[j] + jnp.dot(vt, pt, preferred_element_type=F32)
                m[j] = m_new
    out = jnp.concatenate([acc[j][:V_HEAD] / acc[j][V_HEAD:V_HEAD + 1] for j in heads], axis=0)
    o_ref[0] = out.T.astype(o_ref.dtype)


def attention(q, k, vt, tq=Tiles.attn_queries, kc=Tiles.attn_key_chunk, hp=Tiles.attn_heads):
    bsz, s, _ = q.shape
    nh = vt.shape[1] // V_HEAD
    tq = min(tq, s)
    kc = min(kc, s)
    hp = min(hp, nh)
    assert nh % hp == 0 and (hp * V_HEAD) % LANES == 0
    return pl.pallas_call(
        functools.partial(_attn_kernel, kc=kc),
        out_shape=jax.ShapeDtypeStruct((bsz, s, nh * V_HEAD), BF16),
        grid=(bsz, nh // hp, s // tq),
        in_specs=[pl.BlockSpec((1, tq, hp * HEAD_PAD), lambda b, h, i: (b, i, h)),
                  pl.BlockSpec((1, s, hp * HEAD_PAD), lambda b, h, i: (b, 0, h)),
                  pl.BlockSpec((1, hp * V_HEAD, s), lambda b, h, i: (b, h, 0))],
        out_specs=pl.BlockSpec((1, tq, hp * V_HEAD), lambda b, h, i: (b, i, h)),
        compiler_params=_params("parallel", "parallel", "parallel"),
        name="mla_attention",
    )(q, k, vt)


def _proj_res_kernel(a_ref, w_ref, x_ref, o_ref):
    o_ref[...] = x_ref[...] + _dot(a_ref[...], w_ref[...])


def proj_residual(a, w, x, bm=Tiles.row_tile):
    m, k = a.shape
    n = w.shape[1]
    bm = min(bm, m)
    return pl.pallas_call(
        _proj_res_kernel,
        out_shape=jax.ShapeDtypeStruct((m, n), F32),
        grid=(m // bm,),
        in_specs=[pl.BlockSpec((bm, k), lambda i: (i, 0)), _const_spec(w.shape),
                  pl.BlockSpec((bm, n), lambda i: (i, 0))],
        out_specs=pl.BlockSpec((bm, n), lambda i: (i, 0)),
        compiler_params=_params("parallel"),
        name="proj_residual",
    )(a, w.astype(BF16), x)


def _affinity_kernel(x_ref, g_ref, wr_ref, gate_ref, hn_ref):
    h = _rms(x_ref[0], g_ref[...])
    hn_ref[0] = h.astype(hn_ref.dtype)
    h_hi, h_lo = _split_bf16(h)
    w_hi, w_lo = _split_bf16(wr_ref[...])
    logits = (lax.dot_general(w_hi, h_hi, NT_DIMS, preferred_element_type=F32)
              + lax.dot_general(w_hi, h_lo, NT_DIMS, preferred_element_type=F32)
              + lax.dot_general(w_lo, h_hi, NT_DIMS, preferred_element_type=F32))
    ex = jnp.exp(logits - jnp.max(logits, axis=0, keepdims=True))
    gate_ref[0] = ex / jnp.sum(ex, axis=0, keepdims=True)


def _select_kernel(aff_ref, pos_ref, *, cap):
    aff = aff_ref[...]
    ne, s = aff.shape
    bits = pltpu.bitcast(aff, jnp.int32)

    def count(mask):
        return jnp.sum(jnp.where(mask, 1.0, 0.0), axis=1, keepdims=True)

    def search(nbits, accept):
        value = jnp.zeros((ne, 1), jnp.int32)
        shift = nbits
        while shift > 0:
            width = 2 if shift % 2 == 0 else 1
            shift -= width
            digit = jnp.zeros((ne, 1), jnp.int32)
            for j in range(1, 1 << width):
                digit = digit + jnp.where(accept(value + (j << shift)), 1, 0)
            value = value + jnp.left_shift(digit, shift)
        return value

    thr = search(31, lambda cand: count(bits >= cand) >= cap)
    gt = bits > thr
    eq = bits == thr
    need = cap - count(gt)
    idx = lax.broadcasted_iota(jnp.int32, (ne, s), 1)
    cut = search(s.bit_length(), lambda cand: count(eq & (idx < cand)) <= need)
    sel = gt | (eq & (idx < cut))

    blk = LANES if s % LANES == 0 else s
    ri = lax.broadcasted_iota(jnp.int32, (blk, blk), 0)
    ci = lax.broadcasted_iota(jnp.int32, (blk, blk), 1)
    upper = jnp.where(ri <= ci, 1.0, 0.0).astype(BF16)
    off = jnp.zeros((ne, 1), F32)
    for j in range(s // blk):
        sl = slice(j * blk, (j + 1) * blk)
        sel_j = sel[:, sl]
        inc = jnp.dot(jnp.where(sel_j, 1.0, 0.0).astype(BF16), upper, preferred_element_type=F32)
        pos_ref[:, sl] = jnp.where(sel_j, (off + inc - 1.0).astype(jnp.int32), -1)
        off = off + inc[:, blk - 1:blk]


def route(x, norm_g, w_router, cap):
    bsz, s, d = x.shape
    ne = w_router.shape[1]
    out = pl.BlockSpec((1, ne, s), lambda b: (b, 0, 0))
    tok = pl.BlockSpec((1, s, d), lambda b: (b, 0, 0))
    gate, hn = pl.pallas_call(
        _affinity_kernel,
        out_shape=(jax.ShapeDtypeStruct((bsz, ne, s), F32), jax.ShapeDtypeStruct((bsz, s, d), BF16)),
        grid=(bsz,),
        in_specs=[tok, _const_spec((1, d)), _const_spec((ne, d))],
        out_specs=(out, tok),
        compiler_params=_params("parallel"),
        name="ec_affinity",
    )(x, norm_g.reshape(1, d), w_router.T)
    pos = pl.pallas_call(
        functools.partial(_select_kernel, cap=cap),
        out_shape=jax.ShapeDtypeStruct((bsz * ne, s), jnp.int32),
        grid=(1,),
        in_specs=[_const_spec((bsz * ne, s))],
        out_specs=_const_spec((bsz * ne, s)),
        compiler_params=_params("arbitrary"),
        name="ec_select",
    )(gate.reshape(bsz * ne, s))
    return pos.reshape(bsz, ne, s), gate, hn


def _selection(pos_row, cap):
    return pos_row == lax.broadcasted_iota(jnp.int32, (cap, pos_row.shape[1]), 0)


def _expert_ffn_kernel(h_ref, pos_ref, gate_ref, wg_ref, wu_ref, wd_ref, ye_ref, wg_s, wu_s, wd_s, *, cap):
    @pl.when(pl.program_id(1) == 0)
    def _():
        wg_s[...] = wg_ref[0, 0].astype(BF16)
        wu_s[...] = wu_ref[0, 0].astype(BF16)
        wd_s[...] = wd_ref[0, 0].astype(BF16)

    rows = range(h_ref.shape[0])
    onehot = [_selection(pos_ref[i, 0], cap) for i in rows]
    sel = [jnp.where(o, 1.0, 0.0).astype(BF16) for o in onehot]
    gcol = [jnp.sum(jnp.where(onehot[i], gate_ref[i, 0], 0.0), axis=1, keepdims=True) for i in rows]
    xe = [jnp.dot(sel[i], h_ref[i], preferred_element_type=F32).astype(BF16) for i in rows]
    hg = [jnp.dot(x_, wg_s[...], preferred_element_type=F32) for x_ in xe]
    hu = [jnp.dot(x_, wu_s[...], preferred_element_type=F32) for x_ in xe]
    hid = [(g_ * jax.nn.sigmoid(g_) * u_).astype(BF16) for g_, u_ in zip(hg, hu)]
    for i in rows:
        ye_ref[i, 0] = (jnp.dot(hid[i], wd_s[...], preferred_element_type=F32) * gcol[i]).astype(ye_ref.dtype)


def _expert_scatter_kernel(x_ref, pos_ref, ye_ref, o_ref, *, cap, eg):
    @pl.when(pl.program_id(1) == 0)
    def _():
        o_ref[...] = x_ref[...]

    sel = jnp.concatenate(
        [jnp.where(_selection(pos_ref[0, g], cap), 1.0, 0.0).astype(BF16) for g in range(eg)], axis=0)
    ye = ye_ref[0].reshape(eg * cap, ye_ref.shape[-1])
    o_ref[0] += lax.dot_general(sel, ye, TN_DIMS, preferred_element_type=F32)


def expert_choice_ffn(x, norm_g, w_router, w_gate, w_up, w_down, layer, eg=Tiles.scatter_experts):
    bsz, s, d = x.shape
    _, ne, _, ff = w_gate.shape
    cap = EC_CAPACITY * s // ne
    eg = min(eg, ne)
    pos, gate, hn = route(x, norm_g, w_router, cap)
    pos4 = pos.reshape(bsz, ne, 1, s)
    gate4 = gate.reshape(bsz, ne, 1, s)
    nb = min(Tiles.expert_batch_rows, bsz)
    assert bsz % nb == 0
    row = pl.BlockSpec((nb, 1, 1, s), lambda e, b: (b, e, 0, 0))
    wspec = lambda shape: pl.BlockSpec((1, 1) + shape, lambda e, b: (layer, e, 0, 0))
    ye = pl.pallas_call(
        functools.partial(_expert_ffn_kernel, cap=cap),
        out_shape=jax.ShapeDtypeStruct((bsz, ne, cap, d), BF16),
        grid=(ne, bsz // nb),
        in_specs=[pl.BlockSpec((nb, s, d), lambda e, b: (b, 0, 0)), row, row,
                  wspec((d, ff)), wspec((d, ff)), wspec((ff, d))],
        out_specs=pl.BlockSpec((nb, 1, cap, d), lambda e, b: (b, e, 0, 0)),
        scratch_shapes=[pltpu.VMEM((d, ff), BF16), pltpu.VMEM((d, ff), BF16), pltpu.VMEM((ff, d), BF16)],
        compiler_params=_params("arbitrary", "arbitrary"),
        name="ec_expert_ffn",
    )(hn, pos4, gate4, w_gate, w_up, w_down)
    tok = pl.BlockSpec((1, s, d), lambda b, j: (b, 0, 0))
    return pl.pallas_call(
        functools.partial(_expert_scatter_kernel, cap=cap, eg=eg),
        out_shape=jax.ShapeDtypeStruct((bsz, s, d), F32),
        grid=(bsz, ne // eg),
        in_specs=[tok, pl.BlockSpec((1, eg, 1, s), lambda b, j: (b, j, 0, 0)),
                  pl.BlockSpec((1, eg, cap, d), lambda b, j: (b, j, 0, 0))],
        out_specs=tok,
        compiler_params=_params("parallel", "arbitrary"),
        name="ec_expert_scatter",
    )(x, pos4, ye)


def rwkv7_layer(x, v_first, norm_g, mu, w_r, w_k, w_v, w_o, w0, w1, w2, a0, a1, a2,
                g1, g2, k_k, k_a, r_k, gn_g, gn_b, vres):
    bsz, s, d = x.shape
    r, na, v, g, bonus, lw, k_d, b_d, v_first = rwkv_pre(
        x, v_first, norm_g, mu, w_r, w_k, w_v, w0, w1, w2, a0, a1, a2, g1, g2, k_k, k_a, r_k, vres)
    yf = wkv7(r, lw[0], k_d[0], v, na, b_d[0], reverse=False)
    yb = wkv7(r, lw[1], k_d[1], v, na, b_d[1], reverse=True)
    flat = lambda t: t.reshape(bsz * s, d)
    out = rwkv_post(flat(yf), flat(yb), flat(bonus), flat(g), flat(x), gn_g, gn_b, w_o)
    return out.reshape(bsz, s, d), v_first


def rope_tables(positions):
    inv_freq = ROPE_THETA ** (-jnp.arange(0, QK_ROPE, 2, dtype=F32) / QK_ROPE)
    ang = positions.astype(F32)[..., None] * inv_freq
    cos, sin = jnp.cos(ang), jnp.sin(ang)
    shape = cos.shape[:-1]
    z = lambda n: jnp.zeros(shape + (n,), F32)
    tail = HEAD_PAD - QK_NOPE - QK_ROPE
    cf = jnp.concatenate([jnp.ones(shape + (QK_NOPE,), F32), cos, cos, z(tail)], -1)
    sf = jnp.concatenate([z(QK_NOPE), sin, sin, z(tail)], -1)
    return cf.reshape(-1, HEAD_PAD), sf.reshape(-1, HEAD_PAD)


def mla_layer(x, tables, norm_g, w_in, q_norm_g, kv_norm_g, w_uq, w_ukv, q_head_g, k_head_g, w_o):
    bsz, s, d = x.shape
    q, k, vt = mla_pre(x, tables, norm_g, w_in, q_norm_g, kv_norm_g, w_uq, w_ukv, q_head_g, k_head_g)
    o = attention(q, k, vt)
    return proj_residual(o.reshape(bsz * s, -1), w_o, x.reshape(bsz * s, d)).reshape(bsz, s, d)


def kernel(x, positions, norm_mix_g, norm_ffn_g,
           rw_mu, rw_wr, rw_wk, rw_wv, rw_wo, rw_w0, rw_w1, rw_w2,
           rw_a0, rw_a1, rw_a2, rw_g1, rw_g2, rw_kk, rw_ka, rw_rk,
           rw_gn_g, rw_gn_b, rw_v0, rw_v1, rw_v2,
           mla_w_in, mla_q_norm_g, mla_kv_norm_g, mla_w_uq, mla_w_ukv,
           mla_q_head_g, mla_k_head_g, mla_w_o,
           moe_router, moe_w_gate, moe_w_up, moe_w_down):
    depth = norm_mix_g.shape[0]
    n_mixers = 2
    tables = rope_tables(positions)
    v_first = None
    for i in range(depth):
        j = i // n_mixers
        if i % n_mixers == 0:
            vres = None if j == 0 else (rw_v0[j - 1], rw_v1[j - 1], rw_v2[j - 1])
            x, v_first = rwkv7_layer(
                x, v_first, norm_mix_g[i], rw_mu[j], rw_wr[j], rw_wk[j], rw_wv[j], rw_wo[j],
                rw_w0[j], rw_w1[j], rw_w2[j], rw_a0[j], rw_a1[j], rw_a2[j],
                rw_g1[j], rw_g2[j], rw_kk[j], rw_ka[j], rw_rk[j],
                rw_gn_g[j], rw_gn_b[j], vres)
        else:
            x = mla_layer(x, tables, norm_mix_g[i], mla_w_in[j], mla_q_norm_g[j], mla_kv_norm_g[j],
                          mla_w_uq[j], mla_w_ukv[j], mla_q_head_g[j], mla_k_head_g[j], mla_w_o[j])
        x = expert_choice_ffn(x, norm_ffn_g[i], moe_router[i], moe_w_gate, moe_w_up, moe_w_down, layer=i)
    return x
```
